```python
import math
import jax, jax.numpy as jnp
from jax import lax
import numpy as np

D_MODEL = 1024
BATCH = 8
SEQ = 4096
DEPTH = 2

CHUNK = 64
Q_BLOCK = 128
SB_HEADS = 16
SB_HEAD_DIM = D_MODEL // SB_HEADS
SB_WIDTH = SB_HEADS * SB_HEAD_DIM
CONV_WIDTH = D_MODEL
CONV_K = 3
N_BRANCH = 2
FFN_HIDDEN = -(-8 * D_MODEL // (3 * 256)) * 256
IN_WIDTH = 3 * SB_WIDTH + 3 * CONV_WIDTH + N_BRANCH * D_MODEL
EPS = 1e-6

kernel_name = "stickbreak_shortconv_griffin_adaln_block"


def rmsnorm(x, g):
    xf = x.astype(jnp.float32)
    y = xf * lax.rsqrt(jnp.mean(xf * xf, axis=-1, keepdims=True) + EPS)
    return (y * g.astype(jnp.float32)).astype(x.dtype)


def stick_breaking_attention(q, k, v):
    b, s_len, h, dh = q.shape
    qh = jnp.transpose(q, (0, 2, 1, 3)).astype(jnp.float32)
    kh = jnp.transpose(k, (0, 2, 1, 3)).astype(jnp.float32)
    vh = jnp.transpose(v, (0, 2, 1, 3)).astype(jnp.float32)
    inv_sqrt = 1.0 / math.sqrt(dh)
    outs = []
    for i in range(s_len // Q_BLOCK):
        t0 = i * Q_BLOCK
        n_keys = t0 + Q_BLOCK
        qb = qh[:, :, t0:t0 + Q_BLOCK]
        kp = kh[:, :, :n_keys]
        vp = vh[:, :, :n_keys]
        z = jnp.einsum('bhqd,bhkd->bhqk', qb, kp) * inv_sqrt
        t_idx = t0 + jnp.arange(Q_BLOCK)[:, None]
        s_idx = jnp.arange(n_keys)[None, :]
        mask = s_idx < t_idx
        log_not = jnp.where(mask, jax.nn.log_sigmoid(-z), 0.0)
        excl = lax.cumsum(log_not, axis=3, reverse=True) - log_not
        log_a = jax.nn.log_sigmoid(z) + excl
        a = jnp.where(mask, jnp.exp(log_a), 0.0)
        outs.append(jnp.einsum('bhqk,bhkd->bhqd', a, vp))
    o = jnp.concatenate(outs, axis=2)
    o = jnp.transpose(o, (0, 2, 1, 3)).reshape(b, s_len, h * dh)
    return o.astype(q.dtype)


def causal_dwconv(x, w):
    ch = x.shape[-1]
    return lax.conv_general_dilated(
        x, w[:, None, :].astype(x.dtype), window_strides=(1,),
        padding=[(CONV_K - 1, 0)], dimension_numbers=('NWC', 'WIO', 'NWC'),
        feature_group_count=ch)


def _fwd_setup_inputs(seed: int = 0) -> dict:
    key = jax.random.key(seed)
    ks = jax.random.split(key, 18)
    f32 = jnp.float32

    def nrm(k, shape, fan_in):
        return jax.random.normal(k, shape, f32) * (fan_in ** -0.5)

    def gain(k, shape):
        return 1.0 + 0.02 * jax.random.normal(k, shape, f32)

    return {
        "x": jax.random.normal(ks[0], (BATCH, SEQ, D_MODEL), f32),
        "c": jax.random.normal(ks[1], (BATCH, D_MODEL), f32),
        "ada_w": nrm(ks[2], (DEPTH, D_MODEL, 6 * D_MODEL), D_MODEL),
        "ada_b": 0.02 * jax.random.normal(ks[3], (DEPTH, 6 * D_MODEL), f32),
        "ln1_g": gain(ks[4], (DEPTH, D_MODEL)),
        "w_in": nrm(ks[5], (DEPTH, D_MODEL, IN_WIDTH), D_MODEL),
        "q_norm_g": gain(ks[6], (DEPTH, SB_HEAD_DIM)),
        "k_norm_g": gain(ks[7], (DEPTH, SB_HEAD_DIM)),
        "conv_w": nrm(ks[8], (DEPTH, CONV_K, CONV_WIDTH), CONV_K),
        "w_branch_a": nrm(ks[9], (DEPTH, SB_WIDTH, D_MODEL), SB_WIDTH),
        "w_branch_b": nrm(ks[10], (DEPTH, CONV_WIDTH, D_MODEL), CONV_WIDTH),
        "w_out": nrm(ks[11], (DEPTH, D_MODEL, D_MODEL), D_MODEL),
        "ln2_g": gain(ks[12], (DEPTH, D_MODEL)),
        "w_ffn_gate": nrm(ks[13], (DEPTH, D_MODEL, FFN_HIDDEN), D_MODEL),
        "w_ffn_up": nrm(ks[14], (DEPTH, D_MODEL, FFN_HIDDEN), D_MODEL),
        "w_ffn_down": nrm(ks[15], (DEPTH, FFN_HIDDEN, D_MODEL), FFN_HIDDEN),
    }


def _fwd_reference(x, c, ada_w, ada_b, ln1_g, w_in, q_norm_g, k_norm_g, conv_w,
              w_branch_a, w_branch_b, w_out, ln2_g, w_ffn_gate, w_ffn_up, w_ffn_down):
    b, s_len, d = x.shape
    split_at = np.cumsum([SB_WIDTH, SB_WIDTH, SB_WIDTH,
                          CONV_WIDTH, CONV_WIDTH, CONV_WIDTH, D_MODEL])
    c_act = jax.nn.silu(c)
    for l in range(DEPTH):
        mod = c_act @ ada_w[l] + ada_b[l]
        sh1, sc1, g1, sh2, sc2, g2 = [m[:, None, :] for m in jnp.split(mod, 6, axis=-1)]

        h = rmsnorm(x, ln1_g[l]) * (1.0 + sc1) + sh1
        p = h @ w_in[l]
        q, k, v, cb, cc, cx, ga, gb = jnp.split(p, split_at, axis=-1)
        q = rmsnorm(q.reshape(b, s_len, SB_HEADS, SB_HEAD_DIM), q_norm_g[l])
        k = rmsnorm(k.reshape(b, s_len, SB_HEADS, SB_HEAD_DIM), k_norm_g[l])
        v = v.reshape(b, s_len, SB_HEADS, SB_HEAD_DIM)
        y_a = stick_breaking_attention(q, k, v)
        y_b = cb * causal_dwconv(cc * cx, conv_w[l])
        merged = (jax.nn.sigmoid(ga) * (y_a @ w_branch_a[l])
                  + jax.nn.sigmoid(gb) * (y_b @ w_branch_b[l]))
        x = x + g1 * (merged @ w_out[l])

        h = rmsnorm(x, ln2_g[l]) * (1.0 + sc2) + sh2
        f = (jax.nn.silu(h @ w_ffn_gate[l]) * (h @ w_ffn_up[l])) @ w_ffn_down[l]
        x = x + g2 * f
    return x


import jax as _jax
import jax.numpy as _jnp

TWIN_FORMAT = 'train_step'
FWD_PARAMS = ['x', 'c', 'ada_w', 'ada_b', 'ln1_g', 'w_in', 'q_norm_g', 'k_norm_g', 'conv_w', 'w_branch_a', 'w_branch_b', 'w_out', 'ln2_g', 'w_ffn_gate', 'w_ffn_up', 'w_ffn_down']
TWIN_WEIGHTS = ['ada_w', 'ada_b', 'ln1_g', 'w_in', 'q_norm_g', 'k_norm_g', 'conv_w', 'w_branch_a', 'w_branch_b', 'w_out', 'ln2_g', 'w_ffn_gate', 'w_ffn_up', 'w_ffn_down']
TWIN_DIFF_INPUT = 'x'
TWIN_INPUTS = ['x', 'c', 'ada_w', 'ada_b', 'ln1_g', 'w_in', 'q_norm_g', 'k_norm_g', 'conv_w', 'w_branch_a', 'w_branch_b', 'w_out', 'ln2_g', 'w_ffn_gate', 'w_ffn_up', 'w_ffn_down', 'loss_target', 'm_ada_w', 'm_ada_b', 'm_ln1_g', 'm_w_in', 'm_q_norm_g', 'm_k_norm_g', 'm_conv_w', 'm_w_branch_a', 'm_w_branch_b', 'm_w_out', 'm_ln2_g', 'm_w_ffn_gate', 'm_w_ffn_up', 'm_w_ffn_down', 'v_ada_w', 'v_ada_b', 'v_ln1_g', 'v_w_in', 'v_q_norm_g', 'v_k_norm_g', 'v_conv_w', 'v_w_branch_a', 'v_w_branch_b', 'v_w_out', 'v_ln2_g', 'v_w_ffn_gate', 'v_w_ffn_up', 'v_w_ffn_down']
TWIN_OUTPUTS = ['loss', 'grad_x', 'grad_ada_w', 'grad_ada_b', 'grad_ln1_g', 'grad_w_in', 'grad_q_norm_g', 'grad_k_norm_g', 'grad_conv_w', 'grad_w_branch_a', 'grad_w_branch_b', 'grad_w_out', 'grad_ln2_g', 'grad_w_ffn_gate', 'grad_w_ffn_up', 'grad_w_ffn_down', 'delta_ada_w', 'delta_ada_b', 'delta_ln1_g', 'delta_w_in', 'delta_q_norm_g', 'delta_k_norm_g', 'delta_conv_w', 'delta_w_branch_a', 'delta_w_branch_b', 'delta_w_out', 'delta_ln2_g', 'delta_w_ffn_gate', 'delta_w_ffn_up', 'delta_w_ffn_down', 'new_m_ada_w', 'new_m_ada_b', 'new_m_ln1_g', 'new_m_w_in', 'new_m_q_norm_g', 'new_m_k_norm_g', 'new_m_conv_w', 'new_m_w_branch_a', 'new_m_w_branch_b', 'new_m_w_out', 'new_m_ln2_g', 'new_m_w_ffn_gate', 'new_m_w_ffn_up', 'new_m_w_ffn_down', 'new_v_ada_w', 'new_v_ada_b', 'new_v_ln1_g', 'new_v_w_in', 'new_v_q_norm_g', 'new_v_k_norm_g', 'new_v_conv_w', 'new_v_w_branch_a', 'new_v_w_branch_b', 'new_v_w_out', 'new_v_ln2_g', 'new_v_w_ffn_gate', 'new_v_w_ffn_up', 'new_v_w_ffn_down']
TWIN_LEAF_KINDS = {'loss': 'loss', 'grad_x': 'grad_x', 'grad_ada_w': 'grad_w', 'grad_ada_b': 'grad_w', 'grad_ln1_g': 'grad_w', 'grad_w_in': 'grad_w', 'grad_q_norm_g': 'grad_w', 'grad_k_norm_g': 'grad_w', 'grad_conv_w': 'grad_w', 'grad_w_branch_a': 'grad_w', 'grad_w_branch_b': 'grad_w', 'grad_w_out': 'grad_w', 'grad_ln2_g': 'grad_w', 'grad_w_ffn_gate': 'grad_w', 'grad_w_ffn_up': 'grad_w', 'grad_w_ffn_down': 'grad_w', 'delta_ada_w': 'delta_w', 'delta_ada_b': 'delta_w', 'delta_ln1_g': 'delta_w', 'delta_w_in': 'delta_w', 'delta_q_norm_g': 'delta_w', 'delta_k_norm_g': 'delta_w', 'delta_conv_w': 'delta_w', 'delta_w_branch_a': 'delta_w', 'delta_w_branch_b': 'delta_w', 'delta_w_out': 'delta_w', 'delta_ln2_g': 'delta_w', 'delta_w_ffn_gate': 'delta_w', 'delta_w_ffn_up': 'delta_w', 'delta_w_ffn_down': 'delta_w', 'new_m_ada_w': 'new_m', 'new_m_ada_b': 'new_m', 'new_m_ln1_g': 'new_m', 'new_m_w_in': 'new_m', 'new_m_q_norm_g': 'new_m', 'new_m_k_norm_g': 'new_m', 'new_m_conv_w': 'new_m', 'new_m_w_branch_a': 'new_m', 'new_m_w_branch_b': 'new_m', 'new_m_w_out': 'new_m', 'new_m_ln2_g': 'new_m', 'new_m_w_ffn_gate': 'new_m', 'new_m_w_ffn_up': 'new_m', 'new_m_w_ffn_down': 'new_m', 'new_v_ada_w': 'new_v', 'new_v_ada_b': 'new_v', 'new_v_ln1_g': 'new_v', 'new_v_w_in': 'new_v', 'new_v_q_norm_g': 'new_v', 'new_v_k_norm_g': 'new_v', 'new_v_conv_w': 'new_v', 'new_v_w_branch_a': 'new_v', 'new_v_w_branch_b': 'new_v', 'new_v_w_out': 'new_v', 'new_v_ln2_g': 'new_v', 'new_v_w_ffn_gate': 'new_v', 'new_v_w_ffn_up': 'new_v', 'new_v_w_ffn_down': 'new_v'}


def _forward(args):
    return _fwd_reference(*[args[k] for k in FWD_PARAMS])


def _output_shape():
    def fwd():
        inp = _fwd_setup_inputs(0)
        return _fwd_reference(*[inp[k] for k in FWD_PARAMS])
    out = _jax.eval_shape(fwd)
    return out.shape, out.dtype

N_MICROBATCH = 1
ADAM_LR = 0.001
ADAM_B1 = 0.9
ADAM_B2 = 0.999
ADAM_EPS = 1e-08
ADAM_WD = 0.01
ADAM_STEP = 10
PER_EXAMPLE_BATCH_AXIS = {'x': 0, 'c': 0, 'loss_target': 0}
SHARED_INPUTS = []
_WEIGHT_DTYPES = {'ada_w': _jnp.float32, 'ada_b': _jnp.float32, 'ln1_g': _jnp.float32, 'w_in': _jnp.float32, 'q_norm_g': _jnp.float32, 'k_norm_g': _jnp.float32, 'conv_w': _jnp.float32, 'w_branch_a': _jnp.float32, 'w_branch_b': _jnp.float32, 'w_out': _jnp.float32, 'ln2_g': _jnp.float32, 'w_ffn_gate': _jnp.float32, 'w_ffn_up': _jnp.float32, 'w_ffn_down': _jnp.float32}
MOMENT_SCALE = {'ada_w': 8.199265e+00, 'ada_b': 2.122385e+01, 'ln1_g': 5.606662e+01, 'w_in': 1.710225e+00, 'q_norm_g': 4.215016e+00, 'k_norm_g': 4.223365e+00, 'conv_w': 1.159177e+01, 'w_branch_a': 1.984926e+00, 'w_branch_b': 1.247714e+00, 'w_out': 2.188212e+00, 'ln2_g': 2.719811e+01, 'w_ffn_gate': 1.333621e+00, 'w_ffn_up': 1.128941e+00, 'w_ffn_down': 1.608763e+00}


def _to_microbatches(a, axis):
    t = _jnp.moveaxis(a, axis, 0)
    t = t.reshape((N_MICROBATCH, t.shape[0] // N_MICROBATCH) + t.shape[1:])
    return _jnp.moveaxis(t, 1, axis + 1)


def setup_inputs(seed: int = 0) -> dict:
    inp = _fwd_setup_inputs(seed)
    key = _jax.random.fold_in(_jax.random.key(seed), 7919)
    shape, _ = _output_shape()
    out = dict(inp)
    out["loss_target"] = _jax.random.normal(_jax.random.fold_in(key, 0), shape, _jnp.float32)
    for i, name in enumerate(TWIN_WEIGHTS):
        w = inp[name].astype(_jnp.float32)
        if MOMENT_SCALE is None:
            s = _jnp.sqrt(_jnp.mean(_jnp.square(w)) + 1e-30)
        else:
            s = MOMENT_SCALE[name]
        km, kv = _jax.random.split(_jax.random.fold_in(key, i + 1))
        out[name] = w
        out["m_" + name] = s * _jax.random.normal(km, w.shape, _jnp.float32)
        out["v_" + name] = (s * s) * _jax.random.uniform(kv, w.shape, _jnp.float32, 0.5, 1.5)
    if N_MICROBATCH > 1:
        for name, axis in PER_EXAMPLE_BATCH_AXIS.items():
            out[name] = _to_microbatches(out[name], axis)
    return {'x': out['x'], 'c': out['c'], 'ada_w': out['ada_w'], 'ada_b': out['ada_b'], 'ln1_g': out['ln1_g'], 'w_in': out['w_in'], 'q_norm_g': out['q_norm_g'], 'k_norm_g': out['k_norm_g'], 'conv_w': out['conv_w'], 'w_branch_a': out['w_branch_a'], 'w_branch_b': out['w_branch_b'], 'w_out': out['w_out'], 'ln2_g': out['ln2_g'], 'w_ffn_gate': out['w_ffn_gate'], 'w_ffn_up': out['w_ffn_up'], 'w_ffn_down': out['w_ffn_down'], 'loss_target': out['loss_target'], 'm_ada_w': out['m_ada_w'], 'm_ada_b': out['m_ada_b'], 'm_ln1_g': out['m_ln1_g'], 'm_w_in': out['m_w_in'], 'm_q_norm_g': out['m_q_norm_g'], 'm_k_norm_g': out['m_k_norm_g'], 'm_conv_w': out['m_conv_w'], 'm_w_branch_a': out['m_w_branch_a'], 'm_w_branch_b': out['m_w_branch_b'], 'm_w_out': out['m_w_out'], 'm_ln2_g': out['m_ln2_g'], 'm_w_ffn_gate': out['m_w_ffn_gate'], 'm_w_ffn_up': out['m_w_ffn_up'], 'm_w_ffn_down': out['m_w_ffn_down'], 'v_ada_w': out['v_ada_w'], 'v_ada_b': out['v_ada_b'], 'v_ln1_g': out['v_ln1_g'], 'v_w_in': out['v_w_in'], 'v_q_norm_g': out['v_q_norm_g'], 'v_k_norm_g': out['v_k_norm_g'], 'v_conv_w': out['v_conv_w'], 'v_w_branch_a': out['v_w_branch_a'], 'v_w_branch_b': out['v_w_branch_b'], 'v_w_out': out['v_w_out'], 'v_ln2_g': out['v_ln2_g'], 'v_w_ffn_gate': out['v_w_ffn_gate'], 'v_w_ffn_up': out['v_w_ffn_up'], 'v_w_ffn_down': out['v_w_ffn_down']}


def _loss(weights, diff, rest, loss_target):
    with _jax.named_scope("forward"):
        args = {**rest, TWIN_DIFF_INPUT: diff, **{k: w.astype(_WEIGHT_DTYPES[k]) for k, w in weights.items()}}
        y = _forward(args)
    with _jax.named_scope("loss_head"):
        err = _jnp.square(y.astype(_jnp.float32) - loss_target)
        return 0.5 * _jnp.sum(_jnp.mean(err, axis=-1)) if err.ndim else 0.5 * err


def _adamw(w, g, m, v):
    m = ADAM_B1 * m + (1.0 - ADAM_B1) * g
    v = ADAM_B2 * v + (1.0 - ADAM_B2) * _jnp.square(g)
    m_hat = m / (1.0 - ADAM_B1 ** ADAM_STEP)
    v_hat = v / (1.0 - ADAM_B2 ** ADAM_STEP)
    delta = -ADAM_LR * (m_hat / (_jnp.sqrt(v_hat) + ADAM_EPS) + ADAM_WD * w)
    return delta, m, v


def reference(x, c, ada_w, ada_b, ln1_g, w_in, q_norm_g, k_norm_g, conv_w, w_branch_a, w_branch_b, w_out, ln2_g, w_ffn_gate, w_ffn_up, w_ffn_down, loss_target, m_ada_w, m_ada_b, m_ln1_g, m_w_in, m_q_norm_g, m_k_norm_g, m_conv_w, m_w_branch_a, m_w_branch_b, m_w_out, m_ln2_g, m_w_ffn_gate, m_w_ffn_up, m_w_ffn_down, v_ada_w, v_ada_b, v_ln1_g, v_w_in, v_q_norm_g, v_k_norm_g, v_conv_w, v_w_branch_a, v_w_branch_b, v_w_out, v_ln2_g, v_w_ffn_gate, v_w_ffn_up, v_w_ffn_down):
    given = dict(x=x, c=c, ada_w=ada_w, ada_b=ada_b, ln1_g=ln1_g, w_in=w_in, q_norm_g=q_norm_g, k_norm_g=k_norm_g, conv_w=conv_w, w_branch_a=w_branch_a, w_branch_b=w_branch_b, w_out=w_out, ln2_g=ln2_g, w_ffn_gate=w_ffn_gate, w_ffn_up=w_ffn_up, w_ffn_down=w_ffn_down, loss_target=loss_target, m_ada_w=m_ada_w, m_ada_b=m_ada_b, m_ln1_g=m_ln1_g, m_w_in=m_w_in, m_q_norm_g=m_q_norm_g, m_k_norm_g=m_k_norm_g, m_conv_w=m_conv_w, m_w_branch_a=m_w_branch_a, m_w_branch_b=m_w_branch_b, m_w_out=m_w_out, m_ln2_g=m_ln2_g, m_w_ffn_gate=m_w_ffn_gate, m_w_ffn_up=m_w_ffn_up, m_w_ffn_down=m_w_ffn_down, v_ada_w=v_ada_w, v_ada_b=v_ada_b, v_ln1_g=v_ln1_g, v_w_in=v_w_in, v_q_norm_g=v_q_norm_g, v_k_norm_g=v_k_norm_g, v_conv_w=v_conv_w, v_w_branch_a=v_w_branch_a, v_w_branch_b=v_w_branch_b, v_w_out=v_w_out, v_ln2_g=v_ln2_g, v_w_ffn_gate=v_w_ffn_gate, v_w_ffn_up=v_w_ffn_up, v_w_ffn_down=v_w_ffn_down)
    weights = {n: given[n] for n in TWIN_WEIGHTS}
    shared = {n: given[n] for n in SHARED_INPUTS}
    per_example = {n: given[n] for n in ['x', 'c']}
    grad_fn = _jax.value_and_grad(_loss, argnums=(0, 1))

    def one_microbatch(ex, loss_target):
        ex = dict(ex)
        diff = ex.pop(TWIN_DIFF_INPUT)
        return grad_fn(weights, diff, {**shared, **ex}, loss_target)

    if N_MICROBATCH == 1:
        loss, (grad_w, grad_x) = one_microbatch(per_example, given["loss_target"])
    else:
        def body(carry, xs):
            loss_sum, grad_sum = carry
            l_k, (gw_k, gx_k) = one_microbatch(xs[0], xs[1])
            with _jax.named_scope("update"):
                return (loss_sum + l_k, _jax.tree.map(_jnp.add, grad_sum, gw_k)), gx_k

        init = (_jnp.zeros((), _jnp.float32), _jax.tree.map(_jnp.zeros_like, weights))
        (loss, grad_w), grad_x = _jax.lax.scan(body, init, (per_example, given["loss_target"]))
    with _jax.named_scope("update"):
        delta_w, new_m, new_v = {}, {}, {}
        for n in TWIN_WEIGHTS:
            delta_w[n], new_m[n], new_v[n] = _adamw(weights[n], grad_w[n], given["m_" + n], given["v_" + n])
    return (loss, grad_x, *[grad_w[n] for n in TWIN_WEIGHTS], *[delta_w[n] for n in TWIN_WEIGHTS],
            *[new_m[n] for n in TWIN_WEIGHTS], *[new_v[n] for n in TWIN_WEIGHTS])
```

```python
import functools

import numpy as np
import jax
import jax.numpy as jnp
from jax import lax
from jax.experimental import pallas as pl
from jax.experimental.pallas import tpu as pltpu

F32, BF16 = jnp.float32, jnp.bfloat16
MESH = pl.DeviceIdType.MESH
N_DEV = 8
LANES = 128
HEAD_DIM = 64
HEAD_PAIR = 2 * HEAD_DIM
Q_BLOCK = 128
EPS = 1e-6
VMEM_LIMIT = 56 * 1024 * 1024

ADAM_LR, ADAM_B1, ADAM_B2, ADAM_EPS, ADAM_WD, ADAM_STEP = 0.001, 0.9, 0.999, 1e-08, 0.01, 10

_NT = (((1,), (1,)), ((), ()))
_TN = (((0,), (0,)), ((), ()))
_NN = (((1,), (0,)), ((), ()))


def _params(n_grid):
    return pltpu.CompilerParams(dimension_semantics=("arbitrary",) * n_grid, vmem_limit_bytes=VMEM_LIMIT)


def _sigmoid(x):
    return 1.0 / (1.0 + jnp.exp(-x))


def _rowwise(fn, name, *, rows, tile, tiled=(), halos=(), bcast=(), outs=(), accs=()):
    n = rows // tile
    assert n * tile == rows
    in_specs, args = [], []
    for t in tiled:
        arr, cb, w = t[:3]
        rowmap = t[3] if len(t) > 3 else (lambda i: i)
        in_specs.append(pl.BlockSpec((tile, w), functools.partial(lambda i, cb, rowmap: (rowmap(i), cb), cb=cb, rowmap=rowmap)))
        args.append(arr)
    per_tile8 = tile // 8
    for arr, cb, w, side in halos:
        last8 = arr.shape[0] // 8 - 1
        if side == "prev":
            imap = functools.partial(lambda i, cb: (jnp.maximum(i * per_tile8 - 1, 0), cb), cb=cb)
        else:
            imap = functools.partial(lambda i, cb, last8: (jnp.minimum((i + 1) * per_tile8, last8), cb), cb=cb, last8=last8)
        in_specs.append(pl.BlockSpec((8, w), imap))
        args.append(arr)
    for arr in bcast:
        in_specs.append(pl.BlockSpec(arr.shape, functools.partial(lambda i, nd: (0,) * nd, nd=arr.ndim)))
        args.append(arr)
    out_shape = [jax.ShapeDtypeStruct((rows, w), dt) for w, dt in outs] + [jax.ShapeDtypeStruct(s, F32) for s in accs]
    out_specs = [pl.BlockSpec((tile, w), lambda i: (i, 0)) for w, _ in outs] + [pl.BlockSpec(s, lambda i: (0, 0)) for s in accs]
    nt, nh, nb, no, na = len(tiled), len(halos), len(bcast), len(outs), len(accs)

    def body(*refs):
        pid = pl.program_id(0)
        tv = [r[...] for r in refs[:nt]]
        hv = [r[...] for r in refs[nt:nt + nh]]
        bv = [r[...] for r in refs[nt + nh:nt + nh + nb]]
        out_refs = refs[nt + nh + nb:nt + nh + nb + no]
        acc_refs = refs[nt + nh + nb + no:]
        ov, av = fn(pid, tv, hv, bv)
        for r, v in zip(out_refs, ov):
            r[...] = v.astype(r.dtype)
        if na:
            @pl.when(pid == 0)
            def _():
                for r in acc_refs:
                    r[...] = jnp.zeros(r.shape, F32)
            for r, v in zip(acc_refs, av):
                r[...] += v

    res = pl.pallas_call(body, name=name, grid=(n,), in_specs=in_specs, out_specs=out_specs, out_shape=out_shape,
                         compiler_params=_params(1))(*args)
    return res


def _colsum(v):
    return jnp.sum(v, axis=0, keepdims=True)


def _matmul(pairs, mode, *, m, n, k, tm, tn, out_dtype, name):
    tm, tn = min(tm, m), min(tn, n)
    assert m % tm == 0 and n % tn == 0
    in_specs, args = [], []
    for a, acb, b, bcb in pairs:
        if mode == "tn":
            in_specs.append(pl.BlockSpec((k, tm), functools.partial(lambda i, j, o: (0, o + i), o=acb)))
            in_specs.append(pl.BlockSpec((k, tn), functools.partial(lambda i, j, o: (0, o + j), o=bcb)))
        elif mode == "nn":
            in_specs.append(pl.BlockSpec((tm, k), functools.partial(lambda i, j, o: (i, o), o=acb)))
            in_specs.append(pl.BlockSpec((k, tn), functools.partial(lambda i, j, o: (0, o + j), o=bcb)))
        else:
            in_specs.append(pl.BlockSpec((tm, k), functools.partial(lambda i, j, o: (i, o), o=acb)))
            in_specs.append(pl.BlockSpec((tn, k), functools.partial(lambda i, j, o: (j, o), o=bcb)))
        args += [a, b]
    dims = {"nn": _NN, "nt": _NT, "tn": _TN}[mode]
    npairs = len(pairs)

    def body(*refs):
        o_ref = refs[2 * npairs]
        acc = None
        for p in range(npairs):
            d = lax.dot_general(refs[2 * p][...].astype(BF16), refs[2 * p + 1][...].astype(BF16), dims,
                                preferred_element_type=F32)
            acc = d if acc is None else acc + d
        o_ref[...] = acc.astype(o_ref.dtype)

    return pl.pallas_call(body, name=name, grid=(m // tm, n // tn), in_specs=in_specs,
                          out_specs=pl.BlockSpec((tm, tn), lambda i, j: (i, j)),
                          out_shape=jax.ShapeDtypeStruct((m, n), out_dtype), compiler_params=_params(2))(*args)


def _scan_matrix(kind):
    r = np.arange(Q_BLOCK)
    tri = (r[:, None] > r[None, :]) if kind == "suffix" else (r[:, None] < r[None, :])
    half = np.concatenate([tri.astype(np.float32), np.ones((Q_BLOCK, Q_BLOCK), np.float32)], axis=1)
    return jnp.asarray(np.concatenate([half, half], axis=0), BF16)


def _head_sum_matrix():
    r = np.arange(LANES)
    bd = (r[:, None] // HEAD_DIM == r[None, :] // HEAD_DIM).astype(np.float32)
    return jnp.asarray(np.concatenate([bd, bd], axis=0), BF16)


def _split_cat(v):
    hi = v.astype(BF16)
    lo = (v - hi.astype(F32)).astype(BF16)
    return jnp.concatenate([hi, lo], axis=1)


def _head_sums(v, bd2):
    hi = v.astype(BF16)
    lo = (v - hi.astype(F32)).astype(BF16)
    parts = []
    for g in range(v.shape[1] // LANES):
        sl = slice(g * LANES, (g + 1) * LANES)
        parts.append(jnp.dot(jnp.concatenate([hi[:, sl], lo[:, sl]], axis=1), bd2, preferred_element_type=F32))
    return jnp.concatenate(parts, axis=1)


def _sb_block(qm, kblk, scan_suffix, carry_n, diag_mask):
    z = lax.dot_general(qm, kblk, _NT, preferred_element_type=F32)
    sp = jnp.log(1.0 + jnp.exp(-jnp.abs(z)))
    log_not = -(jnp.maximum(z, 0.0) + sp)
    log_beta = z + log_not
    if diag_mask is not None:
        log_not = jnp.where(diag_mask, log_not, 0.0)
    cs = jnp.dot(_split_cat(log_not), scan_suffix, preferred_element_type=F32)
    log_a = log_beta + carry_n + cs[:, :Q_BLOCK]
    a = jnp.exp(log_a)
    if diag_mask is not None:
        a = jnp.where(diag_mask, a, 0.0)
    return a, log_beta, carry_n + cs[:, Q_BLOCK:]


def _attention_fwd(qn, kn, vb, name):
    t, d = qn.shape
    nq, hp = t // Q_BLOCK, d // HEAD_PAIR
    scan_suffix = _scan_matrix("suffix")

    def body(q_ref, k_ref, v_ref, sc_ref, o_ref):
        qi = pl.program_id(1)
        q2 = q_ref[...]
        scan = sc_ref[...]
        lane = lax.broadcasted_iota(jnp.int32, (Q_BLOCK, HEAD_PAIR), 1)
        row = lax.broadcasted_iota(jnp.int32, (Q_BLOCK, Q_BLOCK), 0)
        col = lax.broadcasted_iota(jnp.int32, (Q_BLOCK, Q_BLOCK), 1)
        diag = col < row
        out = jnp.zeros((Q_BLOCK, HEAD_PAIR), F32)
        for hh in range(2):
            head = (lane // HEAD_DIM) == hh
            qm = jnp.where(head, q2.astype(F32), 0.0).astype(BF16)

            def rows_of(j):
                return pl.ds(pl.multiple_of(j * Q_BLOCK, Q_BLOCK), Q_BLOCK)

            a, _, carry = _sb_block(qm, k_ref[rows_of(qi), :], scan, jnp.zeros((Q_BLOCK, Q_BLOCK), F32), diag)
            acc = jnp.dot(a.astype(BF16), v_ref[rows_of(qi), :], preferred_element_type=F32)

            def step(jj, c):
                carry, acc = c
                j = qi - 1 - jj
                a, _, carry = _sb_block(qm, k_ref[rows_of(j), :], scan, carry, None)
                return carry, acc + jnp.dot(a.astype(BF16), v_ref[rows_of(j), :], preferred_element_type=F32)

            _, acc = lax.fori_loop(0, qi, step, (carry, acc))
            out = jnp.where(head, acc, out)
        o_ref[...] = out.astype(o_ref.dtype)

    return pl.pallas_call(
        body, name=name, grid=(hp, nq),
        in_specs=[pl.BlockSpec((Q_BLOCK, HEAD_PAIR), lambda h, i: (i, h)),
                  pl.BlockSpec((t, HEAD_PAIR), lambda h, i: (0, h)),
                  pl.BlockSpec((t, HEAD_PAIR), lambda h, i: (0, h)),
                  pl.BlockSpec((2 * Q_BLOCK, 2 * Q_BLOCK), lambda h, i: (0, 0))],
        out_specs=pl.BlockSpec((Q_BLOCK, HEAD_PAIR), lambda h, i: (i, h)),
        out_shape=jax.ShapeDtypeStruct((t, d), BF16), compiler_params=_params(2))(qn, kn, vb, scan_suffix)


def _attention_bwd(qn, kn, vb, dob, name):
    t, d = qn.shape
    nq, hp = t // Q_BLOCK, d // HEAD_PAIR
    scan_suffix, scan_prefix = _scan_matrix("suffix"), _scan_matrix("prefix")

    def body(q_ref, k_ref, v_ref, do_ref, ss_ref, sp_ref, dq_ref, dk_ref, dv_ref, g_s, b_s):
        qi = pl.program_id(1)

        @pl.when(qi == 0)
        def _():
            dk_ref[...] = jnp.zeros(dk_ref.shape, F32)
            dv_ref[...] = jnp.zeros(dv_ref.shape, F32)

        q2, do2 = q_ref[...], do_ref[...]
        ssuf, spre = ss_ref[...], sp_ref[...]
        lane = lax.broadcasted_iota(jnp.int32, (Q_BLOCK, HEAD_PAIR), 1)
        row = lax.broadcasted_iota(jnp.int32, (Q_BLOCK, Q_BLOCK), 0)
        col = lax.broadcasted_iota(jnp.int32, (Q_BLOCK, Q_BLOCK), 1)
        diag = col < row
        dq_out = jnp.zeros((Q_BLOCK, HEAD_PAIR), F32)

        def rows_of(j):
            return pl.ds(pl.multiple_of(j * Q_BLOCK, Q_BLOCK), Q_BLOCK)

        for hh in range(2):
            head = (lane // HEAD_DIM) == hh
            qm = jnp.where(head, q2.astype(F32), 0.0).astype(BF16)
            dom = jnp.where(head, do2.astype(F32), 0.0).astype(BF16)

            def pass_one(j, carry, mask):
                a, log_beta, carry = _sb_block(qm, k_ref[rows_of(j), :], ssuf, carry, mask)
                beta = jnp.exp(log_beta)
                if mask is not None:
                    beta = jnp.where(mask, beta, 0.0)
                d_a = lax.dot_general(dom, v_ref[rows_of(j), :], _NT, preferred_element_type=F32)
                g_s[hh, j] = a * d_a
                b_s[hh, j] = beta
                dv_ref[rows_of(j), :] += lax.dot_general(a.astype(BF16), dom, _TN, preferred_element_type=F32)
                return carry

            carry = pass_one(qi, jnp.zeros((Q_BLOCK, Q_BLOCK), F32), diag)
            lax.fori_loop(0, qi, lambda jj, c: pass_one(qi - 1 - jj, c, None), carry)

            def pass_two(j, c):
                prefix, dq = c
                g, beta = g_s[hh, j], b_s[hh, j]
                cs = jnp.dot(_split_cat(g), spre, preferred_element_type=F32)
                dz = (g - beta * (g + prefix + cs[:, :Q_BLOCK])).astype(BF16)
                dq = dq + jnp.dot(dz, k_ref[rows_of(j), :], preferred_element_type=F32)
                dk_ref[rows_of(j), :] += lax.dot_general(dz, qm, _TN, preferred_element_type=F32)
                return prefix + cs[:, Q_BLOCK:], dq

            _, dq = lax.fori_loop(0, qi + 1, pass_two,
                                  (jnp.zeros((Q_BLOCK, Q_BLOCK), F32), jnp.zeros((Q_BLOCK, HEAD_PAIR), F32)))
            dq_out = jnp.where(head, dq, dq_out)
        dq_ref[...] = dq_out

    blk = pl.BlockSpec((Q_BLOCK, HEAD_PAIR), lambda h, i: (i, h))
    col_spec = pl.BlockSpec((t, HEAD_PAIR), lambda h, i: (0, h))
    const = pl.BlockSpec((2 * Q_BLOCK, 2 * Q_BLOCK), lambda h, i: (0, 0))
    full = jax.ShapeDtypeStruct((t, d), F32)
    return pl.pallas_call(
        body, name=name, grid=(hp, nq), in_specs=[blk, col_spec, col_spec, blk, const, const],
        out_specs=[blk, col_spec, col_spec], out_shape=[full, full, full],
        scratch_shapes=[pltpu.VMEM((2, nq, Q_BLOCK, Q_BLOCK), F32), pltpu.VMEM((2, nq, Q_BLOCK, Q_BLOCK), F32)],
        compiler_params=_params(2))(qn, kn, vb, dob, scan_suffix, scan_prefix)


def _all_gather(x2d, name):
    r, w = x2d.shape

    def body(x_ref, out_ref, send_sems, recv_sems, local_sem):
        x, y, c = lax.axis_index("x"), lax.axis_index("y"), lax.axis_index("c")
        me, sibling = (x, y, c), (x, y, 1 - c)
        chips = [(1 - x, y), (x, 1 - y), (1 - x, 1 - y)]

        def rows(px, py, pc):
            return out_ref.at[4 * px + 2 * py + pc]

        def copy(k, block, to, src=None):
            return pltpu.make_async_remote_copy(src_ref=rows(*block) if src is None else src, dst_ref=rows(*block),
                                                send_sem=send_sems.at[k], recv_sem=recv_sems.at[k],
                                                device_id=to, device_id_type=MESH)

        mine = pltpu.make_async_copy(x_ref, rows(*me), local_sem)
        mine.start()
        first = [copy(0, me, sibling, src=x_ref)]
        first += [copy(1 + j, me, (*chip, c), src=x_ref) for j, chip in enumerate(chips)]
        for cp in first:
            cp.start()
        passed = [copy(4 + j, (*chip, c), sibling) for j, chip in enumerate(chips)]
        for j, chip in enumerate(chips):
            copy(1 + j, (*chip, c), me).wait_recv()
            passed[j].start()
        copy(0, sibling, me).wait_recv()
        for j, chip in enumerate(chips):
            copy(4 + j, (*chip, 1 - c), me).wait_recv()
        for cp in first + passed:
            cp.wait_send()
        mine.wait()

    return pl.pallas_call(
        body, name=name, out_shape=jax.ShapeDtypeStruct((N_DEV, r, w), x2d.dtype),
        in_specs=[pl.BlockSpec(memory_space=pl.ANY)], out_specs=pl.BlockSpec(memory_space=pl.ANY),
        scratch_shapes=[pltpu.SemaphoreType.DMA((7,)), pltpu.SemaphoreType.DMA((7,)), pltpu.SemaphoreType.DMA])(x2d)


def _exchange_sibling(g_rel, name):
    _, _, r, w = g_rel.shape

    def body(g_ref, recv_ref, send_sems, recv_sems):
        x, y, c = lax.axis_index("x"), lax.axis_index("y"), lax.axis_index("c")
        copies = [pltpu.make_async_remote_copy(src_ref=g_ref.at[j, 1], dst_ref=recv_ref.at[j], send_sem=send_sems.at[j],
                                               recv_sem=recv_sems.at[j], device_id=(x, y, 1 - c), device_id_type=MESH)
                  for j in range(4)]
        for cp in copies:
            cp.start()
        for cp in copies:
            cp.wait_recv()
        for cp in copies:
            cp.wait_send()

    return pl.pallas_call(
        body, name=name, out_shape=jax.ShapeDtypeStruct((4, r, w), g_rel.dtype),
        in_specs=[pl.BlockSpec(memory_space=pl.ANY)], out_specs=pl.BlockSpec(memory_space=pl.ANY),
        scratch_shapes=[pltpu.SemaphoreType.DMA((4,)), pltpu.SemaphoreType.DMA((4,))])(g_rel)


def _exchange_chips(part, name):
    _, r, w = part.shape

    def body(p_ref, recv_ref, send_sems, recv_sems):
        x, y, c = lax.axis_index("x"), lax.axis_index("y"), lax.axis_index("c")
        targets = {1: (x, 1 - y, c), 2: (1 - x, y, c), 3: (1 - x, 1 - y, c)}
        copies = [pltpu.make_async_remote_copy(src_ref=p_ref.at[j], dst_ref=recv_ref.at[j - 1], send_sem=send_sems.at[j - 1],
                                               recv_sem=recv_sems.at[j - 1], device_id=targets[j], device_id_type=MESH)
                  for j in (1, 2, 3)]
        for cp in copies:
            cp.start()
        for cp in copies:
            cp.wait_recv()
        for cp in copies:
            cp.wait_send()

    return pl.pallas_call(
        body, name=name, out_shape=jax.ShapeDtypeStruct((3, r, w), part.dtype),
        in_specs=[pl.BlockSpec(memory_space=pl.ANY)], out_specs=pl.BlockSpec(memory_space=pl.ANY),
        scratch_shapes=[pltpu.SemaphoreType.DMA((3,)), pltpu.SemaphoreType.DMA((3,))])(part)


def _to_relative(a, x, y, c):
    a = a.reshape((2, 2, 2) + a.shape[1:])
    a = jnp.where(x == 1, jnp.flip(a, 0), a)
    a = jnp.where(y == 1, jnp.flip(a, 1), a)
    a = jnp.where(c == 1, jnp.flip(a, 2), a)
    return a.reshape((4, 2) + a.shape[3:])


def _reduce_scatter(g_abs, x, y, c, tile):
    _, r, w = g_abs.shape
    n = r // tile
    g_rel = _to_relative(g_abs, x, y, c)
    recv1 = _exchange_sibling(g_rel, "rs_exchange_sibling")
    add = lambda pid, tv, hv, bv: ([sum(tv[1:], tv[0])], [])
    part, = _rowwise(add, "rs_add_sibling", rows=4 * r, tile=tile,
                     tiled=[(g_rel.reshape(8 * r, w), 0, w, lambda s: 2 * (s // n) * n + s % n), (recv1.reshape(4 * r, w), 0, w)],
                     outs=[(w, F32)])
    recv2 = _exchange_chips(part.reshape(4, r, w), "rs_exchange_chips").reshape(3 * r, w)
    total, = _rowwise(add, "rs_add_chips", rows=r, tile=tile,
                      tiled=[(part, 0, w), (recv2, 0, w), (recv2, 0, w, lambda i: i + n), (recv2, 0, w, lambda i: i + 2 * n)],
                      outs=[(w, F32)])
    return total


def _ada_forward(c_all, ada_w, ada_b_cols):
    depth, d, cols = ada_w.shape

    def body(c_ref, w_ref, b_ref, o_ref):
        cv = c_ref[...]
        act = cv * _sigmoid(cv)
        o_ref[...] = jnp.dot(act, w_ref[...], preferred_element_type=F32, precision=lax.Precision.HIGHEST) + b_ref[...]

    return pl.pallas_call(
        body, name="ada_forward", grid=(depth,),
        in_specs=[pl.BlockSpec((N_DEV, d), lambda l: (0, 0)), pl.BlockSpec((None, d, cols), lambda l: (l, 0, 0)),
                  pl.BlockSpec((None, 1, cols), lambda l: (l, 0, 0))],
        out_specs=pl.BlockSpec((None, N_DEV, cols), lambda l: (l, 0, 0)),
        out_shape=jax.ShapeDtypeStruct((depth, N_DEV, cols), F32), compiler_params=_params(1))(
            c_all, ada_w, ada_b_cols.reshape(depth, 1, cols))


def _ada_backward(c_all, dmod_cols):
    depth, _, cols = dmod_cols.shape
    d = c_all.shape[1]

    def body(c_ref, g_ref, o_ref):
        cv = c_ref[...]
        act = cv * _sigmoid(cv)
        o_ref[...] = lax.dot_general(act, g_ref[...], _TN, preferred_element_type=F32, precision=lax.Precision.HIGHEST)

    return pl.pallas_call(
        body, name="ada_backward", grid=(depth,),
        in_specs=[pl.BlockSpec((N_DEV, d), lambda l: (0, 0)), pl.BlockSpec((None, N_DEV, cols), lambda l: (l, 0, 0))],
        out_specs=pl.BlockSpec((None, d, cols), lambda l: (l, 0, 0)),
        out_shape=jax.ShapeDtypeStruct((depth, d, cols), F32), compiler_params=_params(1))(c_all, dmod_cols)


def _device_sum(a):
    _, r, w = a.shape

    def body(a_ref, o_ref):
        acc = a_ref[0]
        for dev in range(1, N_DEV):
            acc = acc + a_ref[dev]
        o_ref[...] = acc

    return pl.pallas_call(body, name="device_sum", out_shape=jax.ShapeDtypeStruct((r, w), F32),
                          in_specs=[pl.BlockSpec(memory_space=pltpu.VMEM)], out_specs=pl.BlockSpec(memory_space=pltpu.VMEM))(a)


def _adamw(w, g, m, v, name):
    shape = w.shape
    cols = shape[-1]
    rows = int(np.prod(shape[:-1]))
    tile = rows
    for cand in (512, 352, 256):
        if rows > cand and rows % cand == 0:
            tile = cand
            break

    def fn(pid, tv, hv, bv):
        wv, gv, mv, vv = tv
        mn = ADAM_B1 * mv + (1.0 - ADAM_B1) * gv
        vn = ADAM_B2 * vv + (1.0 - ADAM_B2) * (gv * gv)
        m_hat = mn / (1.0 - ADAM_B1 ** ADAM_STEP)
        v_hat = vn / (1.0 - ADAM_B2 ** ADAM_STEP)
        delta = -ADAM_LR * (m_hat / (jnp.sqrt(v_hat) + ADAM_EPS) + ADAM_WD * wv)
        return [delta, mn, vn], []

    res = _rowwise(fn, name, rows=rows, tile=tile, tiled=[(a.reshape(rows, cols), 0, cols) for a in (w, g, m, v)],
                   outs=[(cols, F32)] * 3)
    return [r.reshape(shape) for r in res]


def _norm_modulate(x, y, gate, ln_g, scale, shift, name, tile):
    t, d = x.shape

    def fn(pid, tv, hv, bv):
        if y is None:
            xn = tv[0]
            g_ln, sc, sh = bv
        else:
            g_gate, g_ln, sc, sh = bv
            xn = tv[0] + g_gate * tv[1]
        r = lax.rsqrt(jnp.mean(xn * xn, axis=-1, keepdims=True) + EPS)
        h = (xn * r * g_ln) * (1.0 + sc) + sh
        return ([h] if y is None else [xn, h]), []

    if y is None:
        h, = _rowwise(fn, name, rows=t, tile=tile, tiled=[(x, 0, d)], bcast=[ln_g, scale, shift], outs=[(d, BF16)])
        return x, h
    xn, h = _rowwise(fn, name, rows=t, tile=tile, tiled=[(x, 0, d), (y, 0, d)], bcast=[gate, ln_g, scale, shift],
                     outs=[(d, F32), (d, BF16)])
    return xn, h


def _norm_backward(dh, x, dres, ln_g, scale, name, tile):
    t, d = x.shape

    def fn(pid, tv, hv, bv):
        dhv, xv, dr = tv
        g_ln, sc = bv
        r = lax.rsqrt(jnp.mean(xv * xv, axis=-1, keepdims=True) + EPS)
        xn = xv * r
        dxn = dhv * (1.0 + sc) * g_ln
        dx = dr + r * (dxn - xn * jnp.mean(dxn * xn, axis=-1, keepdims=True))
        return [dx], [_colsum(dhv), _colsum(dhv * (xn * g_ln)), _colsum(dhv * (1.0 + sc) * xn)]

    return _rowwise(fn, name, rows=t, tile=tile, tiled=[(dh, 0, d), (x, 0, d), (dres, 0, d)], bcast=[ln_g, scale],
                    outs=[(d, F32)], accs=[(1, d)] * 3)


def _shift_rows(u, halo, k, pid, first_tile_zero):
    rows = lax.broadcasted_iota(jnp.int32, u.shape, 0)
    halo = halo * jnp.where(pid == 0, 0.0, 1.0) if first_tile_zero else halo
    out = pltpu.roll(u, k, axis=0)
    for j in range(k):
        out = jnp.where(rows == j, halo[8 - k + j:8 - k + j + 1, :], out)
    return out


def _shift_rows_up(u, halo, k, pid, n_tiles):
    tile = u.shape[0]
    rows = lax.broadcasted_iota(jnp.int32, u.shape, 0)
    halo = halo * jnp.where(pid == n_tiles - 1, 0.0, 1.0)
    out = pltpu.roll(u, tile - k, axis=0)
    for j in range(k):
        out = jnp.where(rows == tile - k + j, halo[j:j + 1, :], out)
    return out


def _layer_forward(x_in, y_prev, gate_prev, mod, w, l, tile):
    sh1, sc1, g1, sh2, sc2, g2 = mod
    t, d = x_in.shape
    f = w["w_g"].shape[1]
    bd2 = _head_sum_matrix()
    s = {}
    s["x"], s["h"] = _norm_modulate(x_in, y_prev, gate_prev, w["ln1"], sc1, sh1, f"l{l}_norm1", tile)
    p = _matmul([(s["h"], 0, w["w_in"], 0)], "nn", m=t, n=8 * d, k=d, tm=1024, tn=512, out_dtype=F32, name=f"l{l}_in_proj")
    s["p"] = p

    def qk_norm(pid, tv, hv, bv):
        qr, kr, vr = tv
        qg, kg, bd = bv

        def nrm(xv, g):
            r = lax.rsqrt(_head_sums(xv * xv, bd) * (1.0 / HEAD_DIM) + EPS)
            return xv * r * g
        return [nrm(qr, qg) * 0.125, nrm(kr, kg), vr], []

    s["qn"], s["kn"], s["vb"] = _rowwise(qk_norm, f"l{l}_qk_norm", rows=t, tile=tile, tiled=[(p, 0, d), (p, 1, d), (p, 2, d)],
                                         bcast=[w["qg"], w["kg"], bd2], outs=[(d, BF16)] * 3)
    s["ya"] = _attention_fwd(s["qn"], s["kn"], s["vb"], f"l{l}_attention")

    def conv_fwd(pid, tv, hv, bv):
        cb, cc, cx = tv
        u = cc * cx
        hu = hv[0] * hv[1]
        cw = bv[0]
        conv = cw[0:1, :] * _shift_rows(u, hu, 2, pid, True) + cw[1:2, :] * _shift_rows(u, hu, 1, pid, True) + cw[2:3, :] * u
        return [cb * conv], []

    s["yb"], = _rowwise(conv_fwd, f"l{l}_conv", rows=t, tile=tile, tiled=[(p, 3, d), (p, 4, d), (p, 5, d)],
                        halos=[(p, 4, d, "prev"), (p, 5, d, "prev")], bcast=[w["conv"]], outs=[(d, BF16)])
    s["a"] = _matmul([(s["ya"], 0, w["w_a"], 0)], "nn", m=t, n=d, k=d, tm=1024, tn=512, out_dtype=F32, name=f"l{l}_branch_a")
    s["b"] = _matmul([(s["yb"], 0, w["w_b"], 0)], "nn", m=t, n=d, k=d, tm=1024, tn=512, out_dtype=F32, name=f"l{l}_branch_b")

    def merge(pid, tv, hv, bv):
        av, bvv, ga, gb = tv
        return [_sigmoid(ga) * av + _sigmoid(gb) * bvv], []

    s["merged"], = _rowwise(merge, f"l{l}_merge", rows=t, tile=tile, tiled=[(s["a"], 0, d), (s["b"], 0, d), (p, 6, d), (p, 7, d)],
                            outs=[(d, BF16)])
    s["mo"] = _matmul([(s["merged"], 0, w["w_o"], 0)], "nn", m=t, n=d, k=d, tm=1024, tn=512, out_dtype=F32, name=f"l{l}_out_proj")
    s["x2"], s["h2"] = _norm_modulate(s["x"], s["mo"], g1, w["ln2"], sc2, sh2, f"l{l}_norm2", tile)
    fn_tile = f // 2
    s["g"] = _matmul([(s["h2"], 0, w["w_g"], 0)], "nn", m=t, n=f, k=d, tm=512, tn=fn_tile, out_dtype=F32, name=f"l{l}_ffn_gate")
    s["u"] = _matmul([(s["h2"], 0, w["w_u"], 0)], "nn", m=t, n=f, k=d, tm=512, tn=fn_tile, out_dtype=F32, name=f"l{l}_ffn_up")

    def swiglu(pid, tv, hv, bv):
        gv, uv = tv
        return [gv * _sigmoid(gv) * uv], []

    s["s"], = _rowwise(swiglu, f"l{l}_swiglu", rows=t, tile=tile // 2, tiled=[(s["g"], 0, f), (s["u"], 0, f)], outs=[(f, BF16)])
    s["f"] = _matmul([(s["s"], 0, w["w_d"], 0)], "nn", m=t, n=d, k=f, tm=512, tn=512, out_dtype=F32, name=f"l{l}_ffn_down")
    return s


def _layer_backward(dx3, s, mod, w, l, tile):
    sh1, sc1, g1, sh2, sc2, g2 = mod
    t, d = dx3.shape
    f = w["w_g"].shape[1]
    p = s["p"]
    bd2 = _head_sum_matrix()
    n_tiles = t // tile
    grads = {}

    def gate_bwd(pid, tv, hv, bv):
        return [tv[0] * bv[0]], [_colsum(tv[0] * tv[1])]

    df, dg2 = _rowwise(gate_bwd, f"l{l}_bwd_gate2", rows=t, tile=tile, tiled=[(dx3, 0, d), (s["f"], 0, d)], bcast=[g2],
                       outs=[(d, BF16)], accs=[(1, d)])
    grads["w_d"] = _matmul([(s["s"], 0, df, 0)], "tn", m=f, n=d, k=t, tm=256, tn=512, out_dtype=F32, name=f"l{l}_dw_down")
    ds = _matmul([(df, 0, w["w_d"], 0)], "nt", m=t, n=f, k=d, tm=512, tn=f // 2, out_dtype=F32, name=f"l{l}_d_swiglu")

    def swiglu_bwd(pid, tv, hv, bv):
        dsv, gv, uv = tv
        sig = _sigmoid(gv)
        return [dsv * uv * (sig * (1.0 + gv * (1.0 - sig))), dsv * (gv * sig)], []

    dgt, dup = _rowwise(swiglu_bwd, f"l{l}_bwd_swiglu", rows=t, tile=tile // 2, tiled=[(ds, 0, f), (s["g"], 0, f), (s["u"], 0, f)],
                        outs=[(f, BF16)] * 2)
    grads["w_g"] = _matmul([(s["h2"], 0, dgt, 0)], "tn", m=d, n=f, k=t, tm=256, tn=f // 2, out_dtype=F32, name=f"l{l}_dw_gate")
    grads["w_u"] = _matmul([(s["h2"], 0, dup, 0)], "tn", m=d, n=f, k=t, tm=256, tn=f // 2, out_dtype=F32, name=f"l{l}_dw_up")
    dh2 = _matmul([(dgt, 0, w["w_g"], 0), (dup, 0, w["w_u"], 0)], "nt", m=t, n=d, k=f, tm=512, tn=512, out_dtype=F32,
                  name=f"l{l}_dh2")
    dx2, dsh2, dsc2, grads["ln2"] = _norm_backward(dh2, s["x2"], dx3, w["ln2"], sc2, f"l{l}_bwd_norm2", tile)

    dmo, dg1 = _rowwise(gate_bwd, f"l{l}_bwd_gate1", rows=t, tile=tile, tiled=[(dx2, 0, d), (s["mo"], 0, d)], bcast=[g1],
                        outs=[(d, BF16)], accs=[(1, d)])
    grads["w_o"] = _matmul([(s["merged"], 0, dmo, 0)], "tn", m=d, n=d, k=t, tm=256, tn=512, out_dtype=F32, name=f"l{l}_dw_out")
    dmerged = _matmul([(dmo, 0, w["w_o"], 0)], "nt", m=t, n=d, k=d, tm=1024, tn=512, out_dtype=F32, name=f"l{l}_d_merged")

    def merge_bwd(pid, tv, hv, bv):
        dm, av, bvv, ga, gb = tv
        sa, sb = _sigmoid(ga), _sigmoid(gb)
        return [dm * sa, dm * sb, dm * av * (sa * (1.0 - sa)), dm * bvv * (sb * (1.0 - sb))], []

    d_a, d_b, dga, dgb = _rowwise(merge_bwd, f"l{l}_bwd_merge", rows=t, tile=tile,
                                  tiled=[(dmerged, 0, d), (s["a"], 0, d), (s["b"], 0, d), (p, 6, d), (p, 7, d)], outs=[(d, BF16)] * 4)
    grads["w_a"] = _matmul([(s["ya"], 0, d_a, 0)], "tn", m=d, n=d, k=t, tm=256, tn=512, out_dtype=F32, name=f"l{l}_dw_a")
    grads["w_b"] = _matmul([(s["yb"], 0, d_b, 0)], "tn", m=d, n=d, k=t, tm=256, tn=512, out_dtype=F32, name=f"l{l}_dw_b")
    dya = _matmul([(d_a, 0, w["w_a"], 0)], "nt", m=t, n=d, k=d, tm=1024, tn=512, out_dtype=BF16, name=f"l{l}_d_ya")
    dyb = _matmul([(d_b, 0, w["w_b"], 0)], "nt", m=t, n=d, k=d, tm=1024, tn=512, out_dtype=F32, name=f"l{l}_d_yb")

    def conv_bwd(pid, tv, hv, bv):
        dy, cb, cc, cx = tv
        cw = bv[0]
        u, hu = cc * cx, hv[0] * hv[1]
        u1, u2 = _shift_rows(u, hu, 1, pid, True), _shift_rows(u, hu, 2, pid, True)
        conv = cw[0:1, :] * u2 + cw[1:2, :] * u1 + cw[2:3, :] * u
        dconv, hd = dy * cb, hv[2] * hv[3]
        du = (cw[2:3, :] * dconv + cw[1:2, :] * _shift_rows_up(dconv, hd, 1, pid, n_tiles)
              + cw[0:1, :] * _shift_rows_up(dconv, hd, 2, pid, n_tiles))
        return [dy * conv, du * cx, du * cc], [_colsum(dconv * u2), _colsum(dconv * u1), _colsum(dconv * u)]

    dcb, dcc, dcx, dcw0, dcw1, dcw2 = _rowwise(
        conv_bwd, f"l{l}_bwd_conv", rows=t, tile=tile, tiled=[(dyb, 0, d), (p, 3, d), (p, 4, d), (p, 5, d)],
        halos=[(p, 4, d, "prev"), (p, 5, d, "prev"), (dyb, 0, d, "next"), (p, 3, d, "next")], bcast=[w["conv"]],
        outs=[(d, BF16)] * 3, accs=[(1, d)] * 3)
    grads["conv"] = jnp.concatenate([dcw0, dcw1, dcw2], axis=0)

    dqs, dkn, dv = _attention_bwd(s["qn"], s["kn"], s["vb"], dya, f"l{l}_bwd_attention")

    def qk_norm_bwd(pid, tv, hv, bv):
        dq, dk, qr, kr, dvv = tv
        qg, kg, bd = bv

        def bwd(dy, xv, g):
            r = lax.rsqrt(_head_sums(xv * xv, bd) * (1.0 / HEAD_DIM) + EPS)
            yv = xv * r
            dyn = dy * g
            dx = r * (dyn - yv * (_head_sums(dyn * yv, bd) * (1.0 / HEAD_DIM)))
            return dx, _colsum(dy * yv)

        dxq, dgq = bwd(dq * 0.125, qr, qg)
        dxk, dgk = bwd(dk, kr, kg)
        return [dxq, dxk, dvv], [dgq, dgk]

    dqr, dkr, dvb, grads["qg"], grads["kg"] = _rowwise(
        qk_norm_bwd, f"l{l}_bwd_qk_norm", rows=t, tile=tile, tiled=[(dqs, 0, d), (dkn, 0, d), (p, 0, d), (p, 1, d), (dv, 0, d)],
        bcast=[w["qg"], w["kg"], bd2], outs=[(d, BF16)] * 3, accs=[(1, d)] * 2)

    dp = [dqr, dkr, dvb, dcb, dcc, dcx, dga, dgb]
    grads["w_in"] = [_matmul([(s["h"], 0, dpk, 0)], "tn", m=d, n=d, k=t, tm=256, tn=512, out_dtype=F32, name=f"l{l}_dw_in{k}")
                     for k, dpk in enumerate(dp)]
    dh = _matmul([(dpk, 0, w["w_in"], k) for k, dpk in enumerate(dp)], "nt", m=t, n=d, k=d, tm=512, tn=512, out_dtype=F32,
                 name=f"l{l}_dh")
    dx, dsh1, dsc1, grads["ln1"] = _norm_backward(dh, s["x"], dx2, w["ln1"], sc1, f"l{l}_bwd_norm1", tile)
    return dx, grads, [dsh1, dsc1, dg1, dsh2, dsc2, dg2]


_BIG = ["w_in", "w_branch_a", "w_branch_b", "w_out", "w_ffn_gate", "w_ffn_up", "w_ffn_down"]
_COL_SHARDED = {"w_in", "w_ffn_gate", "w_ffn_up"}


def _unpack_weight(flat8, shard_shape, col_sharded):
    r, c = shard_shape
    a = flat8.reshape(N_DEV, r, c)
    if col_sharded:
        return jnp.transpose(a, (1, 0, 2)).reshape(r, N_DEV * c)
    return a.reshape(N_DEV * r, c)


def _pack_grad(full, col_sharded):
    r, c = full.shape
    if col_sharded:
        return jnp.transpose(full.reshape(r, N_DEV, c // N_DEV), (1, 0, 2)).reshape(N_DEV, -1)
    return full.reshape(N_DEV, -1)


def kernel(x, c, ada_w, ada_b, ln1_g, w_in, q_norm_g, k_norm_g, conv_w, w_branch_a, w_branch_b, w_out, ln2_g, w_ffn_gate, w_ffn_up, w_ffn_down, loss_target, m_ada_w, m_ada_b, m_ln1_g, m_w_in, m_q_norm_g, m_k_norm_g, m_conv_w, m_w_branch_a, m_w_branch_b, m_w_out, m_ln2_g, m_w_ffn_gate, m_w_ffn_up, m_w_ffn_down, v_ada_w, v_ada_b, v_ln1_g, v_w_in, v_q_norm_g, v_k_norm_g, v_conv_w, v_w_branch_a, v_w_branch_b, v_w_out, v_ln2_g, v_w_ffn_gate, v_w_ffn_up, v_w_ffn_down):
    weights = dict(ada_w=ada_w, ada_b=ada_b, ln1_g=ln1_g, w_in=w_in, q_norm_g=q_norm_g, k_norm_g=k_norm_g, conv_w=conv_w,
                   w_branch_a=w_branch_a, w_branch_b=w_branch_b, w_out=w_out, ln2_g=ln2_g, w_ffn_gate=w_ffn_gate,
                   w_ffn_up=w_ffn_up, w_ffn_down=w_ffn_down)
    m_in = dict(ada_w=m_ada_w, ada_b=m_ada_b, ln1_g=m_ln1_g, w_in=m_w_in, q_norm_g=m_q_norm_g, k_norm_g=m_k_norm_g,
                conv_w=m_conv_w, w_branch_a=m_w_branch_a, w_branch_b=m_w_branch_b, w_out=m_w_out, ln2_g=m_ln2_g,
                w_ffn_gate=m_w_ffn_gate, w_ffn_up=m_w_ffn_up, w_ffn_down=m_w_ffn_down)
    v_in = dict(ada_w=v_ada_w, ada_b=v_ada_b, ln1_g=v_ln1_g, w_in=v_w_in, q_norm_g=v_q_norm_g, k_norm_g=v_k_norm_g,
                conv_w=v_conv_w, w_branch_a=v_w_branch_a, w_branch_b=v_w_branch_b, w_out=v_w_out, ln2_g=v_ln2_g,
                w_ffn_gate=v_w_ffn_gate, w_ffn_up=v_w_ffn_up, w_ffn_down=v_w_ffn_down)
    names = list(weights)

    mx, my, mc = lax.axis_index("x"), lax.axis_index("y"), lax.axis_index("c")
    me = 4 * mx + 2 * my + mc
    xs, target = x[0], loss_target[0]
    t, d = xs.shape
    depth = ada_w.shape[0]
    mod_cols = ada_w.shape[2]
    conv_cols = conv_w.shape[2]
    row_w = 1024
    tile = 512 if t % 512 == 0 else t

    small = jnp.concatenate([c.reshape(-1), conv_w.reshape(-1)])
    small_n = -(-small.shape[0] // row_w) * row_w
    small = jnp.pad(small, (0, small_n - small.shape[0])).reshape(-1, row_w)
    small_all = _all_gather(small, "gather_cond").reshape(N_DEV, -1)
    c_all = small_all[:, :d]
    conv_full = jnp.transpose(small_all[:, d:d + depth * 3 * conv_cols].reshape(N_DEV, depth, 3, conv_cols), (1, 2, 0, 3)
                              ).reshape(depth, 3, N_DEV * conv_cols)

    ada_b_cols = lax.dynamic_slice_in_dim(ada_b, me * mod_cols, mod_cols, axis=1)
    mod_part = _ada_forward(c_all, ada_w, ada_b_cols)
    mod_all = _all_gather(mod_part.reshape(-1, row_w), "gather_mod").reshape(N_DEV, depth, N_DEV, mod_cols)
    mod_mine = lax.dynamic_index_in_dim(mod_all, me, axis=2, keepdims=False)
    mod = jnp.transpose(mod_mine, (1, 0, 2)).reshape(depth, 6, 1, d)

    flat = jnp.concatenate([weights[n][l].reshape(-1).astype(BF16) for l in range(depth) for n in _BIG])
    w_all = _all_gather(flat.reshape(-1, row_w), "gather_weights").reshape(N_DEV, -1)
    layer_w, off = [], 0
    short = dict(w_in="w_in", w_branch_a="w_a", w_branch_b="w_b", w_out="w_o", w_ffn_gate="w_g", w_ffn_up="w_u", w_ffn_down="w_d")
    for l in range(depth):
        wl = {}
        for n in _BIG:
            shard = weights[n].shape[1:]
            size = shard[0] * shard[1]
            wl[short[n]] = _unpack_weight(w_all[:, off:off + size], shard, n in _COL_SHARDED)
            off += size
        wl["ln1"], wl["ln2"] = ln1_g[l][None], ln2_g[l][None]
        wl["qg"] = jnp.tile(q_norm_g[l], d // HEAD_DIM)[None]
        wl["kg"] = jnp.tile(k_norm_g[l], d // HEAD_DIM)[None]
        wl["conv"] = conv_full[l]
        layer_w.append(wl)

    saved = []
    x_cur, y_prev, gate_prev = xs, None, None
    for l in range(depth):
        mods = [mod[l, k] for k in range(6)]
        s = _layer_forward(x_cur, y_prev, gate_prev, mods, layer_w[l], l, tile)
        saved.append(s)
        x_cur, y_prev, gate_prev = s["x2"], s["f"], mods[5]

    def loss_head(pid, tv, hv, bv):
        diff = tv[0] + bv[0] * tv[1] - tv[2]
        return [diff * (1.0 / d)], [_colsum(diff * diff) * (0.5 / d)]

    dx, loss_cols = _rowwise(loss_head, "loss_head", rows=t, tile=tile, tiled=[(x_cur, 0, d), (y_prev, 0, d), (target, 0, d)],
                             bcast=[gate_prev], outs=[(d, F32)], accs=[(1, d)])

    layer_g, dmods = [None] * depth, [None] * depth
    for l in reversed(range(depth)):
        mods = [mod[l, k] for k in range(6)]
        dx, layer_g[l], dmods[l] = _layer_backward(dx, saved[l], mods, layer_w[l], l, tile)
    grad_x = dx[None]

    pieces = [jnp.concatenate(dmods[l], axis=1) for l in range(depth)]
    for key in ("ln1", "ln2", "qg", "kg"):
        pieces += [layer_g[l][key] for l in range(depth)]
    pieces += [layer_g[l]["conv"].reshape(1, -1) for l in range(depth)]
    pieces.append(loss_cols)
    part_small = jnp.concatenate(pieces, axis=1).reshape(-1, row_w)
    part_all = _all_gather(part_small, "gather_small_grads")
    summed = _device_sum(part_all).reshape(-1)
    n_mod = depth * 6 * d
    dmod_all = part_all.reshape(N_DEV, -1)[:, :n_mod].reshape(N_DEV, depth, 6 * d)
    dmod_cols = jnp.transpose(lax.dynamic_slice_in_dim(dmod_all, me * mod_cols, mod_cols, axis=2), (1, 0, 2))
    g = {"ada_w": _ada_backward(c_all, dmod_cols), "ada_b": summed[:n_mod].reshape(depth, 6 * d)}
    off = n_mod
    g["ln1_g"] = summed[off:off + depth * d].reshape(depth, d)
    g["ln2_g"] = summed[off + depth * d:off + 2 * depth * d].reshape(depth, d)
    g["q_norm_g"] = summed[off + 2 * depth * d:off + 3 * depth * d].reshape(depth, d // HEAD_DIM, HEAD_DIM).sum(axis=1)
    g["k_norm_g"] = summed[off + 3 * depth * d:off + 4 * depth * d].reshape(depth, d // HEAD_DIM, HEAD_DIM).sum(axis=1)
    off += 4 * depth * d
    conv_g = summed[off:off + depth * 3 * d].reshape(depth, 3, N_DEV, conv_cols)
    g["conv_w"] = lax.dynamic_index_in_dim(conv_g, me, axis=2, keepdims=False)
    off += depth * 3 * d
    loss = jnp.sum(summed[off:off + d])

    rows8 = []
    for l in range(depth):
        gl = layer_g[l]
        rows8.append(jnp.stack([a.reshape(-1) for a in gl["w_in"]]))
        for n in _BIG[1:]:
            rows8.append(_pack_grad(gl[short[n]], n in _COL_SHARDED))
    g_abs = jnp.concatenate(rows8, axis=1)
    r_total = g_abs.shape[1] // row_w
    rs_tile = next(cand for cand in (616, 512, 256, 128, 64, 32, 16, 8) if r_total % cand == 0)
    reduced = _reduce_scatter(g_abs.reshape(N_DEV, r_total, row_w), mx, my, mc, rs_tile).reshape(-1)
    off = 0
    per_layer = {n: [] for n in _BIG}
    for l in range(depth):
        for n in _BIG:
            shard = weights[n].shape[1:]
            size = shard[0] * shard[1]
            per_layer[n].append(reduced[off:off + size].reshape(shard))
            off += size
    for n in _BIG:
        g[n] = jnp.stack(per_layer[n])

    deltas, new_m, new_v = {}, {}, {}
    for n in names:
        deltas[n], new_m[n], new_v[n] = _adamw(weights[n], g[n], m_in[n], v_in[n], f"adamw_{n}")
    return (loss, grad_x, *[g[n] for n in names], *[deltas[n] for n in names], *[new_m[n] for n in names],
            *[new_v[n] for n in names])
```

```python
import functools

import numpy as np
import jax
import jax.numpy as jnp
from jax import lax
from jax.experimental import pallas as pl
from jax.experimental.pallas import tpu as pltpu

F32, BF16 = jnp.float32, jnp.bfloat16
MESH = pl.DeviceIdType.MESH
N_DEV = 8
LANES = 128
HEAD_DIM = 64
HEAD_PAIR = 2 * HEAD_DIM
ATT_TILE = 256
EPS = 1e-6
VMEM_LIMIT = 56 * 1024 * 1024

ADAM_LR, ADAM_B1, ADAM_B2, ADAM_EPS, ADAM_WD, ADAM_STEP = 0.001, 0.9, 0.999, 1e-08, 0.01, 10

_NT = (((1,), (1,)), ((), ()))
_TN = (((0,), (0,)), ((), ()))
_NN = (((1,), (0,)), ((), ()))


def _params(n_grid):
    return pltpu.CompilerParams(dimension_semantics=("arbitrary",) * n_grid, vmem_limit_bytes=VMEM_LIMIT)


def _sigmoid(x):
    return 1.0 / (1.0 + jnp.exp(-x))


def _rowwise(fn, name, *, rows, tile, tiled=(), halos=(), bcast=(), outs=(), accs=()):
    n = rows // tile
    assert n * tile == rows
    in_specs, args = [], []
    for t in tiled:
        arr, cb, w = t[:3]
        rowmap = t[3] if len(t) > 3 else (lambda i: i)
        in_specs.append(pl.BlockSpec((tile, w), functools.partial(lambda i, cb, rowmap: (rowmap(i), cb), cb=cb, rowmap=rowmap)))
        args.append(arr)
    per_tile8 = tile // 8
    for arr, cb, w, side in halos:
        last8 = arr.shape[0] // 8 - 1
        if side == "prev":
            imap = functools.partial(lambda i, cb: (jnp.maximum(i * per_tile8 - 1, 0), cb), cb=cb)
        else:
            imap = functools.partial(lambda i, cb, last8: (jnp.minimum((i + 1) * per_tile8, last8), cb), cb=cb, last8=last8)
        in_specs.append(pl.BlockSpec((8, w), imap))
        args.append(arr)
    for arr in bcast:
        in_specs.append(pl.BlockSpec(arr.shape, functools.partial(lambda i, nd: (0,) * nd, nd=arr.ndim)))
        args.append(arr)
    out_shape = [jax.ShapeDtypeStruct((rows, w), dt) for w, dt in outs] + [jax.ShapeDtypeStruct(s, F32) for s in accs]
    out_specs = [pl.BlockSpec((tile, w), lambda i: (i, 0)) for w, _ in outs] + [pl.BlockSpec(s, lambda i: (0, 0)) for s in accs]
    nt, nh, nb, no, na = len(tiled), len(halos), len(bcast), len(outs), len(accs)

    def body(*refs):
        pid = pl.program_id(0)
        tv = [r[...] for r in refs[:nt]]
        hv = [r[...] for r in refs[nt:nt + nh]]
        bv = [r[...] for r in refs[nt + nh:nt + nh + nb]]
        out_refs = refs[nt + nh + nb:nt + nh + nb + no]
        acc_refs = refs[nt + nh + nb + no:]
        ov, av = fn(pid, tv, hv, bv)
        for r, v in zip(out_refs, ov):
            r[...] = v.astype(r.dtype)
        if na:
            @pl.when(pid == 0)
            def _():
                for r in acc_refs:
                    r[...] = jnp.zeros(r.shape, F32)
            for r, v in zip(acc_refs, av):
                r[...] += v

    res = pl.pallas_call(body, name=name, grid=(n,), in_specs=in_specs, out_specs=out_specs, out_shape=out_shape,
                         compiler_params=_params(1))(*args)
    return res


def _colsum(v):
    return jnp.sum(v, axis=0, keepdims=True)


def _matmul(pairs, mode, *, m, n, k, tm, tn, out_dtype, name):
    tm, tn = min(tm, m), min(tn, n)
    assert m % tm == 0 and n % tn == 0
    in_specs, args = [], []
    for a, acb, b, bcb in pairs:
        if mode == "tn":
            in_specs.append(pl.BlockSpec((k, tm), functools.partial(lambda i, j, o: (0, o + i), o=acb)))
            in_specs.append(pl.BlockSpec((k, tn), functools.partial(lambda i, j, o: (0, o + j), o=bcb)))
        elif mode == "nn":
            in_specs.append(pl.BlockSpec((tm, k), functools.partial(lambda i, j, o: (i, o), o=acb)))
            in_specs.append(pl.BlockSpec((k, tn), functools.partial(lambda i, j, o: (0, o + j), o=bcb)))
        else:
            in_specs.append(pl.BlockSpec((tm, k), functools.partial(lambda i, j, o: (i, o), o=acb)))
            in_specs.append(pl.BlockSpec((tn, k), functools.partial(lambda i, j, o: (j, o), o=bcb)))
        args += [a, b]
    dims = {"nn": _NN, "nt": _NT, "tn": _TN}[mode]
    npairs = len(pairs)

    def body(*refs):
        o_ref = refs[2 * npairs]
        acc = None
        for p in range(npairs):
            d = lax.dot_general(refs[2 * p][...].astype(BF16), refs[2 * p + 1][...].astype(BF16), dims,
                                preferred_element_type=F32)
            acc = d if acc is None else acc + d
        o_ref[...] = acc.astype(o_ref.dtype)

    return pl.pallas_call(body, name=name, grid=(m // tm, n // tn), in_specs=in_specs,
                          out_specs=pl.BlockSpec((tm, tn), lambda i, j: (i, j)),
                          out_shape=jax.ShapeDtypeStruct((m, n), out_dtype), compiler_params=_params(2))(*args)


def _scan_matrix(kind, n):
    r = np.arange(n)
    tri = (r[:, None] > r[None, :]) if kind == "suffix" else (r[:, None] < r[None, :])
    half = np.concatenate([tri.astype(np.float32), np.ones((n, LANES), np.float32)], axis=1)
    return jnp.asarray(np.concatenate([half, half], axis=0), BF16)


def _head_sum_matrix():
    r = np.arange(LANES)
    bd = (r[:, None] // HEAD_DIM == r[None, :] // HEAD_DIM).astype(np.float32)
    return jnp.asarray(np.concatenate([bd, bd], axis=0), BF16)


def _split_cat(v):
    hi = v.astype(BF16)
    lo = (v - hi.astype(F32)).astype(BF16)
    return jnp.concatenate([hi, lo], axis=1)


def _head_sums(v, bd2):
    hi = v.astype(BF16)
    lo = (v - hi.astype(F32)).astype(BF16)
    parts = []
    for g in range(v.shape[1] // LANES):
        sl = slice(g * LANES, (g + 1) * LANES)
        parts.append(jnp.dot(jnp.concatenate([hi[:, sl], lo[:, sl]], axis=1), bd2, preferred_element_type=F32))
    return jnp.concatenate(parts, axis=1)


def _running_sums(v, scan, carry, reverse):
    nb = v.shape[1] // LANES
    outs = [None] * nb
    for b in (reversed(range(nb)) if reverse else range(nb)):
        cs = jnp.dot(_split_cat(v[:, b * LANES:(b + 1) * LANES]), scan, preferred_element_type=F32)
        outs[b] = carry + cs[:, :LANES]
        carry = carry + cs[:, LANES:]
    return jnp.concatenate(outs, axis=1), carry


def _sb_tile(qm, ktile, scan_suffix, carry_n, mask):
    z = lax.dot_general(qm, ktile, _NT, preferred_element_type=F32)
    sp = jnp.log(1.0 + jnp.exp(-jnp.abs(z)))
    log_not = -(jnp.maximum(z, 0.0) + sp)
    log_beta = z + log_not
    if mask is not None:
        log_not = jnp.where(mask, log_not, 0.0)
    tail, carry_n = _running_sums(log_not, scan_suffix, carry_n, True)
    a = jnp.exp(log_beta + tail)
    if mask is not None:
        a = jnp.where(mask, a, 0.0)
    return a, log_beta, carry_n


def _head_masks(tq):
    lane = lax.broadcasted_iota(jnp.int32, (tq, HEAD_PAIR), 1)
    return [(lane // HEAD_DIM) == hh for hh in range(2)]


def _diag_mask(tq):
    return lax.broadcasted_iota(jnp.int32, (tq, tq), 1) < lax.broadcasted_iota(jnp.int32, (tq, tq), 0)


def _attention_fwd(qn, kn, vb, name):
    t, d = qn.shape
    tq = min(ATT_TILE, t)
    nq, hp = t // tq, d // HEAD_PAIR
    scan_suffix = _scan_matrix("suffix", LANES)

    def body(q_ref, k_ref, v_ref, sc_ref, o_ref):
        qi = pl.program_id(1)
        q2 = q_ref[...].astype(F32)
        scan = sc_ref[...]
        heads, diag = _head_masks(tq), _diag_mask(tq)
        qms = [jnp.where(h, q2, 0.0).astype(BF16) for h in heads]

        def rows_of(j):
            return pl.ds(pl.multiple_of(j * tq, tq), tq)

        state = []
        for hh in range(2):
            a, _, carry = _sb_tile(qms[hh], k_ref[rows_of(qi), :], scan, jnp.zeros((tq, LANES), F32), diag)
            state += [carry, jnp.dot(a.astype(BF16), v_ref[rows_of(qi), :], preferred_element_type=F32)]

        def step(jj, c):
            j = qi - 1 - jj
            kt, vt = k_ref[rows_of(j), :], v_ref[rows_of(j), :]
            new = []
            for hh in range(2):
                a, _, carry = _sb_tile(qms[hh], kt, scan, c[2 * hh], None)
                new += [carry, c[2 * hh + 1] + jnp.dot(a.astype(BF16), vt, preferred_element_type=F32)]
            return tuple(new)

        state = lax.fori_loop(0, qi, step, tuple(state))
        o_ref[...] = jnp.where(heads[0], state[1], state[3]).astype(o_ref.dtype)

    return pl.pallas_call(
        body, name=name, grid=(hp, nq),
        in_specs=[pl.BlockSpec((tq, HEAD_PAIR), lambda h, i: (i, h)),
                  pl.BlockSpec((t, HEAD_PAIR), lambda h, i: (0, h)),
                  pl.BlockSpec((t, HEAD_PAIR), lambda h, i: (0, h)),
                  pl.BlockSpec(scan_suffix.shape, lambda h, i: (0, 0))],
        out_specs=pl.BlockSpec((tq, HEAD_PAIR), lambda h, i: (i, h)),
        out_shape=jax.ShapeDtypeStruct((t, d), BF16), compiler_params=_params(2))(qn, kn, vb, scan_suffix)


def _attention_bwd(qn, kn, vb, dob, name):
    t, d = qn.shape
    tq = min(ATT_TILE, t)
    nq, hp = t // tq, d // HEAD_PAIR
    scan_suffix, scan_prefix = _scan_matrix("suffix", LANES), _scan_matrix("prefix", LANES)

    def body(q_ref, k_ref, v_ref, do_ref, ss_ref, sp_ref, dq_ref, dk_ref, dv_ref, g_s, b_s):
        qi = pl.program_id(1)

        @pl.when(qi == 0)
        def _():
            dk_ref[...] = jnp.zeros(dk_ref.shape, F32)
            dv_ref[...] = jnp.zeros(dv_ref.shape, F32)

        q2, do2 = q_ref[...].astype(F32), do_ref[...].astype(F32)
        ssuf, spre = ss_ref[...], sp_ref[...]
        heads, diag = _head_masks(tq), _diag_mask(tq)
        qms = [jnp.where(h, q2, 0.0).astype(BF16) for h in heads]
        doms = [jnp.where(h, do2, 0.0).astype(BF16) for h in heads]

        def rows_of(j):
            return pl.ds(pl.multiple_of(j * tq, tq), tq)

        def pass_one(j, carries, mask):
            kt, vt = k_ref[rows_of(j), :], v_ref[rows_of(j), :]
            new, dv = [], None
            for hh in range(2):
                a, log_beta, carry = _sb_tile(qms[hh], kt, ssuf, carries[hh], mask)
                beta = jnp.exp(log_beta)
                if mask is not None:
                    beta = jnp.where(mask, beta, 0.0)
                d_a = lax.dot_general(doms[hh], vt, _NT, preferred_element_type=F32)
                g_s[hh, j] = a * d_a
                b_s[hh, j] = beta
                part = lax.dot_general(a.astype(BF16), doms[hh], _TN, preferred_element_type=F32)
                dv = part if dv is None else dv + part
                new.append(carry)
            dv_ref[rows_of(j), :] += dv
            return tuple(new)

        zero = jnp.zeros((tq, LANES), F32)
        carries = pass_one(qi, (zero, zero), diag)
        lax.fori_loop(0, qi, lambda jj, c: pass_one(qi - 1 - jj, c, None), carries)

        def pass_two(j, c):
            kt = k_ref[rows_of(j), :]
            new, dk = [], None
            for hh in range(2):
                prefix, dq = c[2 * hh], c[2 * hh + 1]
                g, beta = g_s[hh, j], b_s[hh, j]
                before, prefix = _running_sums(g, spre, prefix, False)
                dz = (g - beta * (g + before)).astype(BF16)
                part = lax.dot_general(dz, qms[hh], _TN, preferred_element_type=F32)
                dk = part if dk is None else dk + part
                new += [prefix, dq + jnp.dot(dz, kt, preferred_element_type=F32)]
            dk_ref[rows_of(j), :] += dk
            return tuple(new)

        zq = jnp.zeros((tq, HEAD_PAIR), F32)
        out = lax.fori_loop(0, qi + 1, pass_two, (zero, zq, zero, zq))
        dq_ref[...] = jnp.where(heads[0], out[1], out[3])

    blk = pl.BlockSpec((tq, HEAD_PAIR), lambda h, i: (i, h))
    col_spec = pl.BlockSpec((t, HEAD_PAIR), lambda h, i: (0, h))
    const = pl.BlockSpec(scan_suffix.shape, lambda h, i: (0, 0))
    full = jax.ShapeDtypeStruct((t, d), F32)
    return pl.pallas_call(
        body, name=name, grid=(hp, nq), in_specs=[blk, col_spec, col_spec, blk, const, const],
        out_specs=[blk, col_spec, col_spec], out_shape=[full, full, full],
        scratch_shapes=[pltpu.VMEM((2, nq, tq, tq), F32), pltpu.VMEM((2, nq, tq, tq), F32)],
        compiler_params=_params(2))(qn, kn, vb, dob, scan_suffix, scan_prefix)


def _all_gather(x2d, name):
    r, w = x2d.shape

    def body(x_ref, out_ref, send_sems, recv_sems, local_sem):
        x, y, c = lax.axis_index("x"), lax.axis_index("y"), lax.axis_index("c")
        me, sibling = (x, y, c), (x, y, 1 - c)
        chips = [(1 - x, y), (x, 1 - y), (1 - x, 1 - y)]

        def rows(px, py, pc):
            return out_ref.at[4 * px + 2 * py + pc]

        def copy(k, block, to, src=None):
            return pltpu.make_async_remote_copy(src_ref=rows(*block) if src is None else src, dst_ref=rows(*block),
                                                send_sem=send_sems.at[k], recv_sem=recv_sems.at[k],
                                                device_id=to, device_id_type=MESH)

        mine = pltpu.make_async_copy(x_ref, rows(*me), local_sem)
        mine.start()
        first = [copy(0, me, sibling, src=x_ref)]
        first += [copy(1 + j, me, (*chip, c), src=x_ref) for j, chip in enumerate(chips)]
        for cp in first:
            cp.start()
        passed = [copy(4 + j, (*chip, c), sibling) for j, chip in enumerate(chips)]
        for j, chip in enumerate(chips):
            copy(1 + j, (*chip, c), me).wait_recv()
            passed[j].start()
        copy(0, sibling, me).wait_recv()
        for j, chip in enumerate(chips):
            copy(4 + j, (*chip, 1 - c), me).wait_recv()
        for cp in first + passed:
            cp.wait_send()
        mine.wait()

    return pl.pallas_call(
        body, name=name, out_shape=jax.ShapeDtypeStruct((N_DEV, r, w), x2d.dtype),
        in_specs=[pl.BlockSpec(memory_space=pl.ANY)], out_specs=pl.BlockSpec(memory_space=pl.ANY),
        scratch_shapes=[pltpu.SemaphoreType.DMA((7,)), pltpu.SemaphoreType.DMA((7,)), pltpu.SemaphoreType.DMA])(x2d)


def _exchange_sibling(g_rel, name):
    _, _, r, w = g_rel.shape

    def body(g_ref, recv_ref, send_sems, recv_sems):
        x, y, c = lax.axis_index("x"), lax.axis_index("y"), lax.axis_index("c")
        copies = [pltpu.make_async_remote_copy(src_ref=g_ref.at[j, 1], dst_ref=recv_ref.at[j], send_sem=send_sems.at[j],
                                               recv_sem=recv_sems.at[j], device_id=(x, y, 1 - c), device_id_type=MESH)
                  for j in range(4)]
        for cp in copies:
            cp.start()
        for cp in copies:
            cp.wait_recv()
        for cp in copies:
            cp.wait_send()

    return pl.pallas_call(
        body, name=name, out_shape=jax.ShapeDtypeStruct((4, r, w), g_rel.dtype),
        in_specs=[pl.BlockSpec(memory_space=pl.ANY)], out_specs=pl.BlockSpec(memory_space=pl.ANY),
        scratch_shapes=[pltpu.SemaphoreType.DMA((4,)), pltpu.SemaphoreType.DMA((4,))])(g_rel)


def _exchange_chips(part, name):
    _, r, w = part.shape

    def body(p_ref, recv_ref, send_sems, recv_sems):
        x, y, c = lax.axis_index("x"), lax.axis_index("y"), lax.axis_index("c")
        targets = {1: (x, 1 - y, c), 2: (1 - x, y, c), 3: (1 - x, 1 - y, c)}
        copies = [pltpu.make_async_remote_copy(src_ref=p_ref.at[j], dst_ref=recv_ref.at[j - 1], send_sem=send_sems.at[j - 1],
                                               recv_sem=recv_sems.at[j - 1], device_id=targets[j], device_id_type=MESH)
                  for j in (1, 2, 3)]
        for cp in copies:
            cp.start()
        for cp in copies:
            cp.wait_recv()
        for cp in copies:
            cp.wait_send()

    return pl.pallas_call(
        body, name=name, out_shape=jax.ShapeDtypeStruct((3, r, w), part.dtype),
        in_specs=[pl.BlockSpec(memory_space=pl.ANY)], out_specs=pl.BlockSpec(memory_space=pl.ANY),
        scratch_shapes=[pltpu.SemaphoreType.DMA((3,)), pltpu.SemaphoreType.DMA((3,))])(part)


def _to_relative(a, x, y, c):
    a = a.reshape((2, 2, 2) + a.shape[1:])
    a = jnp.where(x == 1, jnp.flip(a, 0), a)
    a = jnp.where(y == 1, jnp.flip(a, 1), a)
    a = jnp.where(c == 1, jnp.flip(a, 2), a)
    return a.reshape((4, 2) + a.shape[3:])


def _reduce_scatter(g_abs, x, y, c, tile):
    _, r, w = g_abs.shape
    n = r // tile
    g_rel = _to_relative(g_abs, x, y, c)
    recv1 = _exchange_sibling(g_rel, "rs_exchange_sibling")
    add = lambda pid, tv, hv, bv: ([sum(tv[1:], tv[0])], [])
    part, = _rowwise(add, "rs_add_sibling", rows=4 * r, tile=tile,
                     tiled=[(g_rel.reshape(8 * r, w), 0, w, lambda s: 2 * (s // n) * n + s % n), (recv1.reshape(4 * r, w), 0, w)],
                     outs=[(w, F32)])
    recv2 = _exchange_chips(part.reshape(4, r, w), "rs_exchange_chips").reshape(3 * r, w)
    total, = _rowwise(add, "rs_add_chips", rows=r, tile=tile,
                      tiled=[(part, 0, w), (recv2, 0, w), (recv2, 0, w, lambda i: i + n), (recv2, 0, w, lambda i: i + 2 * n)],
                      outs=[(w, F32)])
    return total


def _ada_forward(c_all, ada_w, ada_b_cols):
    depth, d, cols = ada_w.shape

    def body(c_ref, w_ref, b_ref, o_ref):
        cv = c_ref[...]
        act = cv * _sigmoid(cv)
        o_ref[...] = jnp.dot(act, w_ref[...], preferred_element_type=F32, precision=lax.Precision.HIGHEST) + b_ref[...]

    return pl.pallas_call(
        body, name="ada_forward", grid=(depth,),
        in_specs=[pl.BlockSpec((N_DEV, d), lambda l: (0, 0)), pl.BlockSpec((None, d, cols), lambda l: (l, 0, 0)),
                  pl.BlockSpec((None, 1, cols), lambda l: (l, 0, 0))],
        out_specs=pl.BlockSpec((None, N_DEV, cols), lambda l: (l, 0, 0)),
        out_shape=jax.ShapeDtypeStruct((depth, N_DEV, cols), F32), compiler_params=_params(1))(
            c_all, ada_w, ada_b_cols.reshape(depth, 1, cols))


def _ada_backward(c_all, dmod_cols):
    depth, _, cols = dmod_cols.shape
    d = c_all.shape[1]

    def body(c_ref, g_ref, o_ref):
        cv = c_ref[...]
        act = cv * _sigmoid(cv)
        o_ref[...] = lax.dot_general(act, g_ref[...], _TN, preferred_element_type=F32, precision=lax.Precision.HIGHEST)

    return pl.pallas_call(
        body, name="ada_backward", grid=(depth,),
        in_specs=[pl.BlockSpec((N_DEV, d), lambda l: (0, 0)), pl.BlockSpec((None, N_DEV, cols), lambda l: (l, 0, 0))],
        out_specs=pl.BlockSpec((None, d, cols), lambda l: (l, 0, 0)),
        out_shape=jax.ShapeDtypeStruct((depth, d, cols), F32), compiler_params=_params(1))(c_all, dmod_cols)


def _device_sum(a):
    _, r, w = a.shape

    def body(a_ref, o_ref):
        acc = a_ref[0]
        for dev in range(1, N_DEV):
            acc = acc + a_ref[dev]
        o_ref[...] = acc

    return pl.pallas_call(body, name="device_sum", out_shape=jax.ShapeDtypeStruct((r, w), F32),
                          in_specs=[pl.BlockSpec(memory_space=pltpu.VMEM)], out_specs=pl.BlockSpec(memory_space=pltpu.VMEM))(a)


def _adamw(w, g, m, v, name):
    shape = w.shape
    cols = shape[-1]
    rows = int(np.prod(shape[:-1]))
    tile = rows
    for cand in (512, 352, 256):
        if rows > cand and rows % cand == 0:
            tile = cand
            break

    def fn(pid, tv, hv, bv):
        wv, gv, mv, vv = tv
        mn = ADAM_B1 * mv + (1.0 - ADAM_B1) * gv
        vn = ADAM_B2 * vv + (1.0 - ADAM_B2) * (gv * gv)
        m_hat = mn / (1.0 - ADAM_B1 ** ADAM_STEP)
        v_hat = vn / (1.0 - ADAM_B2 ** ADAM_STEP)
        delta = -ADAM_LR * (m_hat / (jnp.sqrt(v_hat) + ADAM_EPS) + ADAM_WD * wv)
        return [delta, mn, vn], []

    res = _rowwise(fn, name, rows=rows, tile=tile, tiled=[(a.reshape(rows, cols), 0, cols) for a in (w, g, m, v)],
                   outs=[(cols, F32)] * 3)
    return [r.reshape(shape) for r in res]


def _norm_modulate(x, y, gate, ln_g, scale, shift, name, tile):
    t, d = x.shape

    def fn(pid, tv, hv, bv):
        if y is None:
            xn = tv[0]
            g_ln, sc, sh = bv
        else:
            g_gate, g_ln, sc, sh = bv
            xn = tv[0] + g_gate * tv[1]
        r = lax.rsqrt(jnp.mean(xn * xn, axis=-1, keepdims=True) + EPS)
        h = (xn * r * g_ln) * (1.0 + sc) + sh
        return ([h] if y is None else [xn, h]), []

    if y is None:
        h, = _rowwise(fn, name, rows=t, tile=tile, tiled=[(x, 0, d)], bcast=[ln_g, scale, shift], outs=[(d, BF16)])
        return x, h
    xn, h = _rowwise(fn, name, rows=t, tile=tile, tiled=[(x, 0, d), (y, 0, d)], bcast=[gate, ln_g, scale, shift],
                     outs=[(d, F32), (d, BF16)])
    return xn, h


def _norm_backward(dh, x, dres, ln_g, scale, name, tile):
    t, d = x.shape

    def fn(pid, tv, hv, bv):
        dhv, xv, dr = tv
        g_ln, sc = bv
        r = lax.rsqrt(jnp.mean(xv * xv, axis=-1, keepdims=True) + EPS)
        xn = xv * r
        dxn = dhv * (1.0 + sc) * g_ln
        dx = dr + r * (dxn - xn * jnp.mean(dxn * xn, axis=-1, keepdims=True))
        return [dx], [_colsum(dhv), _colsum(dhv * (xn * g_ln)), _colsum(dhv * (1.0 + sc) * xn)]

    return _rowwise(fn, name, rows=t, tile=tile, tiled=[(dh, 0, d), (x, 0, d), (dres, 0, d)], bcast=[ln_g, scale],
                    outs=[(d, F32)], accs=[(1, d)] * 3)


def _shift_rows(u, halo, k, pid, first_tile_zero):
    rows = lax.broadcasted_iota(jnp.int32, u.shape, 0)
    halo = halo * jnp.where(pid == 0, 0.0, 1.0) if first_tile_zero else halo
    out = pltpu.roll(u, k, axis=0)
    for j in range(k):
        out = jnp.where(rows == j, halo[8 - k + j:8 - k + j + 1, :], out)
    return out


def _shift_rows_up(u, halo, k, pid, n_tiles):
    tile = u.shape[0]
    rows = lax.broadcasted_iota(jnp.int32, u.shape, 0)
    halo = halo * jnp.where(pid == n_tiles - 1, 0.0, 1.0)
    out = pltpu.roll(u, tile - k, axis=0)
    for j in range(k):
        out = jnp.where(rows == tile - k + j, halo[j:j + 1, :], out)
    return out


def _layer_forward(x_in, y_prev, gate_prev, mod, w, l, tile):
    sh1, sc1, g1, sh2, sc2, g2 = mod
    t, d = x_in.shape
    f = w["w_g"].shape[1]
    bd2 = _head_sum_matrix()
    s = {}
    s["x"], s["h"] = _norm_modulate(x_in, y_prev, gate_prev, w["ln1"], sc1, sh1, f"l{l}_norm1", tile)
    p = _matmul([(s["h"], 0, w["w_in"], 0)], "nn", m=t, n=8 * d, k=d, tm=1024, tn=512, out_dtype=F32, name=f"l{l}_in_proj")
    s["p"] = p

    def qk_norm(pid, tv, hv, bv):
        qr, kr, vr = tv
        qg, kg, bd = bv

        def nrm(xv, g):
            r = lax.rsqrt(_head_sums(xv * xv, bd) * (1.0 / HEAD_DIM) + EPS)
            return xv * r * g
        return [nrm(qr, qg) * 0.125, nrm(kr, kg), vr], []

    s["qn"], s["kn"], s["vb"] = _rowwise(qk_norm, f"l{l}_qk_norm", rows=t, tile=tile, tiled=[(p, 0, d), (p, 1, d), (p, 2, d)],
                                         bcast=[w["qg"], w["kg"], bd2], outs=[(d, BF16)] * 3)
    s["ya"] = _attention_fwd(s["qn"], s["kn"], s["vb"], f"l{l}_attention")

    def conv_fwd(pid, tv, hv, bv):
        cb, cc, cx = tv
        u = cc * cx
        hu = hv[0] * hv[1]
        cw = bv[0]
        conv = cw[0:1, :] * _shift_rows(u, hu, 2, pid, True) + cw[1:2, :] * _shift_rows(u, hu, 1, pid, True) + cw[2:3, :] * u
        return [cb * conv], []

    s["yb"], = _rowwise(conv_fwd, f"l{l}_conv", rows=t, tile=tile, tiled=[(p, 3, d), (p, 4, d), (p, 5, d)],
                        halos=[(p, 4, d, "prev"), (p, 5, d, "prev")], bcast=[w["conv"]], outs=[(d, BF16)])
    s["a"] = _matmul([(s["ya"], 0, w["w_a"], 0)], "nn", m=t, n=d, k=d, tm=1024, tn=512, out_dtype=F32, name=f"l{l}_branch_a")
    s["b"] = _matmul([(s["yb"], 0, w["w_b"], 0)], "nn", m=t, n=d, k=d, tm=1024, tn=512, out_dtype=F32, name=f"l{l}_branch_b")

    def merge(pid, tv, hv, bv):
        av, bvv, ga, gb = tv
        return [_sigmoid(ga) * av + _sigmoid(gb) * bvv], []

    s["merged"], = _rowwise(merge, f"l{l}_merge", rows=t, tile=tile, tiled=[(s["a"], 0, d), (s["b"], 0, d), (p, 6, d), (p, 7, d)],
                            outs=[(d, BF16)])
    s["mo"] = _matmul([(s["merged"], 0, w["w_o"], 0)], "nn", m=t, n=d, k=d, tm=1024, tn=512, out_dtype=F32, name=f"l{l}_out_proj")
    s["x2"], s["h2"] = _norm_modulate(s["x"], s["mo"], g1, w["ln2"], sc2, sh2, f"l{l}_norm2", tile)
    fn_tile = f // 2
    s["g"] = _matmul([(s["h2"], 0, w["w_g"], 0)], "nn", m=t, n=f, k=d, tm=512, tn=fn_tile, out_dtype=F32, name=f"l{l}_ffn_gate")
    s["u"] = _matmul([(s["h2"], 0, w["w_u"], 0)], "nn", m=t, n=f, k=d, tm=512, tn=fn_tile, out_dtype=F32, name=f"l{l}_ffn_up")

    def swiglu(pid, tv, hv, bv):
        gv, uv = tv
        return [gv * _sigmoid(gv) * uv], []

    s["s"], = _rowwise(swiglu, f"l{l}_swiglu", rows=t, tile=tile // 2, tiled=[(s["g"], 0, f), (s["u"], 0, f)], outs=[(f, BF16)])
    s["f"] = _matmul([(s["s"], 0, w["w_d"], 0)], "nn", m=t, n=d, k=f, tm=512, tn=512, out_dtype=F32, name=f"l{l}_ffn_down")
    return s


def _layer_backward(dx3, s, mod, w, l, tile):
    sh1, sc1, g1, sh2, sc2, g2 = mod
    t, d = dx3.shape
    f = w["w_g"].shape[1]
    p = s["p"]
    bd2 = _head_sum_matrix()
    n_tiles = t // tile
    grads = {}

    def gate_bwd(pid, tv, hv, bv):
        return [tv[0] * bv[0]], [_colsum(tv[0] * tv[1])]

    df, dg2 = _rowwise(gate_bwd, f"l{l}_bwd_gate2", rows=t, tile=tile, tiled=[(dx3, 0, d), (s["f"], 0, d)], bcast=[g2],
                       outs=[(d, BF16)], accs=[(1, d)])
    grads["w_d"] = _matmul([(s["s"], 0, df, 0)], "tn", m=f, n=d, k=t, tm=256, tn=512, out_dtype=F32, name=f"l{l}_dw_down")
    ds = _matmul([(df, 0, w["w_d"], 0)], "nt", m=t, n=f, k=d, tm=512, tn=f // 2, out_dtype=F32, name=f"l{l}_d_swiglu")

    def swiglu_bwd(pid, tv, hv, bv):
        dsv, gv, uv = tv
        sig = _sigmoid(gv)
        return [dsv * uv * (sig * (1.0 + gv * (1.0 - sig))), dsv * (gv * sig)], []

    dgt, dup = _rowwise(swiglu_bwd, f"l{l}_bwd_swiglu", rows=t, tile=tile // 2, tiled=[(ds, 0, f), (s["g"], 0, f), (s["u"], 0, f)],
                        outs=[(f, BF16)] * 2)
    grads["w_g"] = _matmul([(s["h2"], 0, dgt, 0)], "tn", m=d, n=f, k=t, tm=256, tn=f // 2, out_dtype=F32, name=f"l{l}_dw_gate")
    grads["w_u"] = _matmul([(s["h2"], 0, dup, 0)], "tn", m=d, n=f, k=t, tm=256, tn=f // 2, out_dtype=F32, name=f"l{l}_dw_up")
    dh2 = _matmul([(dgt, 0, w["w_g"], 0), (dup, 0, w["w_u"], 0)], "nt", m=t, n=d, k=f, tm=512, tn=512, out_dtype=F32,
                  name=f"l{l}_dh2")
    dx2, dsh2, dsc2, grads["ln2"] = _norm_backward(dh2, s["x2"], dx3, w["ln2"], sc2, f"l{l}_bwd_norm2", tile)

    dmo, dg1 = _rowwise(gate_bwd, f"l{l}_bwd_gate1", rows=t, tile=tile, tiled=[(dx2, 0, d), (s["mo"], 0, d)], bcast=[g1],
                        outs=[(d, BF16)], accs=[(1, d)])
    grads["w_o"] = _matmul([(s["merged"], 0, dmo, 0)], "tn", m=d, n=d, k=t, tm=256, tn=512, out_dtype=F32, name=f"l{l}_dw_out")
    dmerged = _matmul([(dmo, 0, w["w_o"], 0)], "nt", m=t, n=d, k=d, tm=1024, tn=512, out_dtype=F32, name=f"l{l}_d_merged")

    def merge_bwd(pid, tv, hv, bv):
        dm, av, bvv, ga, gb = tv
        sa, sb = _sigmoid(ga), _sigmoid(gb)
        return [dm * sa, dm * sb, dm * av * (sa * (1.0 - sa)), dm * bvv * (sb * (1.0 - sb))], []

    d_a, d_b, dga, dgb = _rowwise(merge_bwd, f"l{l}_bwd_merge", rows=t, tile=tile,
                                  tiled=[(dmerged, 0, d), (s["a"], 0, d), (s["b"], 0, d), (p, 6, d), (p, 7, d)], outs=[(d, BF16)] * 4)
    grads["w_a"] = _matmul([(s["ya"], 0, d_a, 0)], "tn", m=d, n=d, k=t, tm=256, tn=512, out_dtype=F32, name=f"l{l}_dw_a")
    grads["w_b"] = _matmul([(s["yb"], 0, d_b, 0)], "tn", m=d, n=d, k=t, tm=256, tn=512, out_dtype=F32, name=f"l{l}_dw_b")
    dya = _matmul([(d_a, 0, w["w_a"], 0)], "nt", m=t, n=d, k=d, tm=1024, tn=512, out_dtype=BF16, name=f"l{l}_d_ya")
    dyb = _matmul([(d_b, 0, w["w_b"], 0)], "nt", m=t, n=d, k=d, tm=1024, tn=512, out_dtype=F32, name=f"l{l}_d_yb")

    def conv_bwd(pid, tv, hv, bv):
        dy, cb, cc, cx = tv
        cw = bv[0]
        u, hu = cc * cx, hv[0] * hv[1]
        u1, u2 = _shift_rows(u, hu, 1, pid, True), _shift_rows(u, hu, 2, pid, True)
        conv = cw[0:1, :] * u2 + cw[1:2, :] * u1 + cw[2:3, :] * u
        dconv, hd = dy * cb, hv[2] * hv[3]
        du = (cw[2:3, :] * dconv + cw[1:2, :] * _shift_rows_up(dconv, hd, 1, pid, n_tiles)
              + cw[0:1, :] * _shift_rows_up(dconv, hd, 2, pid, n_tiles))
        return [dy * conv, du * cx, du * cc], [_colsum(dconv * u2), _colsum(dconv * u1), _colsum(dconv * u)]

    dcb, dcc, dcx, dcw0, dcw1, dcw2 = _rowwise(
        conv_bwd, f"l{l}_bwd_conv", rows=t, tile=tile, tiled=[(dyb, 0, d), (p, 3, d), (p, 4, d), (p, 5, d)],
        halos=[(p, 4, d, "prev"), (p, 5, d, "prev"), (dyb, 0, d, "next"), (p, 3, d, "next")], bcast=[w["conv"]],
        outs=[(d, BF16)] * 3, accs=[(1, d)] * 3)
    grads["conv"] = jnp.concatenate([dcw0, dcw1, dcw2], axis=0)

    dqs, dkn, dv = _attention_bwd(s["qn"], s["kn"], s["vb"], dya, f"l{l}_bwd_attention")

    def qk_norm_bwd(pid, tv, hv, bv):
        dq, dk, qr, kr, dvv = tv
        qg, kg, bd = bv

        def bwd(dy, xv, g):
            r = lax.rsqrt(_head_sums(xv * xv, bd) * (1.0 / HEAD_DIM) + EPS)
            yv = xv * r
            dyn = dy * g
            dx = r * (dyn - yv * (_head_sums(dyn * yv, bd) * (1.0 / HEAD_DIM)))
            return dx, _colsum(dy * yv)

        dxq, dgq = bwd(dq * 0.125, qr, qg)
        dxk, dgk = bwd(dk, kr, kg)
        return [dxq, dxk, dvv], [dgq, dgk]

    dqr, dkr, dvb, grads["qg"], grads["kg"] = _rowwise(
        qk_norm_bwd, f"l{l}_bwd_qk_norm", rows=t, tile=tile, tiled=[(dqs, 0, d), (dkn, 0, d), (p, 0, d), (p, 1, d), (dv, 0, d)],
        bcast=[w["qg"], w["kg"], bd2], outs=[(d, BF16)] * 3, accs=[(1, d)] * 2)

    dp = [dqr, dkr, dvb, dcb, dcc, dcx, dga, dgb]
    grads["w_in"] = [_matmul([(s["h"], 0, dpk, 0)], "tn", m=d, n=d, k=t, tm=256, tn=512, out_dtype=F32, name=f"l{l}_dw_in{k}")
                     for k, dpk in enumerate(dp)]
    dh = _matmul([(dpk, 0, w["w_in"], k) for k, dpk in enumerate(dp)], "nt", m=t, n=d, k=d, tm=512, tn=512, out_dtype=F32,
                 name=f"l{l}_dh")
    dx, dsh1, dsc1, grads["ln1"] = _norm_backward(dh, s["x"], dx2, w["ln1"], sc1, f"l{l}_bwd_norm1", tile)
    return dx, grads, [dsh1, dsc1, dg1, dsh2, dsc2, dg2]


_BIG = ["w_in", "w_branch_a", "w_branch_b", "w_out", "w_ffn_gate", "w_ffn_up", "w_ffn_down"]
_COL_SHARDED = {"w_in", "w_ffn_gate", "w_ffn_up"}


def _unpack_weight(flat8, shard_shape, col_sharded):
    r, c = shard_shape
    a = flat8.reshape(N_DEV, r, c)
    if col_sharded:
        return jnp.transpose(a, (1, 0, 2)).reshape(r, N_DEV * c)
    return a.reshape(N_DEV * r, c)


def _pack_grad(full, col_sharded):
    r, c = full.shape
    if col_sharded:
        return jnp.transpose(full.reshape(r, N_DEV, c // N_DEV), (1, 0, 2)).reshape(N_DEV, -1)
    return full.reshape(N_DEV, -1)


def kernel(x, c, ada_w, ada_b, ln1_g, w_in, q_norm_g, k_norm_g, conv_w, w_branch_a, w_branch_b, w_out, ln2_g, w_ffn_gate, w_ffn_up, w_ffn_down, loss_target, m_ada_w, m_ada_b, m_ln1_g, m_w_in, m_q_norm_g, m_k_norm_g, m_conv_w, m_w_branch_a, m_w_branch_b, m_w_out, m_ln2_g, m_w_ffn_gate, m_w_ffn_up, m_w_ffn_down, v_ada_w, v_ada_b, v_ln1_g, v_w_in, v_q_norm_g, v_k_norm_g, v_conv_w, v_w_branch_a, v_w_branch_b, v_w_out, v_ln2_g, v_w_ffn_gate, v_w_ffn_up, v_w_ffn_down):
    weights = dict(ada_w=ada_w, ada_b=ada_b, ln1_g=ln1_g, w_in=w_in, q_norm_g=q_norm_g, k_norm_g=k_norm_g, conv_w=conv_w,
                   w_branch_a=w_branch_a, w_branch_b=w_branch_b, w_out=w_out, ln2_g=ln2_g, w_ffn_gate=w_ffn_gate,
                   w_ffn_up=w_ffn_up, w_ffn_down=w_ffn_down)
    m_in = dict(ada_w=m_ada_w, ada_b=m_ada_b, ln1_g=m_ln1_g, w_in=m_w_in, q_norm_g=m_q_norm_g, k_norm_g=m_k_norm_g,
                conv_w=m_conv_w, w_branch_a=m_w_branch_a, w_branch_b=m_w_branch_b, w_out=m_w_out, ln2_g=m_ln2_g,
                w_ffn_gate=m_w_ffn_gate, w_ffn_up=m_w_ffn_up, w_ffn_down=m_w_ffn_down)
    v_in = dict(ada_w=v_ada_w, ada_b=v_ada_b, ln1_g=v_ln1_g, w_in=v_w_in, q_norm_g=v_q_norm_g, k_norm_g=v_k_norm_g,
                conv_w=v_conv_w, w_branch_a=v_w_branch_a, w_branch_b=v_w_branch_b, w_out=v_w_out, ln2_g=v_ln2_g,
                w_ffn_gate=v_w_ffn_gate, w_ffn_up=v_w_ffn_up, w_ffn_down=v_w_ffn_down)
    names = list(weights)

    mx, my, mc = lax.axis_index("x"), lax.axis_index("y"), lax.axis_index("c")
    me = 4 * mx + 2 * my + mc
    xs, target = x[0], loss_target[0]
    t, d = xs.shape
    depth = ada_w.shape[0]
    mod_cols = ada_w.shape[2]
    conv_cols = conv_w.shape[2]
    row_w = 1024
    tile = 512 if t % 512 == 0 else t

    small = jnp.concatenate([c.reshape(-1), conv_w.reshape(-1)])
    small_n = -(-small.shape[0] // row_w) * row_w
    small = jnp.pad(small, (0, small_n - small.shape[0])).reshape(-1, row_w)
    small_all = _all_gather(small, "gather_cond").reshape(N_DEV, -1)
    c_all = small_all[:, :d]
    conv_full = jnp.transpose(small_all[:, d:d + depth * 3 * conv_cols].reshape(N_DEV, depth, 3, conv_cols), (1, 2, 0, 3)
                              ).reshape(depth, 3, N_DEV * conv_cols)

    ada_b_cols = lax.dynamic_slice_in_dim(ada_b, me * mod_cols, mod_cols, axis=1)
    mod_part = _ada_forward(c_all, ada_w, ada_b_cols)
    mod_all = _all_gather(mod_part.reshape(-1, row_w), "gather_mod").reshape(N_DEV, depth, N_DEV, mod_cols)
    mod_mine = lax.dynamic_index_in_dim(mod_all, me, axis=2, keepdims=False)
    mod = jnp.transpose(mod_mine, (1, 0, 2)).reshape(depth, 6, 1, d)

    flat = jnp.concatenate([weights[n][l].reshape(-1).astype(BF16) for l in range(depth) for n in _BIG])
    w_all = _all_gather(flat.reshape(-1, row_w), "gather_weights").reshape(N_DEV, -1)
    layer_w, off = [], 0
    short = dict(w_in="w_in", w_branch_a="w_a", w_branch_b="w_b", w_out="w_o", w_ffn_gate="w_g", w_ffn_up="w_u", w_ffn_down="w_d")
    for l in range(depth):
        wl = {}
        for n in _BIG:
            shard = weights[n].shape[1:]
            size = shard[0] * shard[1]
            wl[short[n]] = _unpack_weight(w_all[:, off:off + size], shard, n in _COL_SHARDED)
            off += size
        wl["ln1"], wl["ln2"] = ln1_g[l][None], ln2_g[l][None]
        wl["qg"] = jnp.tile(q_norm_g[l], d // HEAD_DIM)[None]
        wl["kg"] = jnp.tile(k_norm_g[l], d // HEAD_DIM)[None]
        wl["conv"] = conv_full[l]
        layer_w.append(wl)

    saved = []
    x_cur, y_prev, gate_prev = xs, None, None
    for l in range(depth):
        mods = [mod[l, k] for k in range(6)]
        s = _layer_forward(x_cur, y_prev, gate_prev, mods, layer_w[l], l, tile)
        saved.append(s)
        x_cur, y_prev, gate_prev = s["x2"], s["f"], mods[5]

    def loss_head(pid, tv, hv, bv):
        diff = tv[0] + bv[0] * tv[1] - tv[2]
        return [diff * (1.0 / d)], [_colsum(diff * diff) * (0.5 / d)]

    dx, loss_cols = _rowwise(loss_head, "loss_head", rows=t, tile=tile, tiled=[(x_cur, 0, d), (y_prev, 0, d), (target, 0, d)],
                             bcast=[gate_prev], outs=[(d, F32)], accs=[(1, d)])

    layer_g, dmods = [None] * depth, [None] * depth
    for l in reversed(range(depth)):
        mods = [mod[l, k] for k in range(6)]
        dx, layer_g[l], dmods[l] = _layer_backward(dx, saved[l], mods, layer_w[l], l, tile)
    grad_x = dx[None]

    pieces = [jnp.concatenate(dmods[l], axis=1) for l in range(depth)]
    for key in ("ln1", "ln2", "qg", "kg"):
        pieces += [layer_g[l][key] for l in range(depth)]
    pieces += [layer_g[l]["conv"].reshape(1, -1) for l in range(depth)]
    pieces.append(loss_cols)
    part_small = jnp.concatenate(pieces, axis=1).reshape(-1, row_w)
    part_all = _all_gather(part_small, "gather_small_grads")
    summed = _device_sum(part_all).reshape(-1)
    n_mod = depth * 6 * d
    dmod_all = part_all.reshape(N_DEV, -1)[:, :n_mod].reshape(N_DEV, depth, 6 * d)
    dmod_cols = jnp.transpose(lax.dynamic_slice_in_dim(dmod_all, me * mod_cols, mod_cols, axis=2), (1, 0, 2))
    g = {"ada_w": _ada_backward(c_all, dmod_cols), "ada_b": summed[:n_mod].reshape(depth, 6 * d)}
    off = n_mod
    g["ln1_g"] = summed[off:off + depth * d].reshape(depth, d)
    g["ln2_g"] = summed[off + depth * d:off + 2 * depth * d].reshape(depth, d)
    g["q_norm_g"] = summed[off + 2 * depth * d:off + 3 * depth * d].reshape(depth, d // HEAD_DIM, HEAD_DIM).sum(axis=1)
    g["k_norm_g"] = summed[off + 3 * depth * d:off + 4 * depth * d].reshape(depth, d // HEAD_DIM, HEAD_DIM).sum(axis=1)
    off += 4 * depth * d
    conv_g = summed[off:off + depth * 3 * d].reshape(depth, 3, N_DEV, conv_cols)
    g["conv_w"] = lax.dynamic_index_in_dim(conv_g, me, axis=2, keepdims=False)
    off += depth * 3 * d
    loss = jnp.sum(summed[off:off + d])

    rows8 = []
    for l in range(depth):
        gl = layer_g[l]
        rows8.append(jnp.stack([a.reshape(-1) for a in gl["w_in"]]))
        for n in _BIG[1:]:
            rows8.append(_pack_grad(gl[short[n]], n in _COL_SHARDED))
    g_abs = jnp.concatenate(rows8, axis=1)
    r_total = g_abs.shape[1] // row_w
    rs_tile = next(cand for cand in (616, 512, 256, 128, 64, 32, 16, 8) if r_total % cand == 0)
    reduced = _reduce_scatter(g_abs.reshape(N_DEV, r_total, row_w), mx, my, mc, rs_tile).reshape(-1)
    off = 0
    per_layer = {n: [] for n in _BIG}
    for l in range(depth):
        for n in _BIG:
            shard = weights[n].shape[1:]
            size = shard[0] * shard[1]
            per_layer[n].append(reduced[off:off + size].reshape(shard))
            off += size
    for n in _BIG:
        g[n] = jnp.stack(per_layer[n])

    deltas, new_m, new_v = {}, {}, {}
    for n in names:
        deltas[n], new_m[n], new_v[n] = _adamw(weights[n], g[n], m_in[n], v_in[n], f"adamw_{n}")
    return (loss, grad_x, *[g[n] for n in names], *[deltas[n] for n in names], *[new_m[n] for n in names],
            *[new_v[n] for n in names])
```

```python
import functools

import numpy as np
import jax
import jax.numpy as jnp
from jax import lax
from jax.experimental import pallas as pl
from jax.experimental.pallas import tpu as pltpu

F32, BF16 = jnp.float32, jnp.bfloat16
MESH = pl.DeviceIdType.MESH
N_DEV = 8
LANES = 128
HEAD_DIM = 64
HEAD_PAIR = 2 * HEAD_DIM
ATT_TILE = 256
EPS = 1e-6
VMEM_LIMIT = 56 * 1024 * 1024

ADAM_LR, ADAM_B1, ADAM_B2, ADAM_EPS, ADAM_WD, ADAM_STEP = 0.001, 0.9, 0.999, 1e-08, 0.01, 10

_NT = (((1,), (1,)), ((), ()))
_TN = (((0,), (0,)), ((), ()))
_NN = (((1,), (0,)), ((), ()))


def _params(n_grid):
    return pltpu.CompilerParams(dimension_semantics=("arbitrary",) * n_grid, vmem_limit_bytes=VMEM_LIMIT)


def _sigmoid(x):
    return 1.0 / (1.0 + jnp.exp(-x))


def _rowwise(fn, name, *, rows, tile, tiled=(), halos=(), bcast=(), outs=(), accs=()):
    n = rows // tile
    assert n * tile == rows
    in_specs, args = [], []
    for t in tiled:
        arr, cb, w = t[:3]
        rowmap = t[3] if len(t) > 3 else (lambda i: i)
        in_specs.append(pl.BlockSpec((tile, w), functools.partial(lambda i, cb, rowmap: (rowmap(i), cb), cb=cb, rowmap=rowmap)))
        args.append(arr)
    per_tile8 = tile // 8
    for arr, cb, w, side in halos:
        last8 = arr.shape[0] // 8 - 1
        if side == "prev":
            imap = functools.partial(lambda i, cb: (jnp.maximum(i * per_tile8 - 1, 0), cb), cb=cb)
        else:
            imap = functools.partial(lambda i, cb, last8: (jnp.minimum((i + 1) * per_tile8, last8), cb), cb=cb, last8=last8)
        in_specs.append(pl.BlockSpec((8, w), imap))
        args.append(arr)
    for arr in bcast:
        in_specs.append(pl.BlockSpec(arr.shape, functools.partial(lambda i, nd: (0,) * nd, nd=arr.ndim)))
        args.append(arr)
    out_shape = [jax.ShapeDtypeStruct((rows, w), dt) for w, dt in outs] + [jax.ShapeDtypeStruct(s, F32) for s in accs]
    out_specs = [pl.BlockSpec((tile, w), lambda i: (i, 0)) for w, _ in outs] + [pl.BlockSpec(s, lambda i: (0, 0)) for s in accs]
    nt, nh, nb, no, na = len(tiled), len(halos), len(bcast), len(outs), len(accs)

    def body(*refs):
        pid = pl.program_id(0)
        tv = [r[...] for r in refs[:nt]]
        hv = [r[...] for r in refs[nt:nt + nh]]
        bv = [r[...] for r in refs[nt + nh:nt + nh + nb]]
        out_refs = refs[nt + nh + nb:nt + nh + nb + no]
        acc_refs = refs[nt + nh + nb + no:]
        ov, av = fn(pid, tv, hv, bv)
        for r, v in zip(out_refs, ov):
            r[...] = v.astype(r.dtype)
        if na:
            @pl.when(pid == 0)
            def _():
                for r in acc_refs:
                    r[...] = jnp.zeros(r.shape, F32)
            for r, v in zip(acc_refs, av):
                r[...] += v

    res = pl.pallas_call(body, name=name, grid=(n,), in_specs=in_specs, out_specs=out_specs, out_shape=out_shape,
                         compiler_params=_params(1))(*args)
    return res


def _colsum(v):
    return jnp.sum(v, axis=0, keepdims=True)


def _matmul(pairs, mode, *, m, n, k, tm, tn, out_dtype, name):
    tm, tn = min(tm, m), min(tn, n)
    assert m % tm == 0 and n % tn == 0
    in_specs, args = [], []
    for a, acb, b, bcb in pairs:
        if mode == "tn":
            in_specs.append(pl.BlockSpec((k, tm), functools.partial(lambda i, j, o: (0, o + i), o=acb)))
            in_specs.append(pl.BlockSpec((k, tn), functools.partial(lambda i, j, o: (0, o + j), o=bcb)))
        elif mode == "nn":
            in_specs.append(pl.BlockSpec((tm, k), functools.partial(lambda i, j, o: (i, o), o=acb)))
            in_specs.append(pl.BlockSpec((k, tn), functools.partial(lambda i, j, o: (0, o + j), o=bcb)))
        else:
            in_specs.append(pl.BlockSpec((tm, k), functools.partial(lambda i, j, o: (i, o), o=acb)))
            in_specs.append(pl.BlockSpec((tn, k), functools.partial(lambda i, j, o: (j, o), o=bcb)))
        args += [a, b]
    dims = {"nn": _NN, "nt": _NT, "tn": _TN}[mode]
    npairs = len(pairs)

    def body(*refs):
        o_ref = refs[2 * npairs]
        acc = None
        for p in range(npairs):
            d = lax.dot_general(refs[2 * p][...].astype(BF16), refs[2 * p + 1][...].astype(BF16), dims,
                                preferred_element_type=F32)
            acc = d if acc is None else acc + d
        o_ref[...] = acc.astype(o_ref.dtype)

    return pl.pallas_call(body, name=name, grid=(m // tm, n // tn), in_specs=in_specs,
                          out_specs=pl.BlockSpec((tm, tn), lambda i, j: (i, j)),
                          out_shape=jax.ShapeDtypeStruct((m, n), out_dtype), compiler_params=_params(2))(*args)


def _scan_matrix(kind, n):
    r = np.arange(n)
    tri = (r[:, None] > r[None, :]) if kind == "suffix" else (r[:, None] < r[None, :])
    half = np.concatenate([tri.astype(np.float32), np.ones((n, LANES), np.float32)], axis=1)
    return jnp.asarray(np.concatenate([half, half], axis=0), BF16)


def _head_sum_matrix():
    r = np.arange(LANES)
    bd = (r[:, None] // HEAD_DIM == r[None, :] // HEAD_DIM).astype(np.float32)
    return jnp.asarray(np.concatenate([bd, bd], axis=0), BF16)


def _split_cat(v):
    hi = v.astype(BF16)
    lo = (v - hi.astype(F32)).astype(BF16)
    return jnp.concatenate([hi, lo], axis=1)


def _head_sums(v, bd2):
    hi = v.astype(BF16)
    lo = (v - hi.astype(F32)).astype(BF16)
    parts = []
    for g in range(v.shape[1] // LANES):
        sl = slice(g * LANES, (g + 1) * LANES)
        parts.append(jnp.dot(jnp.concatenate([hi[:, sl], lo[:, sl]], axis=1), bd2, preferred_element_type=F32))
    return jnp.concatenate(parts, axis=1)


def _running_sums(v, scan, carry, reverse):
    nb = v.shape[1] // LANES
    outs = [None] * nb
    for b in (reversed(range(nb)) if reverse else range(nb)):
        cs = jnp.dot(_split_cat(v[:, b * LANES:(b + 1) * LANES]), scan, preferred_element_type=F32)
        outs[b] = carry + cs[:, :LANES]
        carry = carry + cs[:, LANES:]
    return jnp.concatenate(outs, axis=1), carry


def _sb_tile(qm, ktile, scan_suffix, carry_n, mask):
    z = lax.dot_general(qm, ktile, _NT, preferred_element_type=F32)
    sp = jnp.log(1.0 + jnp.exp(-jnp.abs(z)))
    log_not = -(jnp.maximum(z, 0.0) + sp)
    log_beta = z + log_not
    if mask is not None:
        log_not = jnp.where(mask, log_not, 0.0)
    tail, carry_n = _running_sums(log_not, scan_suffix, carry_n, True)
    a = jnp.exp(log_beta + tail)
    if mask is not None:
        a = jnp.where(mask, a, 0.0)
    return a, log_beta, carry_n


def _head_masks(tq):
    lane = lax.broadcasted_iota(jnp.int32, (tq, HEAD_PAIR), 1)
    return [(lane // HEAD_DIM) == hh for hh in range(2)]


def _diag_mask(tq):
    return lax.broadcasted_iota(jnp.int32, (tq, tq), 1) < lax.broadcasted_iota(jnp.int32, (tq, tq), 0)


def _attention_fwd(qn, kn, vb, name):
    t, d = qn.shape
    tq = min(ATT_TILE, t)
    nq, hp = t // tq, d // HEAD_PAIR
    scan_suffix = _scan_matrix("suffix", LANES)

    def body(q_ref, k_ref, v_ref, sc_ref, o_ref):
        qi = pl.program_id(1)
        q2 = q_ref[...].astype(F32)
        scan = sc_ref[...]
        heads, diag = _head_masks(tq), _diag_mask(tq)
        qms = [jnp.where(h, q2, 0.0).astype(BF16) for h in heads]

        def rows_of(j):
            return pl.ds(pl.multiple_of(j * tq, tq), tq)

        state = []
        for hh in range(2):
            a, _, carry = _sb_tile(qms[hh], k_ref[rows_of(qi), :], scan, jnp.zeros((tq, LANES), F32), diag)
            state += [carry, jnp.dot(a.astype(BF16), v_ref[rows_of(qi), :], preferred_element_type=F32)]

        def step(jj, c):
            j = qi - 1 - jj
            kt, vt = k_ref[rows_of(j), :], v_ref[rows_of(j), :]
            new = []
            for hh in range(2):
                a, _, carry = _sb_tile(qms[hh], kt, scan, c[2 * hh], None)
                new += [carry, c[2 * hh + 1] + jnp.dot(a.astype(BF16), vt, preferred_element_type=F32)]
            return tuple(new)

        state = lax.fori_loop(0, qi, step, tuple(state))
        o_ref[...] = jnp.where(heads[0], state[1], state[3]).astype(o_ref.dtype)

    return pl.pallas_call(
        body, name=name, grid=(hp, nq),
        in_specs=[pl.BlockSpec((tq, HEAD_PAIR), lambda h, i: (i, h)),
                  pl.BlockSpec((t, HEAD_PAIR), lambda h, i: (0, h)),
                  pl.BlockSpec((t, HEAD_PAIR), lambda h, i: (0, h)),
                  pl.BlockSpec(scan_suffix.shape, lambda h, i: (0, 0))],
        out_specs=pl.BlockSpec((tq, HEAD_PAIR), lambda h, i: (i, h)),
        out_shape=jax.ShapeDtypeStruct((t, d), BF16), compiler_params=_params(2))(qn, kn, vb, scan_suffix)


def _attention_bwd(qn, kn, vb, dob, name):
    t, d = qn.shape
    tq = min(ATT_TILE, t)
    nq, hp = t // tq, d // HEAD_PAIR
    scan_suffix, scan_prefix = _scan_matrix("suffix", LANES), _scan_matrix("prefix", LANES)

    def body(q_ref, k_ref, v_ref, do_ref, ss_ref, sp_ref, dq_ref, dk_ref, dv_ref, g_s, b_s):
        qi = pl.program_id(1)

        @pl.when(qi == 0)
        def _():
            dk_ref[...] = jnp.zeros(dk_ref.shape, F32)
            dv_ref[...] = jnp.zeros(dv_ref.shape, F32)

        q2, do2 = q_ref[...].astype(F32), do_ref[...].astype(F32)
        ssuf, spre = ss_ref[...], sp_ref[...]
        heads, diag = _head_masks(tq), _diag_mask(tq)
        qms = [jnp.where(h, q2, 0.0).astype(BF16) for h in heads]
        doms = [jnp.where(h, do2, 0.0).astype(BF16) for h in heads]

        def rows_of(j):
            return pl.ds(pl.multiple_of(j * tq, tq), tq)

        def pass_one(j, carries, mask):
            kt, vt = k_ref[rows_of(j), :], v_ref[rows_of(j), :]
            new, dv = [], None
            for hh in range(2):
                a, log_beta, carry = _sb_tile(qms[hh], kt, ssuf, carries[hh], mask)
                beta = jnp.exp(log_beta)
                if mask is not None:
                    beta = jnp.where(mask, beta, 0.0)
                d_a = lax.dot_general(doms[hh], vt, _NT, preferred_element_type=F32)
                g_s[hh, j] = a * d_a
                b_s[hh, j] = beta
                part = lax.dot_general(a.astype(BF16), doms[hh], _TN, preferred_element_type=F32)
                dv = part if dv is None else dv + part
                new.append(carry)
            dv_ref[rows_of(j), :] += dv
            return tuple(new)

        zero = jnp.zeros((tq, LANES), F32)
        carries = pass_one(qi, (zero, zero), diag)
        lax.fori_loop(0, qi, lambda jj, c: pass_one(qi - 1 - jj, c, None), carries)

        def pass_two(j, c):
            kt = k_ref[rows_of(j), :]
            new, dk = [], None
            for hh in range(2):
                prefix, dq = c[2 * hh], c[2 * hh + 1]
                g, beta = g_s[hh, j], b_s[hh, j]
                before, prefix = _running_sums(g, spre, prefix, False)
                dz = (g - beta * (g + before)).astype(BF16)
                part = lax.dot_general(dz, qms[hh], _TN, preferred_element_type=F32)
                dk = part if dk is None else dk + part
                new += [prefix, dq + jnp.dot(dz, kt, preferred_element_type=F32)]
            dk_ref[rows_of(j), :] += dk
            return tuple(new)

        zq = jnp.zeros((tq, HEAD_PAIR), F32)
        out = lax.fori_loop(0, qi + 1, pass_two, (zero, zq, zero, zq))
        dq_ref[...] = jnp.where(heads[0], out[1], out[3])

    blk = pl.BlockSpec((tq, HEAD_PAIR), lambda h, i: (i, h))
    col_spec = pl.BlockSpec((t, HEAD_PAIR), lambda h, i: (0, h))
    const = pl.BlockSpec(scan_suffix.shape, lambda h, i: (0, 0))
    full = jax.ShapeDtypeStruct((t, d), F32)
    return pl.pallas_call(
        body, name=name, grid=(hp, nq), in_specs=[blk, col_spec, col_spec, blk, const, const],
        out_specs=[blk, col_spec, col_spec], out_shape=[full, full, full],
        scratch_shapes=[pltpu.VMEM((2, nq, tq, tq), F32), pltpu.VMEM((2, nq, tq, tq), F32)],
        compiler_params=_params(2))(qn, kn, vb, dob, scan_suffix, scan_prefix)


def _chip_at(x, y, j):
    return (1 - x if j & 2 else x, 1 - y if j & 1 else y)


def _all_gather(shards, name):
    n = len(shards)

    def body(*refs):
        x_refs, out_refs = refs[:n], refs[n:2 * n]
        send_sems, recv_sems, local_sems = refs[2 * n:]
        x, y, c = lax.axis_index("x"), lax.axis_index("y"), lax.axis_index("c")
        me, sibling = (x, y, c), (x, y, 1 - c)
        chips = [_chip_at(x, y, j) for j in (1, 2, 3)]

        def rows(t, px, py, pc):
            return out_refs[t].at[4 * px + 2 * py + pc]

        def copy(t, k, block, to, src=None):
            return pltpu.make_async_remote_copy(src_ref=rows(t, *block) if src is None else src, dst_ref=rows(t, *block),
                                                send_sem=send_sems.at[t, k], recv_sem=recv_sems.at[t, k],
                                                device_id=to, device_id_type=MESH)

        mine = [pltpu.make_async_copy(x_refs[t], rows(t, *me), local_sems.at[t]) for t in range(n)]
        for cp in mine:
            cp.start()
        first = []
        for t in range(n):
            first.append(copy(t, 0, me, sibling, src=x_refs[t]))
            first += [copy(t, 1 + j, me, (*chip, c), src=x_refs[t]) for j, chip in enumerate(chips)]
        for cp in first:
            cp.start()
        passed = []
        for t in range(n):
            for j, chip in enumerate(chips):
                copy(t, 1 + j, (*chip, c), me).wait_recv()
                passed.append(copy(t, 4 + j, (*chip, c), sibling))
                passed[-1].start()
        for t in range(n):
            copy(t, 0, sibling, me).wait_recv()
            for j, chip in enumerate(chips):
                copy(t, 4 + j, (*chip, 1 - c), me).wait_recv()
        for cp in first + passed:
            cp.wait_send()
        for cp in mine:
            cp.wait()

    any_spec = pl.BlockSpec(memory_space=pl.ANY)
    return pl.pallas_call(
        body, name=name, out_shape=[jax.ShapeDtypeStruct((N_DEV,) + a.shape, a.dtype) for a in shards],
        in_specs=[any_spec] * n, out_specs=[any_spec] * n,
        scratch_shapes=[pltpu.SemaphoreType.DMA((n, 7)), pltpu.SemaphoreType.DMA((n, 7)), pltpu.SemaphoreType.DMA((n,))])(*shards)


def _exchange_sibling(tensors, name):
    n = len(tensors)

    def body(*refs):
        g_refs, recv_refs = refs[:n], refs[n:2 * n]
        send_sems, recv_sems = refs[2 * n:]
        x, y, c = lax.axis_index("x"), lax.axis_index("y"), lax.axis_index("c")
        copies = []
        for t in range(n):
            for j in range(4):
                cx, cy = _chip_at(x, y, j)
                copies.append(pltpu.make_async_remote_copy(
                    src_ref=g_refs[t].at[4 * cx + 2 * cy + (1 - c)], dst_ref=recv_refs[t].at[j], send_sem=send_sems.at[t, j],
                    recv_sem=recv_sems.at[t, j], device_id=(x, y, 1 - c), device_id_type=MESH))
        for cp in copies:
            cp.start()
        for cp in copies:
            cp.wait_recv()
        for cp in copies:
            cp.wait_send()

    any_spec = pl.BlockSpec(memory_space=pl.ANY)
    return pl.pallas_call(
        body, name=name, out_shape=[jax.ShapeDtypeStruct((4,) + a.shape[1:], a.dtype) for a in tensors],
        in_specs=[any_spec] * n, out_specs=[any_spec] * n,
        scratch_shapes=[pltpu.SemaphoreType.DMA((n, 4)), pltpu.SemaphoreType.DMA((n, 4))])(*tensors)


def _exchange_chips(parts, name):
    n = len(parts)

    def body(*refs):
        p_refs, recv_refs = refs[:n], refs[n:2 * n]
        send_sems, recv_sems = refs[2 * n:]
        x, y, c = lax.axis_index("x"), lax.axis_index("y"), lax.axis_index("c")
        copies = []
        for t in range(n):
            for j in (1, 2, 3):
                copies.append(pltpu.make_async_remote_copy(
                    src_ref=p_refs[t].at[j], dst_ref=recv_refs[t].at[j - 1], send_sem=send_sems.at[t, j - 1],
                    recv_sem=recv_sems.at[t, j - 1], device_id=(*_chip_at(x, y, j), c), device_id_type=MESH))
        for cp in copies:
            cp.start()
        for cp in copies:
            cp.wait_recv()
        for cp in copies:
            cp.wait_send()

    any_spec = pl.BlockSpec(memory_space=pl.ANY)
    return pl.pallas_call(
        body, name=name, out_shape=[jax.ShapeDtypeStruct((3,) + a.shape[1:], a.dtype) for a in parts],
        in_specs=[any_spec] * n, out_specs=[any_spec] * n,
        scratch_shapes=[pltpu.SemaphoreType.DMA((n, 3)), pltpu.SemaphoreType.DMA((n, 3))])(*parts)


def _add_sibling(g8, recv1, name):
    _, a, b = g8.shape
    ta = next(cand for cand in (512, 384, 256, 128) if a % cand == 0)

    def body(g_ref, r_ref, o_ref):
        o_ref[...] = (g_ref[...].astype(F32) + r_ref[...].astype(F32)).astype(o_ref.dtype)

    def own(j, i):
        x, y, c = lax.axis_index("x"), lax.axis_index("y"), lax.axis_index("c")
        return (4 * (x ^ (j >> 1)) + 2 * (y ^ (j & 1)) + c, i, 0)

    slot = pl.BlockSpec((None, ta, b), lambda j, i: (j, i, 0))
    return pl.pallas_call(body, name=name, grid=(4, a // ta), in_specs=[pl.BlockSpec((None, ta, b), own), slot],
                          out_specs=slot, out_shape=jax.ShapeDtypeStruct((4, a, b), g8.dtype),
                          compiler_params=_params(2))(g8, recv1)


def _ada_forward(c_all, ada_w, ada_b_cols):
    depth, d, cols = ada_w.shape

    def body(c_ref, w_ref, b_ref, o_ref):
        cv = c_ref[...]
        act = cv * _sigmoid(cv)
        o_ref[...] = jnp.dot(act, w_ref[...], preferred_element_type=F32, precision=lax.Precision.HIGHEST) + b_ref[...]

    return pl.pallas_call(
        body, name="ada_forward", grid=(depth,),
        in_specs=[pl.BlockSpec((N_DEV, d), lambda l: (0, 0)), pl.BlockSpec((None, d, cols), lambda l: (l, 0, 0)),
                  pl.BlockSpec((None, 1, cols), lambda l: (l, 0, 0))],
        out_specs=pl.BlockSpec((None, N_DEV, cols), lambda l: (l, 0, 0)),
        out_shape=jax.ShapeDtypeStruct((depth, N_DEV, cols), F32), compiler_params=_params(1))(
            c_all, ada_w, ada_b_cols.reshape(depth, 1, cols))


def _ada_backward(c_all, dmod_cols):
    depth, _, cols = dmod_cols.shape
    d = c_all.shape[1]

    def body(c_ref, g_ref, o_ref):
        cv = c_ref[...]
        act = cv * _sigmoid(cv)
        o_ref[...] = lax.dot_general(act, g_ref[...], _TN, preferred_element_type=F32, precision=lax.Precision.HIGHEST)

    return pl.pallas_call(
        body, name="ada_backward", grid=(depth,),
        in_specs=[pl.BlockSpec((N_DEV, d), lambda l: (0, 0)), pl.BlockSpec((None, N_DEV, cols), lambda l: (l, 0, 0))],
        out_specs=pl.BlockSpec((None, d, cols), lambda l: (l, 0, 0)),
        out_shape=jax.ShapeDtypeStruct((depth, d, cols), F32), compiler_params=_params(1))(c_all, dmod_cols)


def _device_sum(a):
    _, r, w = a.shape

    def body(a_ref, o_ref):
        acc = a_ref[0]
        for dev in range(1, N_DEV):
            acc = acc + a_ref[dev]
        o_ref[...] = acc

    return pl.pallas_call(body, name="device_sum", out_shape=jax.ShapeDtypeStruct((r, w), F32),
                          in_specs=[pl.BlockSpec(memory_space=pltpu.VMEM)], out_specs=pl.BlockSpec(memory_space=pltpu.VMEM))(a)


def _adamw(w, g_layers, m, v, name):
    shape = w.shape
    cols = shape[-1]
    rows = int(np.prod(shape[:-1]))
    layers = len(g_layers)
    per_layer = rows // layers
    tile = next((cand for cand in (512, 352, 256, 128) if per_layer % cand == 0), per_layer)
    n_l = per_layer // tile
    n_parts = len(g_layers[0])

    def fn(pid, tv, hv, bv):
        wv, mv, vv = tv[:3]
        gv = None
        for l in range(layers):
            gl = None
            for part in tv[3 + l * n_parts:3 + (l + 1) * n_parts]:
                gl = part.astype(F32) if gl is None else gl + part.astype(F32)
            gv = gl if gv is None else jnp.where(pid >= l * n_l, gl, gv)
        mn = ADAM_B1 * mv + (1.0 - ADAM_B1) * gv
        vn = ADAM_B2 * vv + (1.0 - ADAM_B2) * (gv * gv)
        m_hat = mn / (1.0 - ADAM_B1 ** ADAM_STEP)
        v_hat = vn / (1.0 - ADAM_B2 ** ADAM_STEP)
        delta = -ADAM_LR * (m_hat / (jnp.sqrt(v_hat) + ADAM_EPS) + ADAM_WD * wv)
        return [gv, delta, mn, vn], []

    tiled = [(a.reshape(rows, cols), 0, cols) for a in (w, m, v)]
    for l, parts in enumerate(g_layers):
        clamp = functools.partial(lambda i, l: jnp.clip(i - l * n_l, 0, n_l - 1), l=l)
        tiled += [(p.reshape(per_layer, cols), 0, cols, clamp) for p in parts]
    res = _rowwise(fn, name, rows=rows, tile=tile, tiled=tiled, outs=[(cols, F32)] * 4)
    return [r.reshape(shape) for r in res]


def _norm_modulate(x, y, gate, ln_g, scale, shift, name, tile):
    t, d = x.shape

    def fn(pid, tv, hv, bv):
        if y is None:
            xn = tv[0]
            g_ln, sc, sh = bv
        else:
            g_gate, g_ln, sc, sh = bv
            xn = tv[0] + g_gate * tv[1]
        r = lax.rsqrt(jnp.mean(xn * xn, axis=-1, keepdims=True) + EPS)
        h = (xn * r * g_ln) * (1.0 + sc) + sh
        return ([h] if y is None else [xn, h]), []

    if y is None:
        h, = _rowwise(fn, name, rows=t, tile=tile, tiled=[(x, 0, d)], bcast=[ln_g, scale, shift], outs=[(d, BF16)])
        return x, h
    xn, h = _rowwise(fn, name, rows=t, tile=tile, tiled=[(x, 0, d), (y, 0, d)], bcast=[gate, ln_g, scale, shift],
                     outs=[(d, F32), (d, BF16)])
    return xn, h


def _norm_backward(dh, x, dres, ln_g, scale, name, tile):
    t, d = x.shape

    def fn(pid, tv, hv, bv):
        dhv, xv, dr = tv
        g_ln, sc = bv
        r = lax.rsqrt(jnp.mean(xv * xv, axis=-1, keepdims=True) + EPS)
        xn = xv * r
        dxn = dhv * (1.0 + sc) * g_ln
        dx = dr + r * (dxn - xn * jnp.mean(dxn * xn, axis=-1, keepdims=True))
        return [dx], [_colsum(dhv), _colsum(dhv * (xn * g_ln)), _colsum(dhv * (1.0 + sc) * xn)]

    return _rowwise(fn, name, rows=t, tile=tile, tiled=[(dh, 0, d), (x, 0, d), (dres, 0, d)], bcast=[ln_g, scale],
                    outs=[(d, F32)], accs=[(1, d)] * 3)


def _shift_rows(u, halo, k, pid, first_tile_zero):
    rows = lax.broadcasted_iota(jnp.int32, u.shape, 0)
    halo = halo * jnp.where(pid == 0, 0.0, 1.0) if first_tile_zero else halo
    out = pltpu.roll(u, k, axis=0)
    for j in range(k):
        out = jnp.where(rows == j, halo[8 - k + j:8 - k + j + 1, :], out)
    return out


def _shift_rows_up(u, halo, k, pid, n_tiles):
    tile = u.shape[0]
    rows = lax.broadcasted_iota(jnp.int32, u.shape, 0)
    halo = halo * jnp.where(pid == n_tiles - 1, 0.0, 1.0)
    out = pltpu.roll(u, tile - k, axis=0)
    for j in range(k):
        out = jnp.where(rows == tile - k + j, halo[j:j + 1, :], out)
    return out


def _layer_forward(x_in, y_prev, gate_prev, mod, w, l, tile):
    sh1, sc1, g1, sh2, sc2, g2 = mod
    t, d = x_in.shape
    f = w["w_g"].shape[1]
    bd2 = _head_sum_matrix()
    s = {}
    s["x"], s["h"] = _norm_modulate(x_in, y_prev, gate_prev, w["ln1"], sc1, sh1, f"l{l}_norm1", tile)
    p = _matmul([(s["h"], 0, w["w_in"], 0)], "nn", m=t, n=8 * d, k=d, tm=1024, tn=512, out_dtype=F32, name=f"l{l}_in_proj")
    s["p"] = p

    def qk_norm(pid, tv, hv, bv):
        qr, kr, vr = tv
        qg, kg, bd = bv

        def nrm(xv, g):
            r = lax.rsqrt(_head_sums(xv * xv, bd) * (1.0 / HEAD_DIM) + EPS)
            return xv * r * g
        return [nrm(qr, qg) * 0.125, nrm(kr, kg), vr], []

    s["qn"], s["kn"], s["vb"] = _rowwise(qk_norm, f"l{l}_qk_norm", rows=t, tile=tile, tiled=[(p, 0, d), (p, 1, d), (p, 2, d)],
                                         bcast=[w["qg"], w["kg"], bd2], outs=[(d, BF16)] * 3)
    s["ya"] = _attention_fwd(s["qn"], s["kn"], s["vb"], f"l{l}_attention")

    def conv_fwd(pid, tv, hv, bv):
        cb, cc, cx = tv
        u = cc * cx
        hu = hv[0] * hv[1]
        cw = bv[0]
        conv = cw[0:1, :] * _shift_rows(u, hu, 2, pid, True) + cw[1:2, :] * _shift_rows(u, hu, 1, pid, True) + cw[2:3, :] * u
        return [cb * conv], []

    s["yb"], = _rowwise(conv_fwd, f"l{l}_conv", rows=t, tile=tile, tiled=[(p, 3, d), (p, 4, d), (p, 5, d)],
                        halos=[(p, 4, d, "prev"), (p, 5, d, "prev")], bcast=[w["conv"]], outs=[(d, BF16)])
    s["a"] = _matmul([(s["ya"], 0, w["w_a"], 0)], "nn", m=t, n=d, k=d, tm=1024, tn=512, out_dtype=F32, name=f"l{l}_branch_a")
    s["b"] = _matmul([(s["yb"], 0, w["w_b"], 0)], "nn", m=t, n=d, k=d, tm=1024, tn=512, out_dtype=F32, name=f"l{l}_branch_b")

    def merge(pid, tv, hv, bv):
        av, bvv, ga, gb = tv
        return [_sigmoid(ga) * av + _sigmoid(gb) * bvv], []

    s["merged"], = _rowwise(merge, f"l{l}_merge", rows=t, tile=tile, tiled=[(s["a"], 0, d), (s["b"], 0, d), (p, 6, d), (p, 7, d)],
                            outs=[(d, BF16)])
    s["mo"] = _matmul([(s["merged"], 0, w["w_o"], 0)], "nn", m=t, n=d, k=d, tm=1024, tn=512, out_dtype=F32, name=f"l{l}_out_proj")
    s["x2"], s["h2"] = _norm_modulate(s["x"], s["mo"], g1, w["ln2"], sc2, sh2, f"l{l}_norm2", tile)
    fn_tile = f // 2
    s["g"] = _matmul([(s["h2"], 0, w["w_g"], 0)], "nn", m=t, n=f, k=d, tm=512, tn=fn_tile, out_dtype=F32, name=f"l{l}_ffn_gate")
    s["u"] = _matmul([(s["h2"], 0, w["w_u"], 0)], "nn", m=t, n=f, k=d, tm=512, tn=fn_tile, out_dtype=F32, name=f"l{l}_ffn_up")

    def swiglu(pid, tv, hv, bv):
        gv, uv = tv
        return [gv * _sigmoid(gv) * uv], []

    s["s"], = _rowwise(swiglu, f"l{l}_swiglu", rows=t, tile=tile // 2, tiled=[(s["g"], 0, f), (s["u"], 0, f)], outs=[(f, BF16)])
    s["f"] = _matmul([(s["s"], 0, w["w_d"], 0)], "nn", m=t, n=d, k=f, tm=512, tn=512, out_dtype=F32, name=f"l{l}_ffn_down")
    return s


def _layer_backward(dx3, s, mod, w, l, tile):
    sh1, sc1, g1, sh2, sc2, g2 = mod
    t, d = dx3.shape
    f = w["w_g"].shape[1]
    p = s["p"]
    bd2 = _head_sum_matrix()
    n_tiles = t // tile
    grads = {}

    def gate_bwd(pid, tv, hv, bv):
        return [tv[0] * bv[0]], [_colsum(tv[0] * tv[1])]

    df, dg2 = _rowwise(gate_bwd, f"l{l}_bwd_gate2", rows=t, tile=tile, tiled=[(dx3, 0, d), (s["f"], 0, d)], bcast=[g2],
                       outs=[(d, BF16)], accs=[(1, d)])
    grads["w_d"] = _matmul([(s["s"], 0, df, 0)], "tn", m=f, n=d, k=t, tm=256, tn=512, out_dtype=BF16, name=f"l{l}_dw_down")
    ds = _matmul([(df, 0, w["w_d"], 0)], "nt", m=t, n=f, k=d, tm=512, tn=f // 2, out_dtype=F32, name=f"l{l}_d_swiglu")

    def swiglu_bwd(pid, tv, hv, bv):
        dsv, gv, uv = tv
        sig = _sigmoid(gv)
        return [dsv * uv * (sig * (1.0 + gv * (1.0 - sig))), dsv * (gv * sig)], []

    dgt, dup = _rowwise(swiglu_bwd, f"l{l}_bwd_swiglu", rows=t, tile=tile // 2, tiled=[(ds, 0, f), (s["g"], 0, f), (s["u"], 0, f)],
                        outs=[(f, BF16)] * 2)
    grads["w_g"] = _matmul([(s["h2"], 0, dgt, 0)], "tn", m=d, n=f, k=t, tm=256, tn=f // 2, out_dtype=BF16, name=f"l{l}_dw_gate")
    grads["w_u"] = _matmul([(s["h2"], 0, dup, 0)], "tn", m=d, n=f, k=t, tm=256, tn=f // 2, out_dtype=BF16, name=f"l{l}_dw_up")
    dh2 = _matmul([(dgt, 0, w["w_g"], 0), (dup, 0, w["w_u"], 0)], "nt", m=t, n=d, k=f, tm=512, tn=512, out_dtype=F32,
                  name=f"l{l}_dh2")
    dx2, dsh2, dsc2, grads["ln2"] = _norm_backward(dh2, s["x2"], dx3, w["ln2"], sc2, f"l{l}_bwd_norm2", tile)

    dmo, dg1 = _rowwise(gate_bwd, f"l{l}_bwd_gate1", rows=t, tile=tile, tiled=[(dx2, 0, d), (s["mo"], 0, d)], bcast=[g1],
                        outs=[(d, BF16)], accs=[(1, d)])
    grads["w_o"] = _matmul([(s["merged"], 0, dmo, 0)], "tn", m=d, n=d, k=t, tm=256, tn=512, out_dtype=BF16, name=f"l{l}_dw_out")
    dmerged = _matmul([(dmo, 0, w["w_o"], 0)], "nt", m=t, n=d, k=d, tm=1024, tn=512, out_dtype=F32, name=f"l{l}_d_merged")

    def merge_bwd(pid, tv, hv, bv):
        dm, av, bvv, ga, gb = tv
        sa, sb = _sigmoid(ga), _sigmoid(gb)
        return [dm * sa, dm * sb, dm * av * (sa * (1.0 - sa)), dm * bvv * (sb * (1.0 - sb))], []

    d_a, d_b, dga, dgb = _rowwise(merge_bwd, f"l{l}_bwd_merge", rows=t, tile=tile,
                                  tiled=[(dmerged, 0, d), (s["a"], 0, d), (s["b"], 0, d), (p, 6, d), (p, 7, d)], outs=[(d, BF16)] * 4)
    grads["w_a"] = _matmul([(s["ya"], 0, d_a, 0)], "tn", m=d, n=d, k=t, tm=256, tn=512, out_dtype=BF16, name=f"l{l}_dw_a")
    grads["w_b"] = _matmul([(s["yb"], 0, d_b, 0)], "tn", m=d, n=d, k=t, tm=256, tn=512, out_dtype=BF16, name=f"l{l}_dw_b")
    dya = _matmul([(d_a, 0, w["w_a"], 0)], "nt", m=t, n=d, k=d, tm=1024, tn=512, out_dtype=BF16, name=f"l{l}_d_ya")
    dyb = _matmul([(d_b, 0, w["w_b"], 0)], "nt", m=t, n=d, k=d, tm=1024, tn=512, out_dtype=F32, name=f"l{l}_d_yb")

    def conv_bwd(pid, tv, hv, bv):
        dy, cb, cc, cx = tv
        cw = bv[0]
        u, hu = cc * cx, hv[0] * hv[1]
        u1, u2 = _shift_rows(u, hu, 1, pid, True), _shift_rows(u, hu, 2, pid, True)
        conv = cw[0:1, :] * u2 + cw[1:2, :] * u1 + cw[2:3, :] * u
        dconv, hd = dy * cb, hv[2] * hv[3]
        du = (cw[2:3, :] * dconv + cw[1:2, :] * _shift_rows_up(dconv, hd, 1, pid, n_tiles)
              + cw[0:1, :] * _shift_rows_up(dconv, hd, 2, pid, n_tiles))
        return [dy * conv, du * cx, du * cc], [_colsum(dconv * u2), _colsum(dconv * u1), _colsum(dconv * u)]

    dcb, dcc, dcx, dcw0, dcw1, dcw2 = _rowwise(
        conv_bwd, f"l{l}_bwd_conv", rows=t, tile=tile, tiled=[(dyb, 0, d), (p, 3, d), (p, 4, d), (p, 5, d)],
        halos=[(p, 4, d, "prev"), (p, 5, d, "prev"), (dyb, 0, d, "next"), (p, 3, d, "next")], bcast=[w["conv"]],
        outs=[(d, BF16)] * 3, accs=[(1, d)] * 3)
    grads["conv"] = jnp.concatenate([dcw0, dcw1, dcw2], axis=0)

    dqs, dkn, dv = _attention_bwd(s["qn"], s["kn"], s["vb"], dya, f"l{l}_bwd_attention")

    def qk_norm_bwd(pid, tv, hv, bv):
        dq, dk, qr, kr, dvv = tv
        qg, kg, bd = bv

        def bwd(dy, xv, g):
            r = lax.rsqrt(_head_sums(xv * xv, bd) * (1.0 / HEAD_DIM) + EPS)
            yv = xv * r
            dyn = dy * g
            dx = r * (dyn - yv * (_head_sums(dyn * yv, bd) * (1.0 / HEAD_DIM)))
            return dx, _colsum(dy * yv)

        dxq, dgq = bwd(dq * 0.125, qr, qg)
        dxk, dgk = bwd(dk, kr, kg)
        return [dxq, dxk, dvv], [dgq, dgk]

    dqr, dkr, dvb, grads["qg"], grads["kg"] = _rowwise(
        qk_norm_bwd, f"l{l}_bwd_qk_norm", rows=t, tile=tile, tiled=[(dqs, 0, d), (dkn, 0, d), (p, 0, d), (p, 1, d), (dv, 0, d)],
        bcast=[w["qg"], w["kg"], bd2], outs=[(d, BF16)] * 3, accs=[(1, d)] * 2)

    dp = [dqr, dkr, dvb, dcb, dcc, dcx, dga, dgb]
    grads["w_in"] = [_matmul([(s["h"], 0, dpk, 0)], "tn", m=d, n=d, k=t, tm=256, tn=512, out_dtype=BF16, name=f"l{l}_dw_in{k}")
                     for k, dpk in enumerate(dp)]
    dh = _matmul([(dpk, 0, w["w_in"], k) for k, dpk in enumerate(dp)], "nt", m=t, n=d, k=d, tm=512, tn=512, out_dtype=F32,
                 name=f"l{l}_dh")
    dx, dsh1, dsc1, grads["ln1"] = _norm_backward(dh, s["x"], dx2, w["ln1"], sc1, f"l{l}_bwd_norm1", tile)
    return dx, grads, [dsh1, dsc1, dg1, dsh2, dsc2, dg2]


_BIG = ["w_in", "w_branch_a", "w_branch_b", "w_out", "w_ffn_gate", "w_ffn_up", "w_ffn_down"]
_SHORT = dict(w_in="w_in", w_branch_a="w_a", w_branch_b="w_b", w_out="w_o", w_ffn_gate="w_g", w_ffn_up="w_u", w_ffn_down="w_d")
_COL_SHARDED = {"w_in", "w_ffn_gate", "w_ffn_up"}


def kernel(x, c, ada_w, ada_b, ln1_g, w_in, q_norm_g, k_norm_g, conv_w, w_branch_a, w_branch_b, w_out, ln2_g, w_ffn_gate, w_ffn_up, w_ffn_down, loss_target, m_ada_w, m_ada_b, m_ln1_g, m_w_in, m_q_norm_g, m_k_norm_g, m_conv_w, m_w_branch_a, m_w_branch_b, m_w_out, m_ln2_g, m_w_ffn_gate, m_w_ffn_up, m_w_ffn_down, v_ada_w, v_ada_b, v_ln1_g, v_w_in, v_q_norm_g, v_k_norm_g, v_conv_w, v_w_branch_a, v_w_branch_b, v_w_out, v_ln2_g, v_w_ffn_gate, v_w_ffn_up, v_w_ffn_down):
    weights = dict(ada_w=ada_w, ada_b=ada_b, ln1_g=ln1_g, w_in=w_in, q_norm_g=q_norm_g, k_norm_g=k_norm_g, conv_w=conv_w,
                   w_branch_a=w_branch_a, w_branch_b=w_branch_b, w_out=w_out, ln2_g=ln2_g, w_ffn_gate=w_ffn_gate,
                   w_ffn_up=w_ffn_up, w_ffn_down=w_ffn_down)
    m_in = dict(ada_w=m_ada_w, ada_b=m_ada_b, ln1_g=m_ln1_g, w_in=m_w_in, q_norm_g=m_q_norm_g, k_norm_g=m_k_norm_g,
                conv_w=m_conv_w, w_branch_a=m_w_branch_a, w_branch_b=m_w_branch_b, w_out=m_w_out, ln2_g=m_ln2_g,
                w_ffn_gate=m_w_ffn_gate, w_ffn_up=m_w_ffn_up, w_ffn_down=m_w_ffn_down)
    v_in = dict(ada_w=v_ada_w, ada_b=v_ada_b, ln1_g=v_ln1_g, w_in=v_w_in, q_norm_g=v_q_norm_g, k_norm_g=v_k_norm_g,
                conv_w=v_conv_w, w_branch_a=v_w_branch_a, w_branch_b=v_w_branch_b, w_out=v_w_out, ln2_g=v_ln2_g,
                w_ffn_gate=v_w_ffn_gate, w_ffn_up=v_w_ffn_up, w_ffn_down=v_w_ffn_down)
    names = list(weights)

    mx, my, mc = lax.axis_index("x"), lax.axis_index("y"), lax.axis_index("c")
    me = 4 * mx + 2 * my + mc
    xs, target = x[0], loss_target[0]
    t, d = xs.shape
    depth = ada_w.shape[0]
    mod_cols = ada_w.shape[2]
    conv_cols = conv_w.shape[2]
    row_w = 1024
    tile = 512 if t % 512 == 0 else t

    small = jnp.concatenate([c.reshape(-1), conv_w.reshape(-1)])
    small_n = -(-small.shape[0] // row_w) * row_w
    small = jnp.pad(small, (0, small_n - small.shape[0])).reshape(-1, row_w)
    small_all = _all_gather([small], "gather_cond")[0].reshape(N_DEV, -1)
    c_all = small_all[:, :d]
    conv_full = jnp.transpose(small_all[:, d:d + depth * 3 * conv_cols].reshape(N_DEV, depth, 3, conv_cols), (1, 2, 0, 3)
                              ).reshape(depth, 3, N_DEV * conv_cols)

    ada_b_cols = lax.dynamic_slice_in_dim(ada_b, me * mod_cols, mod_cols, axis=1)
    mod_part = _ada_forward(c_all, ada_w, ada_b_cols)
    mod_all = _all_gather([mod_part.reshape(-1, row_w)], "gather_mod")[0].reshape(N_DEV, depth, N_DEV, mod_cols)
    mod_mine = lax.dynamic_index_in_dim(mod_all, me, axis=2, keepdims=False)
    mod = jnp.transpose(mod_mine, (1, 0, 2)).reshape(depth, 6, 1, d)

    f_shard = w_ffn_gate.shape[2]
    f_pad = -(-f_shard // LANES) * LANES - f_shard
    pads = dict(w_ffn_gate=((0, 0), (0, 0), (0, f_pad)), w_ffn_up=((0, 0), (0, 0), (0, f_pad)), w_ffn_down=((0, 0), (0, f_pad), (0, 0)))
    shards = [weights[n].astype(BF16) for n in _BIG]
    shards = [jnp.pad(a, pads[n]) if n in pads else a for n, a in zip(_BIG, shards)]
    gathered = dict(zip(_BIG, _all_gather(shards, "gather_weights")))
    layer_w = []
    for l in range(depth):
        wl = {}
        for n in _BIG:
            al = gathered[n][:, l]
            wl[_SHORT[n]] = (jnp.transpose(al, (1, 0, 2)).reshape(al.shape[1], -1) if n in _COL_SHARDED
                             else al.reshape(-1, al.shape[2]))
        wl["ln1"], wl["ln2"] = ln1_g[l][None], ln2_g[l][None]
        wl["qg"] = jnp.tile(q_norm_g[l], d // HEAD_DIM)[None]
        wl["kg"] = jnp.tile(k_norm_g[l], d // HEAD_DIM)[None]
        wl["conv"] = conv_full[l]
        layer_w.append(wl)

    saved = []
    x_cur, y_prev, gate_prev = xs, None, None
    for l in range(depth):
        mods = [mod[l, k] for k in range(6)]
        s = _layer_forward(x_cur, y_prev, gate_prev, mods, layer_w[l], l, tile)
        saved.append(s)
        x_cur, y_prev, gate_prev = s["x2"], s["f"], mods[5]

    def loss_head(pid, tv, hv, bv):
        diff = tv[0] + bv[0] * tv[1] - tv[2]
        return [diff * (1.0 / d)], [_colsum(diff * diff) * (0.5 / d)]

    dx, loss_cols = _rowwise(loss_head, "loss_head", rows=t, tile=tile, tiled=[(x_cur, 0, d), (y_prev, 0, d), (target, 0, d)],
                             bcast=[gate_prev], outs=[(d, F32)], accs=[(1, d)])

    layer_g, dmods = [None] * depth, [None] * depth
    for l in reversed(range(depth)):
        mods = [mod[l, k] for k in range(6)]
        dx, layer_g[l], dmods[l] = _layer_backward(dx, saved[l], mods, layer_w[l], l, tile)
    grad_x = dx[None]

    pieces = [jnp.concatenate(dmods[l], axis=1) for l in range(depth)]
    for key in ("ln1", "ln2", "qg", "kg"):
        pieces += [layer_g[l][key] for l in range(depth)]
    pieces += [layer_g[l]["conv"].reshape(1, -1) for l in range(depth)]
    pieces.append(loss_cols)
    part_small = jnp.concatenate(pieces, axis=1).reshape(-1, row_w)
    part_all = _all_gather([part_small], "gather_small_grads")[0]
    summed = _device_sum(part_all).reshape(-1)
    n_mod = depth * 6 * d
    dmod_all = part_all.reshape(N_DEV, -1)[:, :n_mod].reshape(N_DEV, depth, 6 * d)
    dmod_cols = jnp.transpose(lax.dynamic_slice_in_dim(dmod_all, me * mod_cols, mod_cols, axis=2), (1, 0, 2))
    g = {"ada_w": _ada_backward(c_all, dmod_cols), "ada_b": summed[:n_mod].reshape(depth, 6 * d)}
    off = n_mod
    g["ln1_g"] = summed[off:off + depth * d].reshape(depth, d)
    g["ln2_g"] = summed[off + depth * d:off + 2 * depth * d].reshape(depth, d)
    g["q_norm_g"] = summed[off + 2 * depth * d:off + 3 * depth * d].reshape(depth, d // HEAD_DIM, HEAD_DIM).sum(axis=1)
    g["k_norm_g"] = summed[off + 3 * depth * d:off + 4 * depth * d].reshape(depth, d // HEAD_DIM, HEAD_DIM).sum(axis=1)
    off += 4 * depth * d
    conv_g = summed[off:off + depth * 3 * d].reshape(depth, 3, N_DEV, conv_cols)
    g["conv_w"] = lax.dynamic_index_in_dim(conv_g, me, axis=2, keepdims=False)
    off += depth * 3 * d
    loss = jnp.sum(summed[off:off + d])

    def by_owner(n, full):
        if n == "w_in":
            return jnp.stack(full)
        if n in _COL_SHARDED:
            return jnp.transpose(full.reshape(full.shape[0], N_DEV, -1), (1, 0, 2))
        return full.reshape(N_DEV, -1, full.shape[1])

    tensors = [by_owner(n, layer_g[l][_SHORT[n]]) for l in range(depth) for n in _BIG]
    recv1 = _exchange_sibling(tensors, "rs_exchange_sibling")
    parts = [_add_sibling(a, r, f"rs_add_sibling_{i}") for i, (a, r) in enumerate(zip(tensors, recv1))]
    recv2 = _exchange_chips(parts, "rs_exchange_chips")

    def shard_of(n, a):
        if n in ("w_ffn_gate", "w_ffn_up"):
            return a[:, :f_shard]
        return a[:f_shard] if n == "w_ffn_down" else a

    outs = {}
    for k, n in enumerate(_BIG):
        g_layers = [[shard_of(n, a) for a in (parts[l * len(_BIG) + k][0], *recv2[l * len(_BIG) + k])] for l in range(depth)]
        outs[n] = _adamw(weights[n], g_layers, m_in[n], v_in[n], f"adamw_{n}")
    for n in names:
        if n not in outs:
            outs[n] = _adamw(weights[n], [[g[n]]], m_in[n], v_in[n], f"adamw_{n}")
    return (loss, grad_x, *[outs[n][0] for n in names], *[outs[n][1] for n in names], *[outs[n][2] for n in names],
            *[outs[n][3] for n in names])
```

```python
import functools

import numpy as np
import jax
import jax.numpy as jnp
from jax import lax
from jax.experimental import pallas as pl
from jax.experimental.pallas import tpu as pltpu

F32, BF16 = jnp.float32, jnp.bfloat16
MESH = pl.DeviceIdType.MESH
N_DEV = 8
LANES = 128
HEAD_DIM = 64
HEAD_PAIR = 2 * HEAD_DIM
ATT_TILE = 256
ATT_UNROLL = 4
EPS = 1e-6
VMEM_LIMIT = 56 * 1024 * 1024

ADAM_LR, ADAM_B1, ADAM_B2, ADAM_EPS, ADAM_WD, ADAM_STEP = 0.001, 0.9, 0.999, 1e-08, 0.01, 10

_NT = (((1,), (1,)), ((), ()))
_TN = (((0,), (0,)), ((), ()))
_NN = (((1,), (0,)), ((), ()))


def _params(n_grid):
    return pltpu.CompilerParams(dimension_semantics=("arbitrary",) * n_grid, vmem_limit_bytes=VMEM_LIMIT)


def _sigmoid(x):
    return 1.0 / (1.0 + jnp.exp(-x))


def _rowwise(fn, name, *, rows, tile, tiled=(), halos=(), bcast=(), outs=(), accs=()):
    n = rows // tile
    assert n * tile == rows
    in_specs, args = [], []
    for t in tiled:
        arr, cb, w = t[:3]
        rowmap = t[3] if len(t) > 3 else (lambda i: i)
        in_specs.append(pl.BlockSpec((tile, w), functools.partial(lambda i, cb, rowmap: (rowmap(i), cb), cb=cb, rowmap=rowmap)))
        args.append(arr)
    per_tile8 = tile // 8
    for arr, cb, w, side in halos:
        last8 = arr.shape[0] // 8 - 1
        if side == "prev":
            imap = functools.partial(lambda i, cb: (jnp.maximum(i * per_tile8 - 1, 0), cb), cb=cb)
        else:
            imap = functools.partial(lambda i, cb, last8: (jnp.minimum((i + 1) * per_tile8, last8), cb), cb=cb, last8=last8)
        in_specs.append(pl.BlockSpec((8, w), imap))
        args.append(arr)
    for arr in bcast:
        in_specs.append(pl.BlockSpec(arr.shape, functools.partial(lambda i, nd: (0,) * nd, nd=arr.ndim)))
        args.append(arr)
    out_shape = [jax.ShapeDtypeStruct((rows, w), dt) for w, dt in outs] + [jax.ShapeDtypeStruct(s, F32) for s in accs]
    out_specs = [pl.BlockSpec((tile, w), lambda i: (i, 0)) for w, _ in outs] + [pl.BlockSpec(s, lambda i: (0, 0)) for s in accs]
    nt, nh, nb, no, na = len(tiled), len(halos), len(bcast), len(outs), len(accs)

    def body(*refs):
        pid = pl.program_id(0)
        tv = [r[...] for r in refs[:nt]]
        hv = [r[...] for r in refs[nt:nt + nh]]
        bv = [r[...] for r in refs[nt + nh:nt + nh + nb]]
        out_refs = refs[nt + nh + nb:nt + nh + nb + no]
        acc_refs = refs[nt + nh + nb + no:]
        ov, av = fn(pid, tv, hv, bv)
        for r, v in zip(out_refs, ov):
            r[...] = v.astype(r.dtype)
        if na:
            @pl.when(pid == 0)
            def _():
                for r in acc_refs:
                    r[...] = jnp.zeros(r.shape, F32)
            for r, v in zip(acc_refs, av):
                r[...] += v

    res = pl.pallas_call(body, name=name, grid=(n,), in_specs=in_specs, out_specs=out_specs, out_shape=out_shape,
                         compiler_params=_params(1))(*args)
    return res


def _colsum(v):
    return jnp.sum(v, axis=0, keepdims=True)


def _matmul(pairs, mode, *, m, n, k, tm, tn, out_dtype, name):
    tm, tn = min(tm, m), min(tn, n)
    assert m % tm == 0 and n % tn == 0
    in_specs, args = [], []
    for a, acb, b, bcb in pairs:
        if mode == "tn":
            in_specs.append(pl.BlockSpec((k, tm), functools.partial(lambda i, j, o: (0, o + i), o=acb)))
            in_specs.append(pl.BlockSpec((k, tn), functools.partial(lambda i, j, o: (0, o + j), o=bcb)))
        elif mode == "nn":
            in_specs.append(pl.BlockSpec((tm, k), functools.partial(lambda i, j, o: (i, o), o=acb)))
            in_specs.append(pl.BlockSpec((k, tn), functools.partial(lambda i, j, o: (0, o + j), o=bcb)))
        else:
            in_specs.append(pl.BlockSpec((tm, k), functools.partial(lambda i, j, o: (i, o), o=acb)))
            in_specs.append(pl.BlockSpec((tn, k), functools.partial(lambda i, j, o: (j, o), o=bcb)))
        args += [a, b]
    dims = {"nn": _NN, "nt": _NT, "tn": _TN}[mode]
    npairs = len(pairs)

    def body(*refs):
        o_ref = refs[2 * npairs]
        acc = None
        for p in range(npairs):
            d = lax.dot_general(refs[2 * p][...].astype(BF16), refs[2 * p + 1][...].astype(BF16), dims,
                                preferred_element_type=F32)
            acc = d if acc is None else acc + d
        o_ref[...] = acc.astype(o_ref.dtype)

    return pl.pallas_call(body, name=name, grid=(m // tm, n // tn), in_specs=in_specs,
                          out_specs=pl.BlockSpec((tm, tn), lambda i, j: (i, j)),
                          out_shape=jax.ShapeDtypeStruct((m, n), out_dtype), compiler_params=_params(2))(*args)


def _scan_matrix(kind, n):
    r = np.arange(n)
    tri = (r[:, None] > r[None, :]) if kind == "suffix" else (r[:, None] < r[None, :])
    half = np.concatenate([tri.astype(np.float32), np.ones((n, LANES), np.float32)], axis=1)
    return jnp.asarray(np.concatenate([half, half], axis=0), BF16)


def _head_sum_matrix():
    r = np.arange(LANES)
    bd = (r[:, None] // HEAD_DIM == r[None, :] // HEAD_DIM).astype(np.float32)
    return jnp.asarray(np.concatenate([bd, bd], axis=0), BF16)


def _split_cat(v):
    hi = v.astype(BF16)
    lo = (v - hi.astype(F32)).astype(BF16)
    return jnp.concatenate([hi, lo], axis=1)


def _head_sums(v, bd2):
    hi = v.astype(BF16)
    lo = (v - hi.astype(F32)).astype(BF16)
    parts = []
    for g in range(v.shape[1] // LANES):
        sl = slice(g * LANES, (g + 1) * LANES)
        parts.append(jnp.dot(jnp.concatenate([hi[:, sl], lo[:, sl]], axis=1), bd2, preferred_element_type=F32))
    return jnp.concatenate(parts, axis=1)


def _running_sums(v, scan, carry, reverse):
    nb = v.shape[1] // LANES
    outs = [None] * nb
    for b in (reversed(range(nb)) if reverse else range(nb)):
        cs = jnp.dot(_split_cat(v[:, b * LANES:(b + 1) * LANES]), scan, preferred_element_type=F32)
        outs[b] = carry + cs[:, :LANES]
        carry = carry + cs[:, LANES:]
    return jnp.concatenate(outs, axis=1), carry


def _sb_tile(qm, ktile, scan_suffix, carry_n, mask):
    z = lax.dot_general(qm, ktile, _NT, preferred_element_type=F32)
    sp = jnp.log(1.0 + jnp.exp(-jnp.abs(z)))
    log_not = -(jnp.maximum(z, 0.0) + sp)
    log_beta = z + log_not
    if mask is not None:
        log_not = jnp.where(mask, log_not, 0.0)
    tail, carry_n = _running_sums(log_not, scan_suffix, carry_n, True)
    a = jnp.exp(log_beta + tail)
    if mask is not None:
        a = jnp.where(mask, a, 0.0)
    return a, log_beta, carry_n


def _head_masks(tq):
    lane = lax.broadcasted_iota(jnp.int32, (tq, HEAD_PAIR), 1)
    return [(lane // HEAD_DIM) == hh for hh in range(2)]


def _diag_mask(tq):
    return lax.broadcasted_iota(jnp.int32, (tq, tq), 1) < lax.broadcasted_iota(jnp.int32, (tq, tq), 0)


def _attention_fwd(qn, kn, vb, name):
    t, d = qn.shape
    tq = min(ATT_TILE, t)
    nq, hp = t // tq, d // HEAD_PAIR
    scan_suffix = _scan_matrix("suffix", LANES)

    def body(q_ref, k_ref, v_ref, sc_ref, o_ref):
        qi = pl.program_id(1)
        q2 = q_ref[...].astype(F32)
        scan = sc_ref[...]
        heads, diag = _head_masks(tq), _diag_mask(tq)
        qms = [jnp.where(h, q2, 0.0).astype(BF16) for h in heads]

        def rows_of(j):
            return pl.ds(pl.multiple_of(j * tq, tq), tq)

        state = []
        for hh in range(2):
            a, _, carry = _sb_tile(qms[hh], k_ref[rows_of(qi), :], scan, jnp.zeros((tq, LANES), F32), diag)
            state += [carry, jnp.dot(a.astype(BF16), v_ref[rows_of(qi), :], preferred_element_type=F32)]

        def step(js, c):
            c = list(c)
            for j in js:
                kt, vt = k_ref[rows_of(j), :], v_ref[rows_of(j), :]
                for hh in range(2):
                    a, _, c[2 * hh] = _sb_tile(qms[hh], kt, scan, c[2 * hh], None)
                    c[2 * hh + 1] = c[2 * hh + 1] + jnp.dot(a.astype(BF16), vt, preferred_element_type=F32)
            return tuple(c)

        rem = qi % ATT_UNROLL
        state = lax.fori_loop(0, rem, lambda p, c: step([qi - 1 - p], c), tuple(state))
        state = lax.fori_loop(0, qi // ATT_UNROLL,
                              lambda p, c: step([qi - 1 - rem - ATT_UNROLL * p - u for u in range(ATT_UNROLL)], c), state)
        o_ref[...] = jnp.where(heads[0], state[1], state[3]).astype(o_ref.dtype)

    return pl.pallas_call(
        body, name=name, grid=(hp, nq),
        in_specs=[pl.BlockSpec((tq, HEAD_PAIR), lambda h, i: (i, h)),
                  pl.BlockSpec((t, HEAD_PAIR), lambda h, i: (0, h)),
                  pl.BlockSpec((t, HEAD_PAIR), lambda h, i: (0, h)),
                  pl.BlockSpec(scan_suffix.shape, lambda h, i: (0, 0))],
        out_specs=pl.BlockSpec((tq, HEAD_PAIR), lambda h, i: (i, h)),
        out_shape=jax.ShapeDtypeStruct((t, d), BF16), compiler_params=_params(2))(qn, kn, vb, scan_suffix)


def _attention_bwd(qn, kn, vb, dob, name):
    t, d = qn.shape
    tq = min(ATT_TILE, t)
    nq, hp = t // tq, d // HEAD_PAIR
    scan_suffix, scan_prefix = _scan_matrix("suffix", LANES), _scan_matrix("prefix", LANES)

    def body(q_ref, k_ref, v_ref, do_ref, ss_ref, sp_ref, dq_ref, dk_ref, dv_ref, g_s, b_s):
        qi = pl.program_id(1)

        @pl.when(qi == 0)
        def _():
            dk_ref[...] = jnp.zeros(dk_ref.shape, F32)
            dv_ref[...] = jnp.zeros(dv_ref.shape, F32)

        q2, do2 = q_ref[...].astype(F32), do_ref[...].astype(F32)
        ssuf, spre = ss_ref[...], sp_ref[...]
        heads, diag = _head_masks(tq), _diag_mask(tq)
        qms = [jnp.where(h, q2, 0.0).astype(BF16) for h in heads]
        doms = [jnp.where(h, do2, 0.0).astype(BF16) for h in heads]

        def rows_of(j):
            return pl.ds(pl.multiple_of(j * tq, tq), tq)

        def pass_one(js, carries, mask):
            carries = list(carries)
            for j in js:
                kt, vt = k_ref[rows_of(j), :], v_ref[rows_of(j), :]
                dv = None
                for hh in range(2):
                    a, log_beta, carries[hh] = _sb_tile(qms[hh], kt, ssuf, carries[hh], mask)
                    beta = jnp.exp(log_beta)
                    if mask is not None:
                        beta = jnp.where(mask, beta, 0.0)
                    d_a = lax.dot_general(doms[hh], vt, _NT, preferred_element_type=F32)
                    g_s[hh, j] = a * d_a
                    b_s[hh, j] = beta
                    part = lax.dot_general(a.astype(BF16), doms[hh], _TN, preferred_element_type=F32)
                    dv = part if dv is None else dv + part
                dv_ref[rows_of(j), :] += dv
            return tuple(carries)

        zero = jnp.zeros((tq, LANES), F32)
        rem = qi % ATT_UNROLL
        carries = pass_one([qi], (zero, zero), diag)
        carries = lax.fori_loop(0, rem, lambda p, c: pass_one([qi - 1 - p], c, None), carries)
        lax.fori_loop(0, qi // ATT_UNROLL,
                      lambda p, c: pass_one([qi - 1 - rem - ATT_UNROLL * p - u for u in range(ATT_UNROLL)], c, None), carries)

        def pass_two(js, c):
            c = list(c)
            for j in js:
                kt = k_ref[rows_of(j), :]
                dk = None
                for hh in range(2):
                    g, beta = g_s[hh, j], b_s[hh, j]
                    before, c[2 * hh] = _running_sums(g, spre, c[2 * hh], False)
                    dz = (g - beta * (g + before)).astype(BF16)
                    part = lax.dot_general(dz, qms[hh], _TN, preferred_element_type=F32)
                    dk = part if dk is None else dk + part
                    c[2 * hh + 1] = c[2 * hh + 1] + jnp.dot(dz, kt, preferred_element_type=F32)
                dk_ref[rows_of(j), :] += dk
            return tuple(c)

        zq = jnp.zeros((tq, HEAD_PAIR), F32)
        first = (qi + 1) % ATT_UNROLL
        out = lax.fori_loop(0, first, lambda p, c: pass_two([p], c), (zero, zq, zero, zq))
        out = lax.fori_loop(0, (qi + 1) // ATT_UNROLL,
                            lambda p, c: pass_two([first + ATT_UNROLL * p + u for u in range(ATT_UNROLL)], c), out)
        dq_ref[...] = jnp.where(heads[0], out[1], out[3])

    blk = pl.BlockSpec((tq, HEAD_PAIR), lambda h, i: (i, h))
    col_spec = pl.BlockSpec((t, HEAD_PAIR), lambda h, i: (0, h))
    const = pl.BlockSpec(scan_suffix.shape, lambda h, i: (0, 0))
    full = jax.ShapeDtypeStruct((t, d), F32)
    return pl.pallas_call(
        body, name=name, grid=(hp, nq), in_specs=[blk, col_spec, col_spec, blk, const, const],
        out_specs=[blk, col_spec, col_spec], out_shape=[full, full, full],
        scratch_shapes=[pltpu.VMEM((2, nq, tq, tq), F32), pltpu.VMEM((2, nq, tq, tq), F32)],
        compiler_params=_params(2))(qn, kn, vb, dob, scan_suffix, scan_prefix)


def _chip_at(x, y, j):
    return (1 - x if j & 2 else x, 1 - y if j & 1 else y)


def _all_gather(shards, name):
    n = len(shards)

    def body(*refs):
        x_refs, out_refs = refs[:n], refs[n:2 * n]
        send_sems, recv_sems, local_sems = refs[2 * n:]
        x, y, c = lax.axis_index("x"), lax.axis_index("y"), lax.axis_index("c")
        me, sibling = (x, y, c), (x, y, 1 - c)
        chips = [_chip_at(x, y, j) for j in (1, 2, 3)]

        def rows(t, px, py, pc):
            return out_refs[t].at[4 * px + 2 * py + pc]

        def copy(t, k, block, to, src=None):
            return pltpu.make_async_remote_copy(src_ref=rows(t, *block) if src is None else src, dst_ref=rows(t, *block),
                                                send_sem=send_sems.at[t, k], recv_sem=recv_sems.at[t, k],
                                                device_id=to, device_id_type=MESH)

        mine = [pltpu.make_async_copy(x_refs[t], rows(t, *me), local_sems.at[t]) for t in range(n)]
        for cp in mine:
            cp.start()
        first = []
        for t in range(n):
            first.append(copy(t, 0, me, sibling, src=x_refs[t]))
            first += [copy(t, 1 + j, me, (*chip, c), src=x_refs[t]) for j, chip in enumerate(chips)]
        for cp in first:
            cp.start()
        passed = []
        for t in range(n):
            for j, chip in enumerate(chips):
                copy(t, 1 + j, (*chip, c), me).wait_recv()
                passed.append(copy(t, 4 + j, (*chip, c), sibling))
                passed[-1].start()
        for t in range(n):
            copy(t, 0, sibling, me).wait_recv()
            for j, chip in enumerate(chips):
                copy(t, 4 + j, (*chip, 1 - c), me).wait_recv()
        for cp in first + passed:
            cp.wait_send()
        for cp in mine:
            cp.wait()

    any_spec = pl.BlockSpec(memory_space=pl.ANY)
    return pl.pallas_call(
        body, name=name, out_shape=[jax.ShapeDtypeStruct((N_DEV,) + a.shape, a.dtype) for a in shards],
        in_specs=[any_spec] * n, out_specs=[any_spec] * n,
        scratch_shapes=[pltpu.SemaphoreType.DMA((n, 7)), pltpu.SemaphoreType.DMA((n, 7)), pltpu.SemaphoreType.DMA((n,))])(*shards)


def _exchange_sibling(tensors, name):
    n = len(tensors)

    def body(*refs):
        g_refs, recv_refs = refs[:n], refs[n:2 * n]
        send_sems, recv_sems = refs[2 * n:]
        x, y, c = lax.axis_index("x"), lax.axis_index("y"), lax.axis_index("c")
        copies = []
        for t in range(n):
            for j in range(4):
                cx, cy = _chip_at(x, y, j)
                copies.append(pltpu.make_async_remote_copy(
                    src_ref=g_refs[t].at[4 * cx + 2 * cy + (1 - c)], dst_ref=recv_refs[t].at[j], send_sem=send_sems.at[t, j],
                    recv_sem=recv_sems.at[t, j], device_id=(x, y, 1 - c), device_id_type=MESH))
        for cp in copies:
            cp.start()
        for cp in copies:
            cp.wait_recv()
        for cp in copies:
            cp.wait_send()

    any_spec = pl.BlockSpec(memory_space=pl.ANY)
    return pl.pallas_call(
        body, name=name, out_shape=[jax.ShapeDtypeStruct((4,) + a.shape[1:], a.dtype) for a in tensors],
        in_specs=[any_spec] * n, out_specs=[any_spec] * n,
        scratch_shapes=[pltpu.SemaphoreType.DMA((n, 4)), pltpu.SemaphoreType.DMA((n, 4))])(*tensors)


def _exchange_chips(parts, name):
    n = len(parts)

    def body(*refs):
        p_refs, recv_refs = refs[:n], refs[n:2 * n]
        send_sems, recv_sems = refs[2 * n:]
        x, y, c = lax.axis_index("x"), lax.axis_index("y"), lax.axis_index("c")
        copies = []
        for t in range(n):
            for j in (1, 2, 3):
                copies.append(pltpu.make_async_remote_copy(
                    src_ref=p_refs[t].at[j], dst_ref=recv_refs[t].at[j - 1], send_sem=send_sems.at[t, j - 1],
                    recv_sem=recv_sems.at[t, j - 1], device_id=(*_chip_at(x, y, j), c), device_id_type=MESH))
        for cp in copies:
            cp.start()
        for cp in copies:
            cp.wait_recv()
        for cp in copies:
            cp.wait_send()

    any_spec = pl.BlockSpec(memory_space=pl.ANY)
    return pl.pallas_call(
        body, name=name, out_shape=[jax.ShapeDtypeStruct((3,) + a.shape[1:], a.dtype) for a in parts],
        in_specs=[any_spec] * n, out_specs=[any_spec] * n,
        scratch_shapes=[pltpu.SemaphoreType.DMA((n, 3)), pltpu.SemaphoreType.DMA((n, 3))])(*parts)


def _add_sibling(g8, recv1, name):
    _, a, b = g8.shape
    ta = next(cand for cand in (512, 384, 256, 128) if a % cand == 0)

    def body(g_ref, r_ref, o_ref):
        o_ref[...] = (g_ref[...].astype(F32) + r_ref[...].astype(F32)).astype(o_ref.dtype)

    def own(j, i):
        x, y, c = lax.axis_index("x"), lax.axis_index("y"), lax.axis_index("c")
        return (4 * (x ^ (j >> 1)) + 2 * (y ^ (j & 1)) + c, i, 0)

    slot = pl.BlockSpec((None, ta, b), lambda j, i: (j, i, 0))
    return pl.pallas_call(body, name=name, grid=(4, a // ta), in_specs=[pl.BlockSpec((None, ta, b), own), slot],
                          out_specs=slot, out_shape=jax.ShapeDtypeStruct((4, a, b), g8.dtype),
                          compiler_params=_params(2))(g8, recv1)


def _ada_forward(c_all, ada_w, ada_b_cols):
    depth, d, cols = ada_w.shape

    def body(c_ref, w_ref, b_ref, o_ref):
        cv = c_ref[...]
        act = cv * _sigmoid(cv)
        o_ref[...] = jnp.dot(act, w_ref[...], preferred_element_type=F32, precision=lax.Precision.HIGHEST) + b_ref[...]

    return pl.pallas_call(
        body, name="ada_forward", grid=(depth,),
        in_specs=[pl.BlockSpec((N_DEV, d), lambda l: (0, 0)), pl.BlockSpec((None, d, cols), lambda l: (l, 0, 0)),
                  pl.BlockSpec((None, 1, cols), lambda l: (l, 0, 0))],
        out_specs=pl.BlockSpec((None, N_DEV, cols), lambda l: (l, 0, 0)),
        out_shape=jax.ShapeDtypeStruct((depth, N_DEV, cols), F32), compiler_params=_params(1))(
            c_all, ada_w, ada_b_cols.reshape(depth, 1, cols))


def _ada_backward(c_all, dmod_cols):
    depth, _, cols = dmod_cols.shape
    d = c_all.shape[1]

    def body(c_ref, g_ref, o_ref):
        cv = c_ref[...]
        act = cv * _sigmoid(cv)
        o_ref[...] = lax.dot_general(act, g_ref[...], _TN, preferred_element_type=F32, precision=lax.Precision.HIGHEST)

    return pl.pallas_call(
        body, name="ada_backward", grid=(depth,),
        in_specs=[pl.BlockSpec((N_DEV, d), lambda l: (0, 0)), pl.BlockSpec((None, N_DEV, cols), lambda l: (l, 0, 0))],
        out_specs=pl.BlockSpec((None, d, cols), lambda l: (l, 0, 0)),
        out_shape=jax.ShapeDtypeStruct((depth, d, cols), F32), compiler_params=_params(1))(c_all, dmod_cols)


def _device_sum(a):
    _, r, w = a.shape

    def body(a_ref, o_ref):
        acc = a_ref[0]
        for dev in range(1, N_DEV):
            acc = acc + a_ref[dev]
        o_ref[...] = acc

    return pl.pallas_call(body, name="device_sum", out_shape=jax.ShapeDtypeStruct((r, w), F32),
                          in_specs=[pl.BlockSpec(memory_space=pltpu.VMEM)], out_specs=pl.BlockSpec(memory_space=pltpu.VMEM))(a)


def _adamw(w, g_layers, m, v, name):
    shape = w.shape
    cols = shape[-1]
    rows = int(np.prod(shape[:-1]))
    layers = len(g_layers)
    per_layer = rows // layers
    tile = next((cand for cand in (512, 352, 256, 128) if per_layer % cand == 0), per_layer)
    n_l = per_layer // tile
    n_parts = len(g_layers[0])

    def fn(pid, tv, hv, bv):
        wv, mv, vv = tv[:3]
        gv = None
        for l in range(layers):
            gl = None
            for part in tv[3 + l * n_parts:3 + (l + 1) * n_parts]:
                gl = part.astype(F32) if gl is None else gl + part.astype(F32)
            gv = gl if gv is None else jnp.where(pid >= l * n_l, gl, gv)
        mn = ADAM_B1 * mv + (1.0 - ADAM_B1) * gv
        vn = ADAM_B2 * vv + (1.0 - ADAM_B2) * (gv * gv)
        m_hat = mn / (1.0 - ADAM_B1 ** ADAM_STEP)
        v_hat = vn / (1.0 - ADAM_B2 ** ADAM_STEP)
        delta = -ADAM_LR * (m_hat / (jnp.sqrt(v_hat) + ADAM_EPS) + ADAM_WD * wv)
        return [gv, delta, mn, vn], []

    tiled = [(a.reshape(rows, cols), 0, cols) for a in (w, m, v)]
    for l, parts in enumerate(g_layers):
        clamp = functools.partial(lambda i, l: jnp.clip(i - l * n_l, 0, n_l - 1), l=l)
        tiled += [(p.reshape(per_layer, cols), 0, cols, clamp) for p in parts]
    res = _rowwise(fn, name, rows=rows, tile=tile, tiled=tiled, outs=[(cols, F32)] * 4)
    return [r.reshape(shape) for r in res]


def _norm_modulate(x, y, gate, ln_g, scale, shift, name, tile):
    t, d = x.shape

    def fn(pid, tv, hv, bv):
        if y is None:
            xn = tv[0]
            g_ln, sc, sh = bv
        else:
            g_gate, g_ln, sc, sh = bv
            xn = tv[0] + g_gate * tv[1]
        r = lax.rsqrt(jnp.mean(xn * xn, axis=-1, keepdims=True) + EPS)
        h = (xn * r * g_ln) * (1.0 + sc) + sh
        return ([h] if y is None else [xn, h]), []

    if y is None:
        h, = _rowwise(fn, name, rows=t, tile=tile, tiled=[(x, 0, d)], bcast=[ln_g, scale, shift], outs=[(d, BF16)])
        return x, h
    xn, h = _rowwise(fn, name, rows=t, tile=tile, tiled=[(x, 0, d), (y, 0, d)], bcast=[gate, ln_g, scale, shift],
                     outs=[(d, F32), (d, BF16)])
    return xn, h


def _norm_backward(dh, x, dres, ln_g, scale, name, tile):
    t, d = x.shape

    def fn(pid, tv, hv, bv):
        dhv, xv, dr = tv
        g_ln, sc = bv
        r = lax.rsqrt(jnp.mean(xv * xv, axis=-1, keepdims=True) + EPS)
        xn = xv * r
        dxn = dhv * (1.0 + sc) * g_ln
        dx = dr + r * (dxn - xn * jnp.mean(dxn * xn, axis=-1, keepdims=True))
        return [dx], [_colsum(dhv), _colsum(dhv * (xn * g_ln)), _colsum(dhv * (1.0 + sc) * xn)]

    return _rowwise(fn, name, rows=t, tile=tile, tiled=[(dh, 0, d), (x, 0, d), (dres, 0, d)], bcast=[ln_g, scale],
                    outs=[(d, F32)], accs=[(1, d)] * 3)


def _shift_rows(u, halo, k, pid, first_tile_zero):
    rows = lax.broadcasted_iota(jnp.int32, u.shape, 0)
    halo = halo * jnp.where(pid == 0, 0.0, 1.0) if first_tile_zero else halo
    out = pltpu.roll(u, k, axis=0)
    for j in range(k):
        out = jnp.where(rows == j, halo[8 - k + j:8 - k + j + 1, :], out)
    return out


def _shift_rows_up(u, halo, k, pid, n_tiles):
    tile = u.shape[0]
    rows = lax.broadcasted_iota(jnp.int32, u.shape, 0)
    halo = halo * jnp.where(pid == n_tiles - 1, 0.0, 1.0)
    out = pltpu.roll(u, tile - k, axis=0)
    for j in range(k):
        out = jnp.where(rows == tile - k + j, halo[j:j + 1, :], out)
    return out


def _layer_forward(x_in, y_prev, gate_prev, mod, w, l, tile):
    sh1, sc1, g1, sh2, sc2, g2 = mod
    t, d = x_in.shape
    f = w["w_g"].shape[1]
    bd2 = _head_sum_matrix()
    s = {}
    s["x"], s["h"] = _norm_modulate(x_in, y_prev, gate_prev, w["ln1"], sc1, sh1, f"l{l}_norm1", tile)
    p = _matmul([(s["h"], 0, w["w_in"], 0)], "nn", m=t, n=8 * d, k=d, tm=1024, tn=512, out_dtype=F32, name=f"l{l}_in_proj")
    s["p"] = p

    def qk_norm(pid, tv, hv, bv):
        qr, kr, vr = tv
        qg, kg, bd = bv

        def nrm(xv, g):
            r = lax.rsqrt(_head_sums(xv * xv, bd) * (1.0 / HEAD_DIM) + EPS)
            return xv * r * g
        return [nrm(qr, qg) * 0.125, nrm(kr, kg), vr], []

    s["qn"], s["kn"], s["vb"] = _rowwise(qk_norm, f"l{l}_qk_norm", rows=t, tile=tile, tiled=[(p, 0, d), (p, 1, d), (p, 2, d)],
                                         bcast=[w["qg"], w["kg"], bd2], outs=[(d, BF16)] * 3)
    s["ya"] = _attention_fwd(s["qn"], s["kn"], s["vb"], f"l{l}_attention")

    def conv_fwd(pid, tv, hv, bv):
        cb, cc, cx = tv
        u = cc * cx
        hu = hv[0] * hv[1]
        cw = bv[0]
        conv = cw[0:1, :] * _shift_rows(u, hu, 2, pid, True) + cw[1:2, :] * _shift_rows(u, hu, 1, pid, True) + cw[2:3, :] * u
        return [cb * conv], []

    s["yb"], = _rowwise(conv_fwd, f"l{l}_conv", rows=t, tile=tile, tiled=[(p, 3, d), (p, 4, d), (p, 5, d)],
                        halos=[(p, 4, d, "prev"), (p, 5, d, "prev")], bcast=[w["conv"]], outs=[(d, BF16)])
    s["a"] = _matmul([(s["ya"], 0, w["w_a"], 0)], "nn", m=t, n=d, k=d, tm=1024, tn=512, out_dtype=F32, name=f"l{l}_branch_a")
    s["b"] = _matmul([(s["yb"], 0, w["w_b"], 0)], "nn", m=t, n=d, k=d, tm=1024, tn=512, out_dtype=F32, name=f"l{l}_branch_b")

    def merge(pid, tv, hv, bv):
        av, bvv, ga, gb = tv
        return [_sigmoid(ga) * av + _sigmoid(gb) * bvv], []

    s["merged"], = _rowwise(merge, f"l{l}_merge", rows=t, tile=tile, tiled=[(s["a"], 0, d), (s["b"], 0, d), (p, 6, d), (p, 7, d)],
                            outs=[(d, BF16)])
    s["mo"] = _matmul([(s["merged"], 0, w["w_o"], 0)], "nn", m=t, n=d, k=d, tm=1024, tn=512, out_dtype=F32, name=f"l{l}_out_proj")
    s["x2"], s["h2"] = _norm_modulate(s["x"], s["mo"], g1, w["ln2"], sc2, sh2, f"l{l}_norm2", tile)
    fn_tile = f // 2
    s["g"] = _matmul([(s["h2"], 0, w["w_g"], 0)], "nn", m=t, n=f, k=d, tm=512, tn=fn_tile, out_dtype=F32, name=f"l{l}_ffn_gate")
    s["u"] = _matmul([(s["h2"], 0, w["w_u"], 0)], "nn", m=t, n=f, k=d, tm=512, tn=fn_tile, out_dtype=F32, name=f"l{l}_ffn_up")

    def swiglu(pid, tv, hv, bv):
        gv, uv = tv
        return [gv * _sigmoid(gv) * uv], []

    s["s"], = _rowwise(swiglu, f"l{l}_swiglu", rows=t, tile=tile // 2, tiled=[(s["g"], 0, f), (s["u"], 0, f)], outs=[(f, BF16)])
    s["f"] = _matmul([(s["s"], 0, w["w_d"], 0)], "nn", m=t, n=d, k=f, tm=512, tn=512, out_dtype=F32, name=f"l{l}_ffn_down")
    return s


def _layer_backward(dx3, s, mod, w, l, tile):
    sh1, sc1, g1, sh2, sc2, g2 = mod
    t, d = dx3.shape
    f = w["w_g"].shape[1]
    p = s["p"]
    bd2 = _head_sum_matrix()
    n_tiles = t // tile
    grads = {}

    def gate_bwd(pid, tv, hv, bv):
        return [tv[0] * bv[0]], [_colsum(tv[0] * tv[1])]

    df, dg2 = _rowwise(gate_bwd, f"l{l}_bwd_gate2", rows=t, tile=tile, tiled=[(dx3, 0, d), (s["f"], 0, d)], bcast=[g2],
                       outs=[(d, BF16)], accs=[(1, d)])
    grads["w_d"] = _matmul([(s["s"], 0, df, 0)], "tn", m=f, n=d, k=t, tm=256, tn=512, out_dtype=BF16, name=f"l{l}_dw_down")
    ds = _matmul([(df, 0, w["w_d"], 0)], "nt", m=t, n=f, k=d, tm=512, tn=f // 2, out_dtype=F32, name=f"l{l}_d_swiglu")

    def swiglu_bwd(pid, tv, hv, bv):
        dsv, gv, uv = tv
        sig = _sigmoid(gv)
        return [dsv * uv * (sig * (1.0 + gv * (1.0 - sig))), dsv * (gv * sig)], []

    dgt, dup = _rowwise(swiglu_bwd, f"l{l}_bwd_swiglu", rows=t, tile=tile // 2, tiled=[(ds, 0, f), (s["g"], 0, f), (s["u"], 0, f)],
                        outs=[(f, BF16)] * 2)
    grads["w_g"] = _matmul([(s["h2"], 0, dgt, 0)], "tn", m=d, n=f, k=t, tm=256, tn=f // 2, out_dtype=BF16, name=f"l{l}_dw_gate")
    grads["w_u"] = _matmul([(s["h2"], 0, dup, 0)], "tn", m=d, n=f, k=t, tm=256, tn=f // 2, out_dtype=BF16, name=f"l{l}_dw_up")
    dh2 = _matmul([(dgt, 0, w["w_g"], 0), (dup, 0, w["w_u"], 0)], "nt", m=t, n=d, k=f, tm=512, tn=512, out_dtype=F32,
                  name=f"l{l}_dh2")
    dx2, dsh2, dsc2, grads["ln2"] = _norm_backward(dh2, s["x2"], dx3, w["ln2"], sc2, f"l{l}_bwd_norm2", tile)

    dmo, dg1 = _rowwise(gate_bwd, f"l{l}_bwd_gate1", rows=t, tile=tile, tiled=[(dx2, 0, d), (s["mo"], 0, d)], bcast=[g1],
                        outs=[(d, BF16)], accs=[(1, d)])
    grads["w_o"] = _matmul([(s["merged"], 0, dmo, 0)], "tn", m=d, n=d, k=t, tm=256, tn=512, out_dtype=BF16, name=f"l{l}_dw_out")
    dmerged = _matmul([(dmo, 0, w["w_o"], 0)], "nt", m=t, n=d, k=d, tm=1024, tn=512, out_dtype=F32, name=f"l{l}_d_merged")

    def merge_bwd(pid, tv, hv, bv):
        dm, av, bvv, ga, gb = tv
        sa, sb = _sigmoid(ga), _sigmoid(gb)
        return [dm * sa, dm * sb, dm * av * (sa * (1.0 - sa)), dm * bvv * (sb * (1.0 - sb))], []

    d_a, d_b, dga, dgb = _rowwise(merge_bwd, f"l{l}_bwd_merge", rows=t, tile=tile,
                                  tiled=[(dmerged, 0, d), (s["a"], 0, d), (s["b"], 0, d), (p, 6, d), (p, 7, d)], outs=[(d, BF16)] * 4)
    grads["w_a"] = _matmul([(s["ya"], 0, d_a, 0)], "tn", m=d, n=d, k=t, tm=256, tn=512, out_dtype=BF16, name=f"l{l}_dw_a")
    grads["w_b"] = _matmul([(s["yb"], 0, d_b, 0)], "tn", m=d, n=d, k=t, tm=256, tn=512, out_dtype=BF16, name=f"l{l}_dw_b")
    dya = _matmul([(d_a, 0, w["w_a"], 0)], "nt", m=t, n=d, k=d, tm=1024, tn=512, out_dtype=BF16, name=f"l{l}_d_ya")
    dyb = _matmul([(d_b, 0, w["w_b"], 0)], "nt", m=t, n=d, k=d, tm=1024, tn=512, out_dtype=F32, name=f"l{l}_d_yb")

    def conv_bwd(pid, tv, hv, bv):
        dy, cb, cc, cx = tv
        cw = bv[0]
        u, hu = cc * cx, hv[0] * hv[1]
        u1, u2 = _shift_rows(u, hu, 1, pid, True), _shift_rows(u, hu, 2, pid, True)
        conv = cw[0:1, :] * u2 + cw[1:2, :] * u1 + cw[2:3, :] * u
        dconv, hd = dy * cb, hv[2] * hv[3]
        du = (cw[2:3, :] * dconv + cw[1:2, :] * _shift_rows_up(dconv, hd, 1, pid, n_tiles)
              + cw[0:1, :] * _shift_rows_up(dconv, hd, 2, pid, n_tiles))
        return [dy * conv, du * cx, du * cc], [_colsum(dconv * u2), _colsum(dconv * u1), _colsum(dconv * u)]

    dcb, dcc, dcx, dcw0, dcw1, dcw2 = _rowwise(
        conv_bwd, f"l{l}_bwd_conv", rows=t, tile=tile, tiled=[(dyb, 0, d), (p, 3, d), (p, 4, d), (p, 5, d)],
        halos=[(p, 4, d, "prev"), (p, 5, d, "prev"), (dyb, 0, d, "next"), (p, 3, d, "next")], bcast=[w["conv"]],
        outs=[(d, BF16)] * 3, accs=[(1, d)] * 3)
    grads["conv"] = jnp.concatenate([dcw0, dcw1, dcw2], axis=0)

    dqs, dkn, dv = _attention_bwd(s["qn"], s["kn"], s["vb"], dya, f"l{l}_bwd_attention")

    def qk_norm_bwd(pid, tv, hv, bv):
        dq, dk, qr, kr, dvv = tv
        qg, kg, bd = bv

        def bwd(dy, xv, g):
            r = lax.rsqrt(_head_sums(xv * xv, bd) * (1.0 / HEAD_DIM) + EPS)
            yv = xv * r
            dyn = dy * g
            dx = r * (dyn - yv * (_head_sums(dyn * yv, bd) * (1.0 / HEAD_DIM)))
            return dx, _colsum(dy * yv)

        dxq, dgq = bwd(dq * 0.125, qr, qg)
        dxk, dgk = bwd(dk, kr, kg)
        return [dxq, dxk, dvv], [dgq, dgk]

    dqr, dkr, dvb, grads["qg"], grads["kg"] = _rowwise(
        qk_norm_bwd, f"l{l}_bwd_qk_norm", rows=t, tile=tile, tiled=[(dqs, 0, d), (dkn, 0, d), (p, 0, d), (p, 1, d), (dv, 0, d)],
        bcast=[w["qg"], w["kg"], bd2], outs=[(d, BF16)] * 3, accs=[(1, d)] * 2)

    dp = [dqr, dkr, dvb, dcb, dcc, dcx, dga, dgb]
    grads["w_in"] = [_matmul([(s["h"], 0, dpk, 0)], "tn", m=d, n=d, k=t, tm=256, tn=512, out_dtype=BF16, name=f"l{l}_dw_in{k}")
                     for k, dpk in enumerate(dp)]
    dh = _matmul([(dpk, 0, w["w_in"], k) for k, dpk in enumerate(dp)], "nt", m=t, n=d, k=d, tm=512, tn=512, out_dtype=F32,
                 name=f"l{l}_dh")
    dx, dsh1, dsc1, grads["ln1"] = _norm_backward(dh, s["x"], dx2, w["ln1"], sc1, f"l{l}_bwd_norm1", tile)
    return dx, grads, [dsh1, dsc1, dg1, dsh2, dsc2, dg2]


_BIG = ["w_in", "w_branch_a", "w_branch_b", "w_out", "w_ffn_gate", "w_ffn_up", "w_ffn_down"]
_SHORT = dict(w_in="w_in", w_branch_a="w_a", w_branch_b="w_b", w_out="w_o", w_ffn_gate="w_g", w_ffn_up="w_u", w_ffn_down="w_d")
_COL_SHARDED = {"w_in", "w_ffn_gate", "w_ffn_up"}


def kernel(x, c, ada_w, ada_b, ln1_g, w_in, q_norm_g, k_norm_g, conv_w, w_branch_a, w_branch_b, w_out, ln2_g, w_ffn_gate, w_ffn_up, w_ffn_down, loss_target, m_ada_w, m_ada_b, m_ln1_g, m_w_in, m_q_norm_g, m_k_norm_g, m_conv_w, m_w_branch_a, m_w_branch_b, m_w_out, m_ln2_g, m_w_ffn_gate, m_w_ffn_up, m_w_ffn_down, v_ada_w, v_ada_b, v_ln1_g, v_w_in, v_q_norm_g, v_k_norm_g, v_conv_w, v_w_branch_a, v_w_branch_b, v_w_out, v_ln2_g, v_w_ffn_gate, v_w_ffn_up, v_w_ffn_down):
    weights = dict(ada_w=ada_w, ada_b=ada_b, ln1_g=ln1_g, w_in=w_in, q_norm_g=q_norm_g, k_norm_g=k_norm_g, conv_w=conv_w,
                   w_branch_a=w_branch_a, w_branch_b=w_branch_b, w_out=w_out, ln2_g=ln2_g, w_ffn_gate=w_ffn_gate,
                   w_ffn_up=w_ffn_up, w_ffn_down=w_ffn_down)
    m_in = dict(ada_w=m_ada_w, ada_b=m_ada_b, ln1_g=m_ln1_g, w_in=m_w_in, q_norm_g=m_q_norm_g, k_norm_g=m_k_norm_g,
                conv_w=m_conv_w, w_branch_a=m_w_branch_a, w_branch_b=m_w_branch_b, w_out=m_w_out, ln2_g=m_ln2_g,
                w_ffn_gate=m_w_ffn_gate, w_ffn_up=m_w_ffn_up, w_ffn_down=m_w_ffn_down)
    v_in = dict(ada_w=v_ada_w, ada_b=v_ada_b, ln1_g=v_ln1_g, w_in=v_w_in, q_norm_g=v_q_norm_g, k_norm_g=v_k_norm_g,
                conv_w=v_conv_w, w_branch_a=v_w_branch_a, w_branch_b=v_w_branch_b, w_out=v_w_out, ln2_g=v_ln2_g,
                w_ffn_gate=v_w_ffn_gate, w_ffn_up=v_w_ffn_up, w_ffn_down=v_w_ffn_down)
    names = list(weights)

    mx, my, mc = lax.axis_index("x"), lax.axis_index("y"), lax.axis_index("c")
    me = 4 * mx + 2 * my + mc
    xs, target = x[0], loss_target[0]
    t, d = xs.shape
    depth = ada_w.shape[0]
    mod_cols = ada_w.shape[2]
    conv_cols = conv_w.shape[2]
    row_w = 1024
    tile = 512 if t % 512 == 0 else t

    small = jnp.concatenate([c.reshape(-1), conv_w.reshape(-1)])
    small_n = -(-small.shape[0] // row_w) * row_w
    small = jnp.pad(small, (0, small_n - small.shape[0])).reshape(-1, row_w)
    small_all = _all_gather([small], "gather_cond")[0].reshape(N_DEV, -1)
    c_all = small_all[:, :d]
    conv_full = jnp.transpose(small_all[:, d:d + depth * 3 * conv_cols].reshape(N_DEV, depth, 3, conv_cols), (1, 2, 0, 3)
                              ).reshape(depth, 3, N_DEV * conv_cols)

    ada_b_cols = lax.dynamic_slice_in_dim(ada_b, me * mod_cols, mod_cols, axis=1)
    mod_part = _ada_forward(c_all, ada_w, ada_b_cols)
    mod_all = _all_gather([mod_part.reshape(-1, row_w)], "gather_mod")[0].reshape(N_DEV, depth, N_DEV, mod_cols)
    mod_mine = lax.dynamic_index_in_dim(mod_all, me, axis=2, keepdims=False)
    mod = jnp.transpose(mod_mine, (1, 0, 2)).reshape(depth, 6, 1, d)

    f_shard = w_ffn_gate.shape[2]
    f_pad = -(-f_shard // LANES) * LANES - f_shard
    pads = dict(w_ffn_gate=((0, 0), (0, 0), (0, f_pad)), w_ffn_up=((0, 0), (0, 0), (0, f_pad)), w_ffn_down=((0, 0), (0, f_pad), (0, 0)))
    shards = [weights[n].astype(BF16) for n in _BIG]
    shards = [jnp.pad(a, pads[n]) if n in pads else a for n, a in zip(_BIG, shards)]
    gathered = dict(zip(_BIG, _all_gather(shards, "gather_weights")))
    layer_w = []
    for l in range(depth):
        wl = {}
        for n in _BIG:
            al = gathered[n][:, l]
            wl[_SHORT[n]] = (jnp.transpose(al, (1, 0, 2)).reshape(al.shape[1], -1) if n in _COL_SHARDED
                             else al.reshape(-1, al.shape[2]))
        wl["ln1"], wl["ln2"] = ln1_g[l][None], ln2_g[l][None]
        wl["qg"] = jnp.tile(q_norm_g[l], d // HEAD_DIM)[None]
        wl["kg"] = jnp.tile(k_norm_g[l], d // HEAD_DIM)[None]
        wl["conv"] = conv_full[l]
        layer_w.append(wl)

    saved = []
    x_cur, y_prev, gate_prev = xs, None, None
    for l in range(depth):
        mods = [mod[l, k] for k in range(6)]
        s = _layer_forward(x_cur, y_prev, gate_prev, mods, layer_w[l], l, tile)
        saved.append(s)
        x_cur, y_prev, gate_prev = s["x2"], s["f"], mods[5]

    def loss_head(pid, tv, hv, bv):
        diff = tv[0] + bv[0] * tv[1] - tv[2]
        return [diff * (1.0 / d)], [_colsum(diff * diff) * (0.5 / d)]

    dx, loss_cols = _rowwise(loss_head, "loss_head", rows=t, tile=tile, tiled=[(x_cur, 0, d), (y_prev, 0, d), (target, 0, d)],
                             bcast=[gate_prev], outs=[(d, F32)], accs=[(1, d)])

    layer_g, dmods = [None] * depth, [None] * depth
    for l in reversed(range(depth)):
        mods = [mod[l, k] for k in range(6)]
        dx, layer_g[l], dmods[l] = _layer_backward(dx, saved[l], mods, layer_w[l], l, tile)
    grad_x = dx[None]

    pieces = [jnp.concatenate(dmods[l], axis=1) for l in range(depth)]
    for key in ("ln1", "ln2", "qg", "kg"):
        pieces += [layer_g[l][key] for l in range(depth)]
    pieces += [layer_g[l]["conv"].reshape(1, -1) for l in range(depth)]
    pieces.append(loss_cols)
    part_small = jnp.concatenate(pieces, axis=1).reshape(-1, row_w)
    part_all = _all_gather([part_small], "gather_small_grads")[0]
    summed = _device_sum(part_all).reshape(-1)
    n_mod = depth * 6 * d
    dmod_all = part_all.reshape(N_DEV, -1)[:, :n_mod].reshape(N_DEV, depth, 6 * d)
    dmod_cols = jnp.transpose(lax.dynamic_slice_in_dim(dmod_all, me * mod_cols, mod_cols, axis=2), (1, 0, 2))
    g = {"ada_w": _ada_backward(c_all, dmod_cols), "ada_b": summed[:n_mod].reshape(depth, 6 * d)}
    off = n_mod
    g["ln1_g"] = summed[off:off + depth * d].reshape(depth, d)
    g["ln2_g"] = summed[off + depth * d:off + 2 * depth * d].reshape(depth, d)
    g["q_norm_g"] = summed[off + 2 * depth * d:off + 3 * depth * d].reshape(depth, d // HEAD_DIM, HEAD_DIM).sum(axis=1)
    g["k_norm_g"] = summed[off + 3 * depth * d:off + 4 * depth * d].reshape(depth, d // HEAD_DIM, HEAD_DIM).sum(axis=1)
    off += 4 * depth * d
    conv_g = summed[off:off + depth * 3 * d].reshape(depth, 3, N_DEV, conv_cols)
    g["conv_w"] = lax.dynamic_index_in_dim(conv_g, me, axis=2, keepdims=False)
    off += depth * 3 * d
    loss = jnp.sum(summed[off:off + d])

    def by_owner(n, full):
        if n == "w_in":
            return jnp.stack(full)
        if n in _COL_SHARDED:
            return jnp.transpose(full.reshape(full.shape[0], N_DEV, -1), (1, 0, 2))
        return full.reshape(N_DEV, -1, full.shape[1])

    tensors = [by_owner(n, layer_g[l][_SHORT[n]]) for l in range(depth) for n in _BIG]
    recv1 = _exchange_sibling(tensors, "rs_exchange_sibling")
    parts = [_add_sibling(a, r, f"rs_add_sibling_{i}") for i, (a, r) in enumerate(zip(tensors, recv1))]
    recv2 = _exchange_chips(parts, "rs_exchange_chips")

    def shard_of(n, a):
        if n in ("w_ffn_gate", "w_ffn_up"):
            return a[:, :f_shard]
        return a[:f_shard] if n == "w_ffn_down" else a

    outs = {}
    for k, n in enumerate(_BIG):
        g_layers = [[shard_of(n, a) for a in (parts[l * len(_BIG) + k][0], *recv2[l * len(_BIG) + k])] for l in range(depth)]
        outs[n] = _adamw(weights[n], g_layers, m_in[n], v_in[n], f"adamw_{n}")
    for n in names:
        if n not in outs:
            outs[n] = _adamw(weights[n], [[g[n]]], m_in[n], v_in[n], f"adamw_{n}")
    return (loss, grad_x, *[outs[n][0] for n in names], *[outs[n][1] for n in names], *[outs[n][2] for n in names],
            *[outs[n][3] for n in names])
```

```python
import functools

import numpy as np
import jax
import jax.numpy as jnp
from jax import lax
from jax.experimental import pallas as pl
from jax.experimental.pallas import tpu as pltpu

F32, BF16 = jnp.float32, jnp.bfloat16
MESH = pl.DeviceIdType.MESH
N_DEV = 8
LANES = 128
HEAD_DIM = 64
HEAD_PAIR = 2 * HEAD_DIM
ATT_TILE = 256
ATT_UNROLL = 4
EPS = 1e-6
VMEM_LIMIT = 56 * 1024 * 1024

ADAM_LR, ADAM_B1, ADAM_B2, ADAM_EPS, ADAM_WD, ADAM_STEP = 0.001, 0.9, 0.999, 1e-08, 0.01, 10

_NT = (((1,), (1,)), ((), ()))
_TN = (((0,), (0,)), ((), ()))
_NN = (((1,), (0,)), ((), ()))


def _params(n_grid):
    return pltpu.CompilerParams(dimension_semantics=("arbitrary",) * n_grid, vmem_limit_bytes=VMEM_LIMIT)


def _sigmoid(x):
    return 1.0 / (1.0 + jnp.exp(-x))


def _rowwise(fn, name, *, rows, tile, tiled=(), halos=(), bcast=(), outs=(), accs=()):
    n = rows // tile
    assert n * tile == rows
    in_specs, args = [], []
    for t in tiled:
        arr, cb, w = t[:3]
        rowmap = t[3] if len(t) > 3 else (lambda i: i)
        in_specs.append(pl.BlockSpec((tile, w), functools.partial(lambda i, cb, rowmap: (rowmap(i), cb), cb=cb, rowmap=rowmap)))
        args.append(arr)
    per_tile8 = tile // 8
    for arr, cb, w, side in halos:
        last8 = arr.shape[0] // 8 - 1
        if side == "prev":
            imap = functools.partial(lambda i, cb: (jnp.maximum(i * per_tile8 - 1, 0), cb), cb=cb)
        else:
            imap = functools.partial(lambda i, cb, last8: (jnp.minimum((i + 1) * per_tile8, last8), cb), cb=cb, last8=last8)
        in_specs.append(pl.BlockSpec((8, w), imap))
        args.append(arr)
    for arr in bcast:
        in_specs.append(pl.BlockSpec(arr.shape, functools.partial(lambda i, nd: (0,) * nd, nd=arr.ndim)))
        args.append(arr)
    out_shape = [jax.ShapeDtypeStruct((rows, w), dt) for w, dt in outs] + [jax.ShapeDtypeStruct(s, F32) for s in accs]
    out_specs = [pl.BlockSpec((tile, w), lambda i: (i, 0)) for w, _ in outs] + [pl.BlockSpec(s, lambda i: (0, 0)) for s in accs]
    nt, nh, nb, no, na = len(tiled), len(halos), len(bcast), len(outs), len(accs)

    def body(*refs):
        pid = pl.program_id(0)
        tv = [r[...] for r in refs[:nt]]
        hv = [r[...] for r in refs[nt:nt + nh]]
        bv = [r[...] for r in refs[nt + nh:nt + nh + nb]]
        out_refs = refs[nt + nh + nb:nt + nh + nb + no]
        acc_refs = refs[nt + nh + nb + no:]
        ov, av = fn(pid, tv, hv, bv)
        for r, v in zip(out_refs, ov):
            r[...] = v.astype(r.dtype)
        if na:
            @pl.when(pid == 0)
            def _():
                for r in acc_refs:
                    r[...] = jnp.zeros(r.shape, F32)
            for r, v in zip(acc_refs, av):
                r[...] += v

    res = pl.pallas_call(body, name=name, grid=(n,), in_specs=in_specs, out_specs=out_specs, out_shape=out_shape,
                         compiler_params=_params(1))(*args)
    return res


def _colsum(v):
    return jnp.sum(v, axis=0, keepdims=True)


def _matmul(pairs, mode, *, m, n, k, tm, tn, out_dtype, name):
    tm, tn = min(tm, m), min(tn, n)
    assert m % tm == 0 and n % tn == 0
    in_specs, args = [], []
    for a, acb, b, bcb in pairs:
        if mode == "tn":
            in_specs.append(pl.BlockSpec((k, tm), functools.partial(lambda i, j, o: (0, o + i), o=acb)))
            in_specs.append(pl.BlockSpec((k, tn), functools.partial(lambda i, j, o: (0, o + j), o=bcb)))
        elif mode == "nn":
            in_specs.append(pl.BlockSpec((tm, k), functools.partial(lambda i, j, o: (i, o), o=acb)))
            in_specs.append(pl.BlockSpec((k, tn), functools.partial(lambda i, j, o: (0, o + j), o=bcb)))
        else:
            in_specs.append(pl.BlockSpec((tm, k), functools.partial(lambda i, j, o: (i, o), o=acb)))
            in_specs.append(pl.BlockSpec((tn, k), functools.partial(lambda i, j, o: (j, o), o=bcb)))
        args += [a, b]
    dims = {"nn": _NN, "nt": _NT, "tn": _TN}[mode]
    npairs = len(pairs)

    def body(*refs):
        o_ref = refs[2 * npairs]
        acc = None
        for p in range(npairs):
            d = lax.dot_general(refs[2 * p][...].astype(BF16), refs[2 * p + 1][...].astype(BF16), dims,
                                preferred_element_type=F32)
            acc = d if acc is None else acc + d
        o_ref[...] = acc.astype(o_ref.dtype)

    return pl.pallas_call(body, name=name, grid=(m // tm, n // tn), in_specs=in_specs,
                          out_specs=pl.BlockSpec((tm, tn), lambda i, j: (i, j)),
                          out_shape=jax.ShapeDtypeStruct((m, n), out_dtype), compiler_params=_params(2))(*args)


def _scan_matrix(kind, n):
    r = np.arange(n)
    tri = (r[:, None] > r[None, :]) if kind == "suffix" else (r[:, None] < r[None, :])
    half = np.concatenate([tri.astype(np.float32), np.ones((n, LANES), np.float32)], axis=1)
    return jnp.asarray(np.concatenate([half, half], axis=0), BF16)


def _head_sum_matrix():
    r = np.arange(LANES)
    bd = (r[:, None] // HEAD_DIM == r[None, :] // HEAD_DIM).astype(np.float32)
    return jnp.asarray(np.concatenate([bd, bd], axis=0), BF16)


def _split_cat(v):
    hi = v.astype(BF16)
    lo = (v - hi.astype(F32)).astype(BF16)
    return jnp.concatenate([hi, lo], axis=1)


def _head_sums(v, bd2):
    hi = v.astype(BF16)
    lo = (v - hi.astype(F32)).astype(BF16)
    parts = []
    for g in range(v.shape[1] // LANES):
        sl = slice(g * LANES, (g + 1) * LANES)
        parts.append(jnp.dot(jnp.concatenate([hi[:, sl], lo[:, sl]], axis=1), bd2, preferred_element_type=F32))
    return jnp.concatenate(parts, axis=1)


def _scan_parts(v, scan):
    return [jnp.dot(_split_cat(v[:, b * LANES:(b + 1) * LANES]), scan, preferred_element_type=F32)
            for b in range(v.shape[1] // LANES)]


def _chain_sums(parts, carry, reverse):
    nb = len(parts)
    outs = [None] * nb
    for b in (reversed(range(nb)) if reverse else range(nb)):
        outs[b] = carry + parts[b][:, :LANES]
        carry = carry + parts[b][:, LANES:]
    return jnp.concatenate(outs, axis=1), carry


def _log_weights(z, mask):
    sp = jnp.log(1.0 + jnp.exp(-jnp.abs(z)))
    log_not = -(jnp.maximum(z, 0.0) + sp)
    log_beta = z + log_not
    return (log_not if mask is None else jnp.where(mask, log_not, 0.0)), log_beta


def _pipeline(n_chains, stages):
    for step in range(n_chains + len(stages) - 1):
        for s, stage in enumerate(stages):
            if 0 <= step - s < n_chains:
                stage(step - s)


def _head_masks(tq):
    lane = lax.broadcasted_iota(jnp.int32, (tq, HEAD_PAIR), 1)
    return [(lane // HEAD_DIM) == hh for hh in range(2)]


def _diag_mask(tq):
    return lax.broadcasted_iota(jnp.int32, (tq, tq), 1) < lax.broadcasted_iota(jnp.int32, (tq, tq), 0)


def _attention_fwd(qn, kn, vb, name):
    t, d = qn.shape
    tq = min(ATT_TILE, t)
    nq, hp = t // tq, d // HEAD_PAIR
    scan_suffix = _scan_matrix("suffix", LANES)

    def body(q_ref, k_ref, v_ref, sc_ref, o_ref):
        qi = pl.program_id(1)
        q2 = q_ref[...].astype(F32)
        scan = sc_ref[...]
        heads, diag = _head_masks(tq), _diag_mask(tq)
        qms = [jnp.where(h, q2, 0.0).astype(BF16) for h in heads]

        def rows_of(j):
            return pl.ds(pl.multiple_of(j * tq, tq), tq)

        def step(js, state, mask):
            carry, acc = [state[0], state[2]], [state[1], state[3]]
            kts, vts = [k_ref[rows_of(j), :] for j in js], [v_ref[rows_of(j), :] for j in js]
            z, lw, parts, a = {}, {}, {}, {}

            def s_scores(c):
                z[c] = lax.dot_general(qms[c % 2], kts[c // 2], _NT, preferred_element_type=F32)

            def s_logs(c):
                lw[c] = _log_weights(z.pop(c), mask)

            def s_scan(c):
                parts[c] = _scan_parts(lw[c][0], scan)

            def s_weights(c):
                tail, carry[c % 2] = _chain_sums(parts.pop(c), carry[c % 2], True)
                av = jnp.exp(lw.pop(c)[1] + tail)
                a[c] = (av if mask is None else jnp.where(mask, av, 0.0)).astype(BF16)

            def s_values(c):
                acc[c % 2] = acc[c % 2] + jnp.dot(a.pop(c), vts[c // 2], preferred_element_type=F32)

            _pipeline(2 * len(js), [s_scores, s_logs, s_scan, s_weights, s_values])
            return (carry[0], acc[0], carry[1], acc[1])

        zero, zq = jnp.zeros((tq, LANES), F32), jnp.zeros((tq, HEAD_PAIR), F32)
        state = step([qi], (zero, zq, zero, zq), diag)
        rem = qi % ATT_UNROLL
        state = lax.fori_loop(0, rem, lambda p, c: step([qi - 1 - p], c, None), state)
        state = lax.fori_loop(0, qi // ATT_UNROLL,
                              lambda p, c: step([qi - 1 - rem - ATT_UNROLL * p - u for u in range(ATT_UNROLL)], c, None), state)
        o_ref[...] = jnp.where(heads[0], state[1], state[3]).astype(o_ref.dtype)

    return pl.pallas_call(
        body, name=name, grid=(hp, nq),
        in_specs=[pl.BlockSpec((tq, HEAD_PAIR), lambda h, i: (i, h)),
                  pl.BlockSpec((t, HEAD_PAIR), lambda h, i: (0, h)),
                  pl.BlockSpec((t, HEAD_PAIR), lambda h, i: (0, h)),
                  pl.BlockSpec(scan_suffix.shape, lambda h, i: (0, 0))],
        out_specs=pl.BlockSpec((tq, HEAD_PAIR), lambda h, i: (i, h)),
        out_shape=jax.ShapeDtypeStruct((t, d), BF16), compiler_params=_params(2))(qn, kn, vb, scan_suffix)


def _attention_bwd(qn, kn, vb, dob, name):
    t, d = qn.shape
    tq = min(ATT_TILE, t)
    nq, hp = t // tq, d // HEAD_PAIR
    scan_suffix, scan_prefix = _scan_matrix("suffix", LANES), _scan_matrix("prefix", LANES)

    def body(q_ref, k_ref, v_ref, do_ref, ss_ref, sp_ref, dq_ref, dk_ref, dv_ref, g_s, b_s):
        qi = pl.program_id(1)

        @pl.when(qi == 0)
        def _():
            dk_ref[...] = jnp.zeros(dk_ref.shape, F32)
            dv_ref[...] = jnp.zeros(dv_ref.shape, F32)

        q2, do2 = q_ref[...].astype(F32), do_ref[...].astype(F32)
        ssuf, spre = ss_ref[...], sp_ref[...]
        heads, diag = _head_masks(tq), _diag_mask(tq)
        qms = [jnp.where(h, q2, 0.0).astype(BF16) for h in heads]
        doms = [jnp.where(h, do2, 0.0).astype(BF16) for h in heads]

        def rows_of(j):
            return pl.ds(pl.multiple_of(j * tq, tq), tq)

        def pass_one(js, carries, mask):
            carry = list(carries)
            kts, vts = [k_ref[rows_of(j), :] for j in js], [v_ref[rows_of(j), :] for j in js]
            z, d_a, lw, parts, a, dv = {}, {}, {}, {}, {}, {}

            def s_scores(c):
                z[c] = lax.dot_general(qms[c % 2], kts[c // 2], _NT, preferred_element_type=F32)
                d_a[c] = lax.dot_general(doms[c % 2], vts[c // 2], _NT, preferred_element_type=F32)

            def s_logs(c):
                lw[c] = _log_weights(z.pop(c), mask)

            def s_scan(c):
                parts[c] = _scan_parts(lw[c][0], ssuf)

            def s_weights(c):
                tail, carry[c % 2] = _chain_sums(parts.pop(c), carry[c % 2], True)
                log_beta = lw.pop(c)[1]
                av, beta = jnp.exp(log_beta + tail), jnp.exp(log_beta)
                if mask is not None:
                    av, beta = jnp.where(mask, av, 0.0), jnp.where(mask, beta, 0.0)
                g_s[c % 2, js[c // 2]] = av * d_a.pop(c)
                b_s[c % 2, js[c // 2]] = beta
                a[c] = av.astype(BF16)

            def s_values(c):
                dv[c] = lax.dot_general(a.pop(c), doms[c % 2], _TN, preferred_element_type=F32)
                if c % 2 == 1:
                    dv_ref[rows_of(js[c // 2]), :] += dv.pop(c - 1) + dv.pop(c)

            _pipeline(2 * len(js), [s_scores, s_logs, s_scan, s_weights, s_values])
            return tuple(carry)

        zero = jnp.zeros((tq, LANES), F32)
        rem = qi % ATT_UNROLL
        carries = pass_one([qi], (zero, zero), diag)
        carries = lax.fori_loop(0, rem, lambda p, c: pass_one([qi - 1 - p], c, None), carries)
        lax.fori_loop(0, qi // ATT_UNROLL,
                      lambda p, c: pass_one([qi - 1 - rem - ATT_UNROLL * p - u for u in range(ATT_UNROLL)], c, None), carries)

        def pass_two(js, state):
            prefix, dq = [state[0], state[2]], [state[1], state[3]]
            kts = [k_ref[rows_of(j), :] for j in js]
            gb, parts, dz, dk = {}, {}, {}, {}

            def s_scan(c):
                gb[c] = (g_s[c % 2, js[c // 2]], b_s[c % 2, js[c // 2]])
                parts[c] = _scan_parts(gb[c][0], spre)

            def s_dz(c):
                before, prefix[c % 2] = _chain_sums(parts.pop(c), prefix[c % 2], False)
                g, beta = gb.pop(c)
                dz[c] = (g - beta * (g + before)).astype(BF16)

            def s_grads(c):
                dzc = dz.pop(c)
                dq[c % 2] = dq[c % 2] + jnp.dot(dzc, kts[c // 2], preferred_element_type=F32)
                dk[c] = lax.dot_general(dzc, qms[c % 2], _TN, preferred_element_type=F32)
                if c % 2 == 1:
                    dk_ref[rows_of(js[c // 2]), :] += dk.pop(c - 1) + dk.pop(c)

            _pipeline(2 * len(js), [s_scan, s_dz, s_grads])
            return (prefix[0], dq[0], prefix[1], dq[1])

        zq = jnp.zeros((tq, HEAD_PAIR), F32)
        first = (qi + 1) % ATT_UNROLL
        out = lax.fori_loop(0, first, lambda p, c: pass_two([p], c), (zero, zq, zero, zq))
        out = lax.fori_loop(0, (qi + 1) // ATT_UNROLL,
                            lambda p, c: pass_two([first + ATT_UNROLL * p + u for u in range(ATT_UNROLL)], c), out)
        dq_ref[...] = jnp.where(heads[0], out[1], out[3])

    blk = pl.BlockSpec((tq, HEAD_PAIR), lambda h, i: (i, h))
    col_spec = pl.BlockSpec((t, HEAD_PAIR), lambda h, i: (0, h))
    const = pl.BlockSpec(scan_suffix.shape, lambda h, i: (0, 0))
    full = jax.ShapeDtypeStruct((t, d), F32)
    return pl.pallas_call(
        body, name=name, grid=(hp, nq), in_specs=[blk, col_spec, col_spec, blk, const, const],
        out_specs=[blk, col_spec, col_spec], out_shape=[full, full, full],
        scratch_shapes=[pltpu.VMEM((2, nq, tq, tq), F32), pltpu.VMEM((2, nq, tq, tq), F32)],
        compiler_params=_params(2))(qn, kn, vb, dob, scan_suffix, scan_prefix)


def _chip_at(x, y, j):
    return (1 - x if j & 2 else x, 1 - y if j & 1 else y)


def _all_gather(shards, name):
    n = len(shards)

    def body(*refs):
        x_refs, out_refs = refs[:n], refs[n:2 * n]
        send_sems, recv_sems, local_sems = refs[2 * n:]
        x, y, c = lax.axis_index("x"), lax.axis_index("y"), lax.axis_index("c")
        me, sibling = (x, y, c), (x, y, 1 - c)
        chips = [_chip_at(x, y, j) for j in (1, 2, 3)]

        def rows(t, px, py, pc):
            return out_refs[t].at[4 * px + 2 * py + pc]

        def copy(t, k, block, to, src=None):
            return pltpu.make_async_remote_copy(src_ref=rows(t, *block) if src is None else src, dst_ref=rows(t, *block),
                                                send_sem=send_sems.at[t, k], recv_sem=recv_sems.at[t, k],
                                                device_id=to, device_id_type=MESH)

        mine = [pltpu.make_async_copy(x_refs[t], rows(t, *me), local_sems.at[t]) for t in range(n)]
        for cp in mine:
            cp.start()
        first = []
        for t in range(n):
            first.append(copy(t, 0, me, sibling, src=x_refs[t]))
            first += [copy(t, 1 + j, me, (*chip, c), src=x_refs[t]) for j, chip in enumerate(chips)]
        for cp in first:
            cp.start()
        passed = []
        for t in range(n):
            for j, chip in enumerate(chips):
                copy(t, 1 + j, (*chip, c), me).wait_recv()
                passed.append(copy(t, 4 + j, (*chip, c), sibling))
                passed[-1].start()
        for t in range(n):
            copy(t, 0, sibling, me).wait_recv()
            for j, chip in enumerate(chips):
                copy(t, 4 + j, (*chip, 1 - c), me).wait_recv()
        for cp in first + passed:
            cp.wait_send()
        for cp in mine:
            cp.wait()

    any_spec = pl.BlockSpec(memory_space=pl.ANY)
    return pl.pallas_call(
        body, name=name, out_shape=[jax.ShapeDtypeStruct((N_DEV,) + a.shape, a.dtype) for a in shards],
        in_specs=[any_spec] * n, out_specs=[any_spec] * n,
        scratch_shapes=[pltpu.SemaphoreType.DMA((n, 7)), pltpu.SemaphoreType.DMA((n, 7)), pltpu.SemaphoreType.DMA((n,))])(*shards)


def _exchange_sibling(tensors, name):
    n = len(tensors)

    def body(*refs):
        g_refs, recv_refs = refs[:n], refs[n:2 * n]
        send_sems, recv_sems = refs[2 * n:]
        x, y, c = lax.axis_index("x"), lax.axis_index("y"), lax.axis_index("c")
        copies = []
        for t in range(n):
            for j in range(4):
                cx, cy = _chip_at(x, y, j)
                copies.append(pltpu.make_async_remote_copy(
                    src_ref=g_refs[t].at[4 * cx + 2 * cy + (1 - c)], dst_ref=recv_refs[t].at[j], send_sem=send_sems.at[t, j],
                    recv_sem=recv_sems.at[t, j], device_id=(x, y, 1 - c), device_id_type=MESH))
        for cp in copies:
            cp.start()
        for cp in copies:
            cp.wait_recv()
        for cp in copies:
            cp.wait_send()

    any_spec = pl.BlockSpec(memory_space=pl.ANY)
    return pl.pallas_call(
        body, name=name, out_shape=[jax.ShapeDtypeStruct((4,) + a.shape[1:], a.dtype) for a in tensors],
        in_specs=[any_spec] * n, out_specs=[any_spec] * n,
        scratch_shapes=[pltpu.SemaphoreType.DMA((n, 4)), pltpu.SemaphoreType.DMA((n, 4))])(*tensors)


def _exchange_chips(parts, name):
    n = len(parts)

    def body(*refs):
        p_refs, recv_refs = refs[:n], refs[n:2 * n]
        send_sems, recv_sems = refs[2 * n:]
        x, y, c = lax.axis_index("x"), lax.axis_index("y"), lax.axis_index("c")
        copies = []
        for t in range(n):
            for j in (1, 2, 3):
                copies.append(pltpu.make_async_remote_copy(
                    src_ref=p_refs[t].at[j], dst_ref=recv_refs[t].at[j - 1], send_sem=send_sems.at[t, j - 1],
                    recv_sem=recv_sems.at[t, j - 1], device_id=(*_chip_at(x, y, j), c), device_id_type=MESH))
        for cp in copies:
            cp.start()
        for cp in copies:
            cp.wait_recv()
        for cp in copies:
            cp.wait_send()

    any_spec = pl.BlockSpec(memory_space=pl.ANY)
    return pl.pallas_call(
        body, name=name, out_shape=[jax.ShapeDtypeStruct((3,) + a.shape[1:], a.dtype) for a in parts],
        in_specs=[any_spec] * n, out_specs=[any_spec] * n,
        scratch_shapes=[pltpu.SemaphoreType.DMA((n, 3)), pltpu.SemaphoreType.DMA((n, 3))])(*parts)


def _add_sibling(g8, recv1, name):
    _, a, b = g8.shape
    ta = next(cand for cand in (512, 384, 256, 128) if a % cand == 0)

    def body(g_ref, r_ref, o_ref):
        o_ref[...] = (g_ref[...].astype(F32) + r_ref[...].astype(F32)).astype(o_ref.dtype)

    def own(j, i):
        x, y, c = lax.axis_index("x"), lax.axis_index("y"), lax.axis_index("c")
        return (4 * (x ^ (j >> 1)) + 2 * (y ^ (j & 1)) + c, i, 0)

    slot = pl.BlockSpec((None, ta, b), lambda j, i: (j, i, 0))
    return pl.pallas_call(body, name=name, grid=(4, a // ta), in_specs=[pl.BlockSpec((None, ta, b), own), slot],
                          out_specs=slot, out_shape=jax.ShapeDtypeStruct((4, a, b), g8.dtype),
                          compiler_params=_params(2))(g8, recv1)


def _ada_forward(c_all, ada_w, ada_b_cols):
    depth, d, cols = ada_w.shape

    def body(c_ref, w_ref, b_ref, o_ref):
        cv = c_ref[...]
        act = cv * _sigmoid(cv)
        o_ref[...] = jnp.dot(act, w_ref[...], preferred_element_type=F32, precision=lax.Precision.HIGHEST) + b_ref[...]

    return pl.pallas_call(
        body, name="ada_forward", grid=(depth,),
        in_specs=[pl.BlockSpec((N_DEV, d), lambda l: (0, 0)), pl.BlockSpec((None, d, cols), lambda l: (l, 0, 0)),
                  pl.BlockSpec((None, 1, cols), lambda l: (l, 0, 0))],
        out_specs=pl.BlockSpec((None, N_DEV, cols), lambda l: (l, 0, 0)),
        out_shape=jax.ShapeDtypeStruct((depth, N_DEV, cols), F32), compiler_params=_params(1))(
            c_all, ada_w, ada_b_cols.reshape(depth, 1, cols))


def _ada_backward(c_all, dmod_cols):
    depth, _, cols = dmod_cols.shape
    d = c_all.shape[1]

    def body(c_ref, g_ref, o_ref):
        cv = c_ref[...]
        act = cv * _sigmoid(cv)
        o_ref[...] = lax.dot_general(act, g_ref[...], _TN, preferred_element_type=F32, precision=lax.Precision.HIGHEST)

    return pl.pallas_call(
        body, name="ada_backward", grid=(depth,),
        in_specs=[pl.BlockSpec((N_DEV, d), lambda l: (0, 0)), pl.BlockSpec((None, N_DEV, cols), lambda l: (l, 0, 0))],
        out_specs=pl.BlockSpec((None, d, cols), lambda l: (l, 0, 0)),
        out_shape=jax.ShapeDtypeStruct((depth, d, cols), F32), compiler_params=_params(1))(c_all, dmod_cols)


def _device_sum(a):
    _, r, w = a.shape

    def body(a_ref, o_ref):
        acc = a_ref[0]
        for dev in range(1, N_DEV):
            acc = acc + a_ref[dev]
        o_ref[...] = acc

    return pl.pallas_call(body, name="device_sum", out_shape=jax.ShapeDtypeStruct((r, w), F32),
                          in_specs=[pl.BlockSpec(memory_space=pltpu.VMEM)], out_specs=pl.BlockSpec(memory_space=pltpu.VMEM))(a)


def _adamw(w, g_layers, m, v, name):
    shape = w.shape
    cols = shape[-1]
    rows = int(np.prod(shape[:-1]))
    layers = len(g_layers)
    per_layer = rows // layers
    tile = next((cand for cand in (512, 352, 256, 128) if per_layer % cand == 0), per_layer)
    n_l = per_layer // tile
    n_parts = len(g_layers[0])

    def fn(pid, tv, hv, bv):
        wv, mv, vv = tv[:3]
        gv = None
        for l in range(layers):
            gl = None
            for part in tv[3 + l * n_parts:3 + (l + 1) * n_parts]:
                gl = part.astype(F32) if gl is None else gl + part.astype(F32)
            gv = gl if gv is None else jnp.where(pid >= l * n_l, gl, gv)
        mn = ADAM_B1 * mv + (1.0 - ADAM_B1) * gv
        vn = ADAM_B2 * vv + (1.0 - ADAM_B2) * (gv * gv)
        m_hat = mn / (1.0 - ADAM_B1 ** ADAM_STEP)
        v_hat = vn / (1.0 - ADAM_B2 ** ADAM_STEP)
        delta = -ADAM_LR * (m_hat / (jnp.sqrt(v_hat) + ADAM_EPS) + ADAM_WD * wv)
        return [gv, delta, mn, vn], []

    tiled = [(a.reshape(rows, cols), 0, cols) for a in (w, m, v)]
    for l, parts in enumerate(g_layers):
        clamp = functools.partial(lambda i, l: jnp.clip(i - l * n_l, 0, n_l - 1), l=l)
        tiled += [(p.reshape(per_layer, cols), 0, cols, clamp) for p in parts]
    res = _rowwise(fn, name, rows=rows, tile=tile, tiled=tiled, outs=[(cols, F32)] * 4)
    return [r.reshape(shape) for r in res]


def _norm_modulate(x, y, gate, ln_g, scale, shift, name, tile):
    t, d = x.shape

    def fn(pid, tv, hv, bv):
        if y is None:
            xn = tv[0]
            g_ln, sc, sh = bv
        else:
            g_gate, g_ln, sc, sh = bv
            xn = tv[0] + g_gate * tv[1]
        r = lax.rsqrt(jnp.mean(xn * xn, axis=-1, keepdims=True) + EPS)
        h = (xn * r * g_ln) * (1.0 + sc) + sh
        return ([h] if y is None else [xn, h]), []

    if y is None:
        h, = _rowwise(fn, name, rows=t, tile=tile, tiled=[(x, 0, d)], bcast=[ln_g, scale, shift], outs=[(d, BF16)])
        return x, h
    xn, h = _rowwise(fn, name, rows=t, tile=tile, tiled=[(x, 0, d), (y, 0, d)], bcast=[gate, ln_g, scale, shift],
                     outs=[(d, F32), (d, BF16)])
    return xn, h


def _norm_backward(dh, x, dres, ln_g, scale, name, tile):
    t, d = x.shape

    def fn(pid, tv, hv, bv):
        dhv, xv, dr = tv
        g_ln, sc = bv
        r = lax.rsqrt(jnp.mean(xv * xv, axis=-1, keepdims=True) + EPS)
        xn = xv * r
        dxn = dhv * (1.0 + sc) * g_ln
        dx = dr + r * (dxn - xn * jnp.mean(dxn * xn, axis=-1, keepdims=True))
        return [dx], [_colsum(dhv), _colsum(dhv * (xn * g_ln)), _colsum(dhv * (1.0 + sc) * xn)]

    return _rowwise(fn, name, rows=t, tile=tile, tiled=[(dh, 0, d), (x, 0, d), (dres, 0, d)], bcast=[ln_g, scale],
                    outs=[(d, F32)], accs=[(1, d)] * 3)


def _shift_rows(u, halo, k, pid, first_tile_zero):
    rows = lax.broadcasted_iota(jnp.int32, u.shape, 0)
    halo = halo * jnp.where(pid == 0, 0.0, 1.0) if first_tile_zero else halo
    out = pltpu.roll(u, k, axis=0)
    for j in range(k):
        out = jnp.where(rows == j, halo[8 - k + j:8 - k + j + 1, :], out)
    return out


def _shift_rows_up(u, halo, k, pid, n_tiles):
    tile = u.shape[0]
    rows = lax.broadcasted_iota(jnp.int32, u.shape, 0)
    halo = halo * jnp.where(pid == n_tiles - 1, 0.0, 1.0)
    out = pltpu.roll(u, tile - k, axis=0)
    for j in range(k):
        out = jnp.where(rows == tile - k + j, halo[j:j + 1, :], out)
    return out


def _layer_forward(x_in, y_prev, gate_prev, mod, w, l, tile):
    sh1, sc1, g1, sh2, sc2, g2 = mod
    t, d = x_in.shape
    f = w["w_g"].shape[1]
    bd2 = _head_sum_matrix()
    s = {}
    s["x"], s["h"] = _norm_modulate(x_in, y_prev, gate_prev, w["ln1"], sc1, sh1, f"l{l}_norm1", tile)
    p = _matmul([(s["h"], 0, w["w_in"], 0)], "nn", m=t, n=8 * d, k=d, tm=1024, tn=512, out_dtype=F32, name=f"l{l}_in_proj")
    s["p"] = p

    def qk_norm(pid, tv, hv, bv):
        qr, kr, vr = tv
        qg, kg, bd = bv

        def nrm(xv, g):
            r = lax.rsqrt(_head_sums(xv * xv, bd) * (1.0 / HEAD_DIM) + EPS)
            return xv * r * g
        return [nrm(qr, qg) * 0.125, nrm(kr, kg), vr], []

    s["qn"], s["kn"], s["vb"] = _rowwise(qk_norm, f"l{l}_qk_norm", rows=t, tile=tile, tiled=[(p, 0, d), (p, 1, d), (p, 2, d)],
                                         bcast=[w["qg"], w["kg"], bd2], outs=[(d, BF16)] * 3)
    s["ya"] = _attention_fwd(s["qn"], s["kn"], s["vb"], f"l{l}_attention")

    def conv_fwd(pid, tv, hv, bv):
        cb, cc, cx = tv
        u = cc * cx
        hu = hv[0] * hv[1]
        cw = bv[0]
        conv = cw[0:1, :] * _shift_rows(u, hu, 2, pid, True) + cw[1:2, :] * _shift_rows(u, hu, 1, pid, True) + cw[2:3, :] * u
        return [cb * conv], []

    s["yb"], = _rowwise(conv_fwd, f"l{l}_conv", rows=t, tile=tile, tiled=[(p, 3, d), (p, 4, d), (p, 5, d)],
                        halos=[(p, 4, d, "prev"), (p, 5, d, "prev")], bcast=[w["conv"]], outs=[(d, BF16)])
    s["a"] = _matmul([(s["ya"], 0, w["w_a"], 0)], "nn", m=t, n=d, k=d, tm=1024, tn=512, out_dtype=F32, name=f"l{l}_branch_a")
    s["b"] = _matmul([(s["yb"], 0, w["w_b"], 0)], "nn", m=t, n=d, k=d, tm=1024, tn=512, out_dtype=F32, name=f"l{l}_branch_b")

    def merge(pid, tv, hv, bv):
        av, bvv, ga, gb = tv
        return [_sigmoid(ga) * av + _sigmoid(gb) * bvv], []

    s["merged"], = _rowwise(merge, f"l{l}_merge", rows=t, tile=tile, tiled=[(s["a"], 0, d), (s["b"], 0, d), (p, 6, d), (p, 7, d)],
                            outs=[(d, BF16)])
    s["mo"] = _matmul([(s["merged"], 0, w["w_o"], 0)], "nn", m=t, n=d, k=d, tm=1024, tn=512, out_dtype=F32, name=f"l{l}_out_proj")
    s["x2"], s["h2"] = _norm_modulate(s["x"], s["mo"], g1, w["ln2"], sc2, sh2, f"l{l}_norm2", tile)
    fn_tile = f // 2
    s["g"] = _matmul([(s["h2"], 0, w["w_g"], 0)], "nn", m=t, n=f, k=d, tm=512, tn=fn_tile, out_dtype=F32, name=f"l{l}_ffn_gate")
    s["u"] = _matmul([(s["h2"], 0, w["w_u"], 0)], "nn", m=t, n=f, k=d, tm=512, tn=fn_tile, out_dtype=F32, name=f"l{l}_ffn_up")

    def swiglu(pid, tv, hv, bv):
        gv, uv = tv
        return [gv * _sigmoid(gv) * uv], []

    s["s"], = _rowwise(swiglu, f"l{l}_swiglu", rows=t, tile=tile // 2, tiled=[(s["g"], 0, f), (s["u"], 0, f)], outs=[(f, BF16)])
    s["f"] = _matmul([(s["s"], 0, w["w_d"], 0)], "nn", m=t, n=d, k=f, tm=512, tn=512, out_dtype=F32, name=f"l{l}_ffn_down")
    return s


def _layer_backward(dx3, s, mod, w, l, tile):
    sh1, sc1, g1, sh2, sc2, g2 = mod
    t, d = dx3.shape
    f = w["w_g"].shape[1]
    p = s["p"]
    bd2 = _head_sum_matrix()
    n_tiles = t // tile
    grads = {}

    def gate_bwd(pid, tv, hv, bv):
        return [tv[0] * bv[0]], [_colsum(tv[0] * tv[1])]

    df, dg2 = _rowwise(gate_bwd, f"l{l}_bwd_gate2", rows=t, tile=tile, tiled=[(dx3, 0, d), (s["f"], 0, d)], bcast=[g2],
                       outs=[(d, BF16)], accs=[(1, d)])
    grads["w_d"] = _matmul([(s["s"], 0, df, 0)], "tn", m=f, n=d, k=t, tm=256, tn=512, out_dtype=BF16, name=f"l{l}_dw_down")
    ds = _matmul([(df, 0, w["w_d"], 0)], "nt", m=t, n=f, k=d, tm=512, tn=f // 2, out_dtype=F32, name=f"l{l}_d_swiglu")

    def swiglu_bwd(pid, tv, hv, bv):
        dsv, gv, uv = tv
        sig = _sigmoid(gv)
        return [dsv * uv * (sig * (1.0 + gv * (1.0 - sig))), dsv * (gv * sig)], []

    dgt, dup = _rowwise(swiglu_bwd, f"l{l}_bwd_swiglu", rows=t, tile=tile // 2, tiled=[(ds, 0, f), (s["g"], 0, f), (s["u"], 0, f)],
                        outs=[(f, BF16)] * 2)
    grads["w_g"] = _matmul([(s["h2"], 0, dgt, 0)], "tn", m=d, n=f, k=t, tm=256, tn=f // 2, out_dtype=BF16, name=f"l{l}_dw_gate")
    grads["w_u"] = _matmul([(s["h2"], 0, dup, 0)], "tn", m=d, n=f, k=t, tm=256, tn=f // 2, out_dtype=BF16, name=f"l{l}_dw_up")
    dh2 = _matmul([(dgt, 0, w["w_g"], 0), (dup, 0, w["w_u"], 0)], "nt", m=t, n=d, k=f, tm=512, tn=512, out_dtype=F32,
                  name=f"l{l}_dh2")
    dx2, dsh2, dsc2, grads["ln2"] = _norm_backward(dh2, s["x2"], dx3, w["ln2"], sc2, f"l{l}_bwd_norm2", tile)

    dmo, dg1 = _rowwise(gate_bwd, f"l{l}_bwd_gate1", rows=t, tile=tile, tiled=[(dx2, 0, d), (s["mo"], 0, d)], bcast=[g1],
                        outs=[(d, BF16)], accs=[(1, d)])
    grads["w_o"] = _matmul([(s["merged"], 0, dmo, 0)], "tn", m=d, n=d, k=t, tm=256, tn=512, out_dtype=BF16, name=f"l{l}_dw_out")
    dmerged = _matmul([(dmo, 0, w["w_o"], 0)], "nt", m=t, n=d, k=d, tm=1024, tn=512, out_dtype=F32, name=f"l{l}_d_merged")

    def merge_bwd(pid, tv, hv, bv):
        dm, av, bvv, ga, gb = tv
        sa, sb = _sigmoid(ga), _sigmoid(gb)
        return [dm * sa, dm * sb, dm * av * (sa * (1.0 - sa)), dm * bvv * (sb * (1.0 - sb))], []

    d_a, d_b, dga, dgb = _rowwise(merge_bwd, f"l{l}_bwd_merge", rows=t, tile=tile,
                                  tiled=[(dmerged, 0, d), (s["a"], 0, d), (s["b"], 0, d), (p, 6, d), (p, 7, d)], outs=[(d, BF16)] * 4)
    grads["w_a"] = _matmul([(s["ya"], 0, d_a, 0)], "tn", m=d, n=d, k=t, tm=256, tn=512, out_dtype=BF16, name=f"l{l}_dw_a")
    grads["w_b"] = _matmul([(s["yb"], 0, d_b, 0)], "tn", m=d, n=d, k=t, tm=256, tn=512, out_dtype=BF16, name=f"l{l}_dw_b")
    dya = _matmul([(d_a, 0, w["w_a"], 0)], "nt", m=t, n=d, k=d, tm=1024, tn=512, out_dtype=BF16, name=f"l{l}_d_ya")
    dyb = _matmul([(d_b, 0, w["w_b"], 0)], "nt", m=t, n=d, k=d, tm=1024, tn=512, out_dtype=F32, name=f"l{l}_d_yb")

    def conv_bwd(pid, tv, hv, bv):
        dy, cb, cc, cx = tv
        cw = bv[0]
        u, hu = cc * cx, hv[0] * hv[1]
        u1, u2 = _shift_rows(u, hu, 1, pid, True), _shift_rows(u, hu, 2, pid, True)
        conv = cw[0:1, :] * u2 + cw[1:2, :] * u1 + cw[2:3, :] * u
        dconv, hd = dy * cb, hv[2] * hv[3]
        du = (cw[2:3, :] * dconv + cw[1:2, :] * _shift_rows_up(dconv, hd, 1, pid, n_tiles)
              + cw[0:1, :] * _shift_rows_up(dconv, hd, 2, pid, n_tiles))
        return [dy * conv, du * cx, du * cc], [_colsum(dconv * u2), _colsum(dconv * u1), _colsum(dconv * u)]

    dcb, dcc, dcx, dcw0, dcw1, dcw2 = _rowwise(
        conv_bwd, f"l{l}_bwd_conv", rows=t, tile=tile, tiled=[(dyb, 0, d), (p, 3, d), (p, 4, d), (p, 5, d)],
        halos=[(p, 4, d, "prev"), (p, 5, d, "prev"), (dyb, 0, d, "next"), (p, 3, d, "next")], bcast=[w["conv"]],
        outs=[(d, BF16)] * 3, accs=[(1, d)] * 3)
    grads["conv"] = jnp.concatenate([dcw0, dcw1, dcw2], axis=0)

    dqs, dkn, dv = _attention_bwd(s["qn"], s["kn"], s["vb"], dya, f"l{l}_bwd_attention")

    def qk_norm_bwd(pid, tv, hv, bv):
        dq, dk, qr, kr, dvv = tv
        qg, kg, bd = bv

        def bwd(dy, xv, g):
            r = lax.rsqrt(_head_sums(xv * xv, bd) * (1.0 / HEAD_DIM) + EPS)
            yv = xv * r
            dyn = dy * g
            dx = r * (dyn - yv * (_head_sums(dyn * yv, bd) * (1.0 / HEAD_DIM)))
            return dx, _colsum(dy * yv)

        dxq, dgq = bwd(dq * 0.125, qr, qg)
        dxk, dgk = bwd(dk, kr, kg)
        return [dxq, dxk, dvv], [dgq, dgk]

    dqr, dkr, dvb, grads["qg"], grads["kg"] = _rowwise(
        qk_norm_bwd, f"l{l}_bwd_qk_norm", rows=t, tile=tile, tiled=[(dqs, 0, d), (dkn, 0, d), (p, 0, d), (p, 1, d), (dv, 0, d)],
        bcast=[w["qg"], w["kg"], bd2], outs=[(d, BF16)] * 3, accs=[(1, d)] * 2)

    dp = [dqr, dkr, dvb, dcb, dcc, dcx, dga, dgb]
    grads["w_in"] = [_matmul([(s["h"], 0, dpk, 0)], "tn", m=d, n=d, k=t, tm=256, tn=512, out_dtype=BF16, name=f"l{l}_dw_in{k}")
                     for k, dpk in enumerate(dp)]
    dh = _matmul([(dpk, 0, w["w_in"], k) for k, dpk in enumerate(dp)], "nt", m=t, n=d, k=d, tm=512, tn=512, out_dtype=F32,
                 name=f"l{l}_dh")
    dx, dsh1, dsc1, grads["ln1"] = _norm_backward(dh, s["x"], dx2, w["ln1"], sc1, f"l{l}_bwd_norm1", tile)
    return dx, grads, [dsh1, dsc1, dg1, dsh2, dsc2, dg2]


_BIG = ["w_in", "w_branch_a", "w_branch_b", "w_out", "w_ffn_gate", "w_ffn_up", "w_ffn_down"]
_SHORT = dict(w_in="w_in", w_branch_a="w_a", w_branch_b="w_b", w_out="w_o", w_ffn_gate="w_g", w_ffn_up="w_u", w_ffn_down="w_d")
_COL_SHARDED = {"w_in", "w_ffn_gate", "w_ffn_up"}


def kernel(x, c, ada_w, ada_b, ln1_g, w_in, q_norm_g, k_norm_g, conv_w, w_branch_a, w_branch_b, w_out, ln2_g, w_ffn_gate, w_ffn_up, w_ffn_down, loss_target, m_ada_w, m_ada_b, m_ln1_g, m_w_in, m_q_norm_g, m_k_norm_g, m_conv_w, m_w_branch_a, m_w_branch_b, m_w_out, m_ln2_g, m_w_ffn_gate, m_w_ffn_up, m_w_ffn_down, v_ada_w, v_ada_b, v_ln1_g, v_w_in, v_q_norm_g, v_k_norm_g, v_conv_w, v_w_branch_a, v_w_branch_b, v_w_out, v_ln2_g, v_w_ffn_gate, v_w_ffn_up, v_w_ffn_down):
    weights = dict(ada_w=ada_w, ada_b=ada_b, ln1_g=ln1_g, w_in=w_in, q_norm_g=q_norm_g, k_norm_g=k_norm_g, conv_w=conv_w,
                   w_branch_a=w_branch_a, w_branch_b=w_branch_b, w_out=w_out, ln2_g=ln2_g, w_ffn_gate=w_ffn_gate,
                   w_ffn_up=w_ffn_up, w_ffn_down=w_ffn_down)
    m_in = dict(ada_w=m_ada_w, ada_b=m_ada_b, ln1_g=m_ln1_g, w_in=m_w_in, q_norm_g=m_q_norm_g, k_norm_g=m_k_norm_g,
                conv_w=m_conv_w, w_branch_a=m_w_branch_a, w_branch_b=m_w_branch_b, w_out=m_w_out, ln2_g=m_ln2_g,
                w_ffn_gate=m_w_ffn_gate, w_ffn_up=m_w_ffn_up, w_ffn_down=m_w_ffn_down)
    v_in = dict(ada_w=v_ada_w, ada_b=v_ada_b, ln1_g=v_ln1_g, w_in=v_w_in, q_norm_g=v_q_norm_g, k_norm_g=v_k_norm_g,
                conv_w=v_conv_w, w_branch_a=v_w_branch_a, w_branch_b=v_w_branch_b, w_out=v_w_out, ln2_g=v_ln2_g,
                w_ffn_gate=v_w_ffn_gate, w_ffn_up=v_w_ffn_up, w_ffn_down=v_w_ffn_down)
    names = list(weights)

    mx, my, mc = lax.axis_index("x"), lax.axis_index("y"), lax.axis_index("c")
    me = 4 * mx + 2 * my + mc
    xs, target = x[0], loss_target[0]
    t, d = xs.shape
    depth = ada_w.shape[0]
    mod_cols = ada_w.shape[2]
    conv_cols = conv_w.shape[2]
    row_w = 1024
    tile = 512 if t % 512 == 0 else t

    small = jnp.concatenate([c.reshape(-1), conv_w.reshape(-1)])
    small_n = -(-small.shape[0] // row_w) * row_w
    small = jnp.pad(small, (0, small_n - small.shape[0])).reshape(-1, row_w)
    small_all = _all_gather([small], "gather_cond")[0].reshape(N_DEV, -1)
    c_all = small_all[:, :d]
    conv_full = jnp.transpose(small_all[:, d:d + depth * 3 * conv_cols].reshape(N_DEV, depth, 3, conv_cols), (1, 2, 0, 3)
                              ).reshape(depth, 3, N_DEV * conv_cols)

    ada_b_cols = lax.dynamic_slice_in_dim(ada_b, me * mod_cols, mod_cols, axis=1)
    mod_part = _ada_forward(c_all, ada_w, ada_b_cols)
    mod_all = _all_gather([mod_part.reshape(-1, row_w)], "gather_mod")[0].reshape(N_DEV, depth, N_DEV, mod_cols)
    mod_mine = lax.dynamic_index_in_dim(mod_all, me, axis=2, keepdims=False)
    mod = jnp.transpose(mod_mine, (1, 0, 2)).reshape(depth, 6, 1, d)

    f_shard = w_ffn_gate.shape[2]
    f_pad = -(-f_shard // LANES) * LANES - f_shard
    pads = dict(w_ffn_gate=((0, 0), (0, 0), (0, f_pad)), w_ffn_up=((0, 0), (0, 0), (0, f_pad)), w_ffn_down=((0, 0), (0, f_pad), (0, 0)))
    shards = [weights[n].astype(BF16) for n in _BIG]
    shards = [jnp.pad(a, pads[n]) if n in pads else a for n, a in zip(_BIG, shards)]
    gathered = dict(zip(_BIG, _all_gather(shards, "gather_weights")))
    layer_w = []
    for l in range(depth):
        wl = {}
        for n in _BIG:
            al = gathered[n][:, l]
            wl[_SHORT[n]] = (jnp.transpose(al, (1, 0, 2)).reshape(al.shape[1], -1) if n in _COL_SHARDED
                             else al.reshape(-1, al.shape[2]))
        wl["ln1"], wl["ln2"] = ln1_g[l][None], ln2_g[l][None]
        wl["qg"] = jnp.tile(q_norm_g[l], d // HEAD_DIM)[None]
        wl["kg"] = jnp.tile(k_norm_g[l], d // HEAD_DIM)[None]
        wl["conv"] = conv_full[l]
        layer_w.append(wl)

    saved = []
    x_cur, y_prev, gate_prev = xs, None, None
    for l in range(depth):
        mods = [mod[l, k] for k in range(6)]
        s = _layer_forward(x_cur, y_prev, gate_prev, mods, layer_w[l], l, tile)
        saved.append(s)
        x_cur, y_prev, gate_prev = s["x2"], s["f"], mods[5]

    def loss_head(pid, tv, hv, bv):
        diff = tv[0] + bv[0] * tv[1] - tv[2]
        return [diff * (1.0 / d)], [_colsum(diff * diff) * (0.5 / d)]

    dx, loss_cols = _rowwise(loss_head, "loss_head", rows=t, tile=tile, tiled=[(x_cur, 0, d), (y_prev, 0, d), (target, 0, d)],
                             bcast=[gate_prev], outs=[(d, F32)], accs=[(1, d)])

    layer_g, dmods = [None] * depth, [None] * depth
    for l in reversed(range(depth)):
        mods = [mod[l, k] for k in range(6)]
        dx, layer_g[l], dmods[l] = _layer_backward(dx, saved[l], mods, layer_w[l], l, tile)
    grad_x = dx[None]

    pieces = [jnp.concatenate(dmods[l], axis=1) for l in range(depth)]
    for key in ("ln1", "ln2", "qg", "kg"):
        pieces += [layer_g[l][key] for l in range(depth)]
    pieces += [layer_g[l]["conv"].reshape(1, -1) for l in range(depth)]
    pieces.append(loss_cols)
    part_small = jnp.concatenate(pieces, axis=1).reshape(-1, row_w)
    part_all = _all_gather([part_small], "gather_small_grads")[0]
    summed = _device_sum(part_all).reshape(-1)
    n_mod = depth * 6 * d
    dmod_all = part_all.reshape(N_DEV, -1)[:, :n_mod].reshape(N_DEV, depth, 6 * d)
    dmod_cols = jnp.transpose(lax.dynamic_slice_in_dim(dmod_all, me * mod_cols, mod_cols, axis=2), (1, 0, 2))
    g = {"ada_w": _ada_backward(c_all, dmod_cols), "ada_b": summed[:n_mod].reshape(depth, 6 * d)}
    off = n_mod
    g["ln1_g"] = summed[off:off + depth * d].reshape(depth, d)
    g["ln2_g"] = summed[off + depth * d:off + 2 * depth * d].reshape(depth, d)
    g["q_norm_g"] = summed[off + 2 * depth * d:off + 3 * depth * d].reshape(depth, d // HEAD_DIM, HEAD_DIM).sum(axis=1)
    g["k_norm_g"] = summed[off + 3 * depth * d:off + 4 * depth * d].reshape(depth, d // HEAD_DIM, HEAD_DIM).sum(axis=1)
    off += 4 * depth * d
    conv_g = summed[off:off + depth * 3 * d].reshape(depth, 3, N_DEV, conv_cols)
    g["conv_w"] = lax.dynamic_index_in_dim(conv_g, me, axis=2, keepdims=False)
    off += depth * 3 * d
    loss = jnp.sum(summed[off:off + d])

    def by_owner(n, full):
        if n == "w_in":
            return jnp.stack(full)
        if n in _COL_SHARDED:
            return jnp.transpose(full.reshape(full.shape[0], N_DEV, -1), (1, 0, 2))
        return full.reshape(N_DEV, -1, full.shape[1])

    tensors = [by_owner(n, layer_g[l][_SHORT[n]]) for l in range(depth) for n in _BIG]
    recv1 = _exchange_sibling(tensors, "rs_exchange_sibling")
    parts = [_add_sibling(a, r, f"rs_add_sibling_{i}") for i, (a, r) in enumerate(zip(tensors, recv1))]
    recv2 = _exchange_chips(parts, "rs_exchange_chips")

    def shard_of(n, a):
        if n in ("w_ffn_gate", "w_ffn_up"):
            return a[:, :f_shard]
        return a[:f_shard] if n == "w_ffn_down" else a

    outs = {}
    for k, n in enumerate(_BIG):
        g_layers = [[shard_of(n, a) for a in (parts[l * len(_BIG) + k][0], *recv2[l * len(_BIG) + k])] for l in range(depth)]
        outs[n] = _adamw(weights[n], g_layers, m_in[n], v_in[n], f"adamw_{n}")
    for n in names:
        if n not in outs:
            outs[n] = _adamw(weights[n], [[g[n]]], m_in[n], v_in[n], f"adamw_{n}")
    return (loss, grad_x, *[outs[n][0] for n in names], *[outs[n][1] for n in names], *[outs[n][2] for n in names],
            *[outs[n][3] for n in names])
```

```python
import functools

import numpy as np
import jax
import jax.numpy as jnp
from jax import lax
from jax.experimental import pallas as pl
from jax.experimental.pallas import tpu as pltpu

F32, BF16 = jnp.float32, jnp.bfloat16
MESH = pl.DeviceIdType.MESH
N_DEV = 8
LANES = 128
HEAD_DIM = 64
HEAD_PAIR = 2 * HEAD_DIM
ATT_TILE = 256
ATT_UNROLL = 4
EPS = 1e-6
VMEM_LIMIT = 56 * 1024 * 1024

ADAM_LR, ADAM_B1, ADAM_B2, ADAM_EPS, ADAM_WD, ADAM_STEP = 0.001, 0.9, 0.999, 1e-08, 0.01, 10

_NT = (((1,), (1,)), ((), ()))
_TN = (((0,), (0,)), ((), ()))
_NN = (((1,), (0,)), ((), ()))


def _params(n_grid):
    return pltpu.CompilerParams(dimension_semantics=("arbitrary",) * n_grid, vmem_limit_bytes=VMEM_LIMIT)


def _sigmoid(x):
    return 1.0 / (1.0 + jnp.exp(-x))


def _rowwise(fn, name, *, rows, tile, tiled=(), halos=(), bcast=(), outs=(), accs=()):
    n = rows // tile
    assert n * tile == rows
    in_specs, args = [], []
    for t in tiled:
        arr, cb, w = t[:3]
        rowmap = t[3] if len(t) > 3 else (lambda i: i)
        in_specs.append(pl.BlockSpec((tile, w), functools.partial(lambda i, cb, rowmap: (rowmap(i), cb), cb=cb, rowmap=rowmap)))
        args.append(arr)
    per_tile8 = tile // 8
    for arr, cb, w, side in halos:
        last8 = arr.shape[0] // 8 - 1
        if side == "prev":
            imap = functools.partial(lambda i, cb: (jnp.maximum(i * per_tile8 - 1, 0), cb), cb=cb)
        else:
            imap = functools.partial(lambda i, cb, last8: (jnp.minimum((i + 1) * per_tile8, last8), cb), cb=cb, last8=last8)
        in_specs.append(pl.BlockSpec((8, w), imap))
        args.append(arr)
    for arr in bcast:
        in_specs.append(pl.BlockSpec(arr.shape, functools.partial(lambda i, nd: (0,) * nd, nd=arr.ndim)))
        args.append(arr)
    out_shape = [jax.ShapeDtypeStruct((rows, w), dt) for w, dt in outs] + [jax.ShapeDtypeStruct(s, F32) for s in accs]
    out_specs = [pl.BlockSpec((tile, w), lambda i: (i, 0)) for w, _ in outs] + [pl.BlockSpec(s, lambda i: (0, 0)) for s in accs]
    nt, nh, nb, no, na = len(tiled), len(halos), len(bcast), len(outs), len(accs)

    def body(*refs):
        pid = pl.program_id(0)
        tv = [r[...] for r in refs[:nt]]
        hv = [r[...] for r in refs[nt:nt + nh]]
        bv = [r[...] for r in refs[nt + nh:nt + nh + nb]]
        out_refs = refs[nt + nh + nb:nt + nh + nb + no]
        acc_refs = refs[nt + nh + nb + no:]
        ov, av = fn(pid, tv, hv, bv)
        for r, v in zip(out_refs, ov):
            r[...] = v.astype(r.dtype)
        if na:
            @pl.when(pid == 0)
            def _():
                for r in acc_refs:
                    r[...] = jnp.zeros(r.shape, F32)
            for r, v in zip(acc_refs, av):
                r[...] += v

    res = pl.pallas_call(body, name=name, grid=(n,), in_specs=in_specs, out_specs=out_specs, out_shape=out_shape,
                         compiler_params=_params(1))(*args)
    return res


def _colsum(v):
    return jnp.sum(v, axis=0, keepdims=True)


def _matmul(pairs, mode, *, m, n, k, tm, tn, out_dtype, name):
    tm, tn = min(tm, m), min(tn, n)
    assert m % tm == 0 and n % tn == 0
    in_specs, args = [], []
    for a, acb, b, bcb in pairs:
        if mode == "tn":
            in_specs.append(pl.BlockSpec((k, tm), functools.partial(lambda i, j, o: (0, o + i), o=acb)))
            in_specs.append(pl.BlockSpec((k, tn), functools.partial(lambda i, j, o: (0, o + j), o=bcb)))
        elif mode == "nn":
            in_specs.append(pl.BlockSpec((tm, k), functools.partial(lambda i, j, o: (i, o), o=acb)))
            in_specs.append(pl.BlockSpec((k, tn), functools.partial(lambda i, j, o: (0, o + j), o=bcb)))
        else:
            in_specs.append(pl.BlockSpec((tm, k), functools.partial(lambda i, j, o: (i, o), o=acb)))
            in_specs.append(pl.BlockSpec((tn, k), functools.partial(lambda i, j, o: (j, o), o=bcb)))
        args += [a, b]
    dims = {"nn": _NN, "nt": _NT, "tn": _TN}[mode]
    npairs = len(pairs)

    def body(*refs):
        o_ref = refs[2 * npairs]
        acc = None
        for p in range(npairs):
            d = lax.dot_general(refs[2 * p][...].astype(BF16), refs[2 * p + 1][...].astype(BF16), dims,
                                preferred_element_type=F32)
            acc = d if acc is None else acc + d
        o_ref[...] = acc.astype(o_ref.dtype)

    return pl.pallas_call(body, name=name, grid=(m // tm, n // tn), in_specs=in_specs,
                          out_specs=pl.BlockSpec((tm, tn), lambda i, j: (i, j)),
                          out_shape=jax.ShapeDtypeStruct((m, n), out_dtype), compiler_params=_params(2))(*args)


def _scan_matrix(kind, n):
    r = np.arange(n)
    tri = (r[:, None] > r[None, :]) if kind == "suffix" else (r[:, None] < r[None, :])
    half = np.concatenate([tri.astype(np.float32), np.ones((n, LANES), np.float32)], axis=1)
    return jnp.asarray(np.concatenate([half, half], axis=0), BF16)


def _head_sum_matrix():
    r = np.arange(LANES)
    bd = (r[:, None] // HEAD_DIM == r[None, :] // HEAD_DIM).astype(np.float32)
    return jnp.asarray(np.concatenate([bd, bd], axis=0), BF16)


def _split_cat(v):
    hi = v.astype(BF16)
    lo = (v - hi.astype(F32)).astype(BF16)
    return jnp.concatenate([hi, lo], axis=1)


def _head_sums(v, bd2):
    hi = v.astype(BF16)
    lo = (v - hi.astype(F32)).astype(BF16)
    parts = []
    for g in range(v.shape[1] // LANES):
        sl = slice(g * LANES, (g + 1) * LANES)
        parts.append(jnp.dot(jnp.concatenate([hi[:, sl], lo[:, sl]], axis=1), bd2, preferred_element_type=F32))
    return jnp.concatenate(parts, axis=1)


def _scan_parts(v, scan):
    return [jnp.dot(_split_cat(v[:, b * LANES:(b + 1) * LANES]), scan, preferred_element_type=F32)
            for b in range(v.shape[1] // LANES)]


def _chain_sums(parts, carry, reverse):
    nb = len(parts)
    outs = [None] * nb
    for b in (reversed(range(nb)) if reverse else range(nb)):
        outs[b] = carry + parts[b][:, :LANES]
        carry = carry + parts[b][:, LANES:]
    return jnp.concatenate(outs, axis=1), carry


def _log_weights(z, mask):
    sp = jnp.log(1.0 + jnp.exp(-jnp.abs(z)))
    log_not = -(jnp.maximum(z, 0.0) + sp)
    log_beta = z + log_not
    return (log_not if mask is None else jnp.where(mask, log_not, 0.0)), log_beta


def _pipeline(n_chains, stages):
    for step in range(n_chains + len(stages) - 1):
        for s, stage in enumerate(stages):
            if 0 <= step - s < n_chains:
                stage(step - s)


def _head_masks(tq):
    lane = lax.broadcasted_iota(jnp.int32, (tq, HEAD_PAIR), 1)
    return [(lane // HEAD_DIM) == hh for hh in range(2)]


def _diag_mask(tq):
    return lax.broadcasted_iota(jnp.int32, (tq, tq), 1) < lax.broadcasted_iota(jnp.int32, (tq, tq), 0)


def _attention_fwd(qn, kn, vb, name):
    t, d = qn.shape
    tq = min(ATT_TILE, t)
    nq, hp = t // tq, d // HEAD_PAIR
    scan_suffix = _scan_matrix("suffix", LANES)

    def body(q_ref, k_ref, v_ref, sc_ref, o_ref):
        qi = pl.program_id(1)
        q2 = q_ref[...].astype(F32)
        scan = sc_ref[...]
        heads, diag = _head_masks(tq), _diag_mask(tq)
        qms = [jnp.where(h, q2, 0.0).astype(BF16) for h in heads]

        def rows_of(j):
            return pl.ds(pl.multiple_of(j * tq, tq), tq)

        def step(js, state, mask):
            carry, acc = [state[0], state[2]], [state[1], state[3]]
            kts, vts = [k_ref[rows_of(j), :] for j in js], [v_ref[rows_of(j), :] for j in js]
            z, lw, parts, a = {}, {}, {}, {}

            def s_scores(c):
                z[c] = lax.dot_general(qms[c % 2], kts[c // 2], _NT, preferred_element_type=F32)

            def s_logs(c):
                lw[c] = _log_weights(z.pop(c), mask)

            def s_scan(c):
                parts[c] = _scan_parts(lw[c][0], scan)

            def s_weights(c):
                tail, carry[c % 2] = _chain_sums(parts.pop(c), carry[c % 2], True)
                av = jnp.exp(lw.pop(c)[1] + tail)
                a[c] = (av if mask is None else jnp.where(mask, av, 0.0)).astype(BF16)

            def s_values(c):
                acc[c % 2] = acc[c % 2] + jnp.dot(a.pop(c), vts[c // 2], preferred_element_type=F32)

            _pipeline(2 * len(js), [s_scores, s_logs, s_scan, s_weights, s_values])
            return (carry[0], acc[0], carry[1], acc[1])

        zero, zq = jnp.zeros((tq, LANES), F32), jnp.zeros((tq, HEAD_PAIR), F32)
        state = step([qi], (zero, zq, zero, zq), diag)
        rem = qi % ATT_UNROLL
        state = lax.fori_loop(0, rem, lambda p, c: step([qi - 1 - p], c, None), state)
        state = lax.fori_loop(0, qi // ATT_UNROLL,
                              lambda p, c: step([qi - 1 - rem - ATT_UNROLL * p - u for u in range(ATT_UNROLL)], c, None), state)
        o_ref[...] = jnp.where(heads[0], state[1], state[3]).astype(o_ref.dtype)

    return pl.pallas_call(
        body, name=name, grid=(hp, nq),
        in_specs=[pl.BlockSpec((tq, HEAD_PAIR), lambda h, i: (i, h)),
                  pl.BlockSpec((t, HEAD_PAIR), lambda h, i: (0, h)),
                  pl.BlockSpec((t, HEAD_PAIR), lambda h, i: (0, h)),
                  pl.BlockSpec(scan_suffix.shape, lambda h, i: (0, 0))],
        out_specs=pl.BlockSpec((tq, HEAD_PAIR), lambda h, i: (i, h)),
        out_shape=jax.ShapeDtypeStruct((t, d), BF16), compiler_params=_params(2))(qn, kn, vb, scan_suffix)


def _attention_bwd(qn, kn, vb, dob, name):
    t, d = qn.shape
    tq = min(ATT_TILE, t)
    nq, hp = t // tq, d // HEAD_PAIR
    scan_suffix, scan_prefix = _scan_matrix("suffix", LANES), _scan_matrix("prefix", LANES)

    def body(q_ref, k_ref, v_ref, do_ref, ss_ref, sp_ref, dq_ref, dk_ref, dv_ref, g_s, b_s):
        qi = pl.program_id(1)

        @pl.when(qi == 0)
        def _():
            dk_ref[...] = jnp.zeros(dk_ref.shape, F32)
            dv_ref[...] = jnp.zeros(dv_ref.shape, F32)

        q2, do2 = q_ref[...].astype(F32), do_ref[...].astype(F32)
        ssuf, spre = ss_ref[...], sp_ref[...]
        heads, diag = _head_masks(tq), _diag_mask(tq)
        qms = [jnp.where(h, q2, 0.0).astype(BF16) for h in heads]
        doms = [jnp.where(h, do2, 0.0).astype(BF16) for h in heads]

        def rows_of(j):
            return pl.ds(pl.multiple_of(j * tq, tq), tq)

        def pass_one(js, carries, mask):
            carry = list(carries)
            kts, vts = [k_ref[rows_of(j), :] for j in js], [v_ref[rows_of(j), :] for j in js]
            z, d_a, lw, parts, a, dv = {}, {}, {}, {}, {}, {}

            def s_scores(c):
                z[c] = lax.dot_general(qms[c % 2], kts[c // 2], _NT, preferred_element_type=F32)
                d_a[c] = lax.dot_general(doms[c % 2], vts[c // 2], _NT, preferred_element_type=F32)

            def s_logs(c):
                lw[c] = _log_weights(z.pop(c), mask)

            def s_scan(c):
                parts[c] = _scan_parts(lw[c][0], ssuf)

            def s_weights(c):
                tail, carry[c % 2] = _chain_sums(parts.pop(c), carry[c % 2], True)
                log_beta = lw.pop(c)[1]
                av, beta = jnp.exp(log_beta + tail), jnp.exp(log_beta)
                if mask is not None:
                    av, beta = jnp.where(mask, av, 0.0), jnp.where(mask, beta, 0.0)
                g_s[c % 2, js[c // 2]] = av * d_a.pop(c)
                b_s[c % 2, js[c // 2]] = beta
                a[c] = av.astype(BF16)

            def s_values(c):
                dv[c] = lax.dot_general(a.pop(c), doms[c % 2], _TN, preferred_element_type=F32)
                if c % 2 == 1:
                    dv_ref[rows_of(js[c // 2]), :] += dv.pop(c - 1) + dv.pop(c)

            _pipeline(2 * len(js), [s_scores, s_logs, s_scan, s_weights, s_values])
            return tuple(carry)

        zero = jnp.zeros((tq, LANES), F32)
        rem = qi % ATT_UNROLL
        carries = pass_one([qi], (zero, zero), diag)
        carries = lax.fori_loop(0, rem, lambda p, c: pass_one([qi - 1 - p], c, None), carries)
        lax.fori_loop(0, qi // ATT_UNROLL,
                      lambda p, c: pass_one([qi - 1 - rem - ATT_UNROLL * p - u for u in range(ATT_UNROLL)], c, None), carries)

        def pass_two(js, state):
            prefix, dq = [state[0], state[2]], [state[1], state[3]]
            kts = [k_ref[rows_of(j), :] for j in js]
            gb, parts, dz, dk = {}, {}, {}, {}

            def s_scan(c):
                gb[c] = (g_s[c % 2, js[c // 2]], b_s[c % 2, js[c // 2]])
                parts[c] = _scan_parts(gb[c][0], spre)

            def s_dz(c):
                before, prefix[c % 2] = _chain_sums(parts.pop(c), prefix[c % 2], False)
                g, beta = gb.pop(c)
                dz[c] = (g - beta * (g + before)).astype(BF16)

            def s_grads(c):
                dzc = dz.pop(c)
                dq[c % 2] = dq[c % 2] + jnp.dot(dzc, kts[c // 2], preferred_element_type=F32)
                dk[c] = lax.dot_general(dzc, qms[c % 2], _TN, preferred_element_type=F32)
                if c % 2 == 1:
                    dk_ref[rows_of(js[c // 2]), :] += dk.pop(c - 1) + dk.pop(c)

            _pipeline(2 * len(js), [s_scan, s_dz, s_grads])
            return (prefix[0], dq[0], prefix[1], dq[1])

        zq = jnp.zeros((tq, HEAD_PAIR), F32)
        first = (qi + 1) % ATT_UNROLL
        out = lax.fori_loop(0, first, lambda p, c: pass_two([p], c), (zero, zq, zero, zq))
        out = lax.fori_loop(0, (qi + 1) // ATT_UNROLL,
                            lambda p, c: pass_two([first + ATT_UNROLL * p + u for u in range(ATT_UNROLL)], c), out)
        dq_ref[...] = jnp.where(heads[0], out[1], out[3])

    blk = pl.BlockSpec((tq, HEAD_PAIR), lambda h, i: (i, h))
    col_spec = pl.BlockSpec((t, HEAD_PAIR), lambda h, i: (0, h))
    const = pl.BlockSpec(scan_suffix.shape, lambda h, i: (0, 0))
    full = jax.ShapeDtypeStruct((t, d), F32)
    return pl.pallas_call(
        body, name=name, grid=(hp, nq), in_specs=[blk, col_spec, col_spec, blk, const, const],
        out_specs=[blk, col_spec, col_spec], out_shape=[full, full, full],
        scratch_shapes=[pltpu.VMEM((2, nq, tq, tq), F32), pltpu.VMEM((2, nq, tq, tq), F32)],
        compiler_params=_params(2))(qn, kn, vb, dob, scan_suffix, scan_prefix)


def _chip_at(x, y, j):
    return (1 - x if j & 2 else x, 1 - y if j & 1 else y)


def _all_gather(shards, name):
    n = len(shards)

    def body(*refs):
        x_refs, out_refs = refs[:n], refs[n:2 * n]
        send_sems, recv_sems, local_sems = refs[2 * n:]
        x, y, c = lax.axis_index("x"), lax.axis_index("y"), lax.axis_index("c")
        me, sibling = (x, y, c), (x, y, 1 - c)
        chips = [_chip_at(x, y, j) for j in (1, 2, 3)]

        def rows(t, px, py, pc):
            return out_refs[t].at[4 * px + 2 * py + pc]

        def copy(t, k, block, to, src=None):
            return pltpu.make_async_remote_copy(src_ref=rows(t, *block) if src is None else src, dst_ref=rows(t, *block),
                                                send_sem=send_sems.at[t, k], recv_sem=recv_sems.at[t, k],
                                                device_id=to, device_id_type=MESH)

        mine = [pltpu.make_async_copy(x_refs[t], rows(t, *me), local_sems.at[t]) for t in range(n)]
        for cp in mine:
            cp.start()
        first = []
        for t in range(n):
            first.append(copy(t, 0, me, sibling, src=x_refs[t]))
            first += [copy(t, 1 + j, me, (*chip, c), src=x_refs[t]) for j, chip in enumerate(chips)]
        for cp in first:
            cp.start()
        passed = []
        for t in range(n):
            for j, chip in enumerate(chips):
                copy(t, 1 + j, (*chip, c), me).wait_recv()
                passed.append(copy(t, 4 + j, (*chip, c), sibling))
                passed[-1].start()
        for t in range(n):
            copy(t, 0, sibling, me).wait_recv()
            for j, chip in enumerate(chips):
                copy(t, 4 + j, (*chip, 1 - c), me).wait_recv()
        for cp in first + passed:
            cp.wait_send()
        for cp in mine:
            cp.wait()

    any_spec = pl.BlockSpec(memory_space=pl.ANY)
    return pl.pallas_call(
        body, name=name, out_shape=[jax.ShapeDtypeStruct((N_DEV,) + a.shape, a.dtype) for a in shards],
        in_specs=[any_spec] * n, out_specs=[any_spec] * n,
        scratch_shapes=[pltpu.SemaphoreType.DMA((n, 7)), pltpu.SemaphoreType.DMA((n, 7)), pltpu.SemaphoreType.DMA((n,))])(*shards)


def _exchange_sibling(tensors, name):
    n = len(tensors)

    def body(*refs):
        g_refs, recv_refs = refs[:n], refs[n:2 * n]
        send_sems, recv_sems = refs[2 * n:]
        x, y, c = lax.axis_index("x"), lax.axis_index("y"), lax.axis_index("c")
        copies = []
        for t in range(n):
            for j in range(4):
                cx, cy = _chip_at(x, y, j)
                copies.append(pltpu.make_async_remote_copy(
                    src_ref=g_refs[t].at[4 * cx + 2 * cy + (1 - c)], dst_ref=recv_refs[t].at[j], send_sem=send_sems.at[t, j],
                    recv_sem=recv_sems.at[t, j], device_id=(x, y, 1 - c), device_id_type=MESH))
        for cp in copies:
            cp.start()
        for cp in copies:
            cp.wait_recv()
        for cp in copies:
            cp.wait_send()

    any_spec = pl.BlockSpec(memory_space=pl.ANY)
    return pl.pallas_call(
        body, name=name, out_shape=[jax.ShapeDtypeStruct((4,) + a.shape[1:], a.dtype) for a in tensors],
        in_specs=[any_spec] * n, out_specs=[any_spec] * n,
        scratch_shapes=[pltpu.SemaphoreType.DMA((n, 4)), pltpu.SemaphoreType.DMA((n, 4))])(*tensors)


def _exchange_chips(parts, name):
    n = len(parts)

    def body(*refs):
        p_refs, recv_refs = refs[:n], refs[n:2 * n]
        send_sems, recv_sems = refs[2 * n:]
        x, y, c = lax.axis_index("x"), lax.axis_index("y"), lax.axis_index("c")
        copies = []
        for t in range(n):
            for j in (1, 2, 3):
                copies.append(pltpu.make_async_remote_copy(
                    src_ref=p_refs[t].at[j], dst_ref=recv_refs[t].at[j - 1], send_sem=send_sems.at[t, j - 1],
                    recv_sem=recv_sems.at[t, j - 1], device_id=(*_chip_at(x, y, j), c), device_id_type=MESH))
        for cp in copies:
            cp.start()
        for cp in copies:
            cp.wait_recv()
        for cp in copies:
            cp.wait_send()

    any_spec = pl.BlockSpec(memory_space=pl.ANY)
    return pl.pallas_call(
        body, name=name, out_shape=[jax.ShapeDtypeStruct((3,) + a.shape[1:], a.dtype) for a in parts],
        in_specs=[any_spec] * n, out_specs=[any_spec] * n,
        scratch_shapes=[pltpu.SemaphoreType.DMA((n, 3)), pltpu.SemaphoreType.DMA((n, 3))])(*parts)


def _add_sibling(g8, recv1, name):
    _, a, b = g8.shape
    ta = next(cand for cand in (512, 384, 256, 128) if a % cand == 0)

    def body(g_ref, r_ref, o_ref):
        o_ref[...] = (g_ref[...].astype(F32) + r_ref[...].astype(F32)).astype(o_ref.dtype)

    def own(j, i):
        x, y, c = lax.axis_index("x"), lax.axis_index("y"), lax.axis_index("c")
        return (4 * (x ^ (j >> 1)) + 2 * (y ^ (j & 1)) + c, i, 0)

    slot = pl.BlockSpec((None, ta, b), lambda j, i: (j, i, 0))
    return pl.pallas_call(body, name=name, grid=(4, a // ta), in_specs=[pl.BlockSpec((None, ta, b), own), slot],
                          out_specs=slot, out_shape=jax.ShapeDtypeStruct((4, a, b), g8.dtype),
                          compiler_params=_params(2))(g8, recv1)


def _ada_forward(c_all, ada_w, ada_b_cols):
    depth, d, cols = ada_w.shape

    def body(c_ref, w_ref, b_ref, o_ref):
        cv = c_ref[...]
        act = cv * _sigmoid(cv)
        o_ref[...] = jnp.dot(act, w_ref[...], preferred_element_type=F32, precision=lax.Precision.HIGHEST) + b_ref[...]

    return pl.pallas_call(
        body, name="ada_forward", grid=(depth,),
        in_specs=[pl.BlockSpec((N_DEV, d), lambda l: (0, 0)), pl.BlockSpec((None, d, cols), lambda l: (l, 0, 0)),
                  pl.BlockSpec((None, 1, cols), lambda l: (l, 0, 0))],
        out_specs=pl.BlockSpec((None, N_DEV, cols), lambda l: (l, 0, 0)),
        out_shape=jax.ShapeDtypeStruct((depth, N_DEV, cols), F32), compiler_params=_params(1))(
            c_all, ada_w, ada_b_cols.reshape(depth, 1, cols))


def _ada_backward(c_all, dmod_cols):
    depth, _, cols = dmod_cols.shape
    d = c_all.shape[1]

    def body(c_ref, g_ref, o_ref):
        cv = c_ref[...]
        act = cv * _sigmoid(cv)
        o_ref[...] = lax.dot_general(act, g_ref[...], _TN, preferred_element_type=F32, precision=lax.Precision.HIGHEST)

    return pl.pallas_call(
        body, name="ada_backward", grid=(depth,),
        in_specs=[pl.BlockSpec((N_DEV, d), lambda l: (0, 0)), pl.BlockSpec((None, N_DEV, cols), lambda l: (l, 0, 0))],
        out_specs=pl.BlockSpec((None, d, cols), lambda l: (l, 0, 0)),
        out_shape=jax.ShapeDtypeStruct((depth, d, cols), F32), compiler_params=_params(1))(c_all, dmod_cols)


def _device_sum(a):
    _, r, w = a.shape

    def body(a_ref, o_ref):
        acc = a_ref[0]
        for dev in range(1, N_DEV):
            acc = acc + a_ref[dev]
        o_ref[...] = acc

    return pl.pallas_call(body, name="device_sum", out_shape=jax.ShapeDtypeStruct((r, w), F32),
                          in_specs=[pl.BlockSpec(memory_space=pltpu.VMEM)], out_specs=pl.BlockSpec(memory_space=pltpu.VMEM))(a)


def _adamw(w, g_layers, m, v, name):
    shape = w.shape
    cols = shape[-1]
    rows = int(np.prod(shape[:-1]))
    layers = len(g_layers)
    per_layer = rows // layers
    tile = next((cand for cand in (512, 352, 256, 128) if per_layer % cand == 0), per_layer)
    n_l = per_layer // tile
    n_parts = len(g_layers[0])

    def fn(pid, tv, hv, bv):
        wv, mv, vv = tv[:3]
        gv = None
        for l in range(layers):
            gl = None
            for part in tv[3 + l * n_parts:3 + (l + 1) * n_parts]:
                gl = part.astype(F32) if gl is None else gl + part.astype(F32)
            gv = gl if gv is None else jnp.where(pid >= l * n_l, gl, gv)
        mn = ADAM_B1 * mv + (1.0 - ADAM_B1) * gv
        vn = ADAM_B2 * vv + (1.0 - ADAM_B2) * (gv * gv)
        m_hat = mn / (1.0 - ADAM_B1 ** ADAM_STEP)
        v_hat = vn / (1.0 - ADAM_B2 ** ADAM_STEP)
        delta = -ADAM_LR * (m_hat / (jnp.sqrt(v_hat) + ADAM_EPS) + ADAM_WD * wv)
        return [gv, delta, mn, vn], []

    tiled = [(a.reshape(rows, cols), 0, cols) for a in (w, m, v)]
    for l, parts in enumerate(g_layers):
        clamp = functools.partial(lambda i, l: jnp.clip(i - l * n_l, 0, n_l - 1), l=l)
        tiled += [(p.reshape(per_layer, cols), 0, cols, clamp) for p in parts]
    res = _rowwise(fn, name, rows=rows, tile=tile, tiled=tiled, outs=[(cols, F32)] * 4)
    return [r.reshape(shape) for r in res]


def _norm_modulate(x, y, gate, ln_g, scale, shift, name, tile):
    t, d = x.shape

    def fn(pid, tv, hv, bv):
        if y is None:
            xn = tv[0]
            g_ln, sc, sh = bv
        else:
            g_gate, g_ln, sc, sh = bv
            xn = tv[0] + g_gate * tv[1]
        r = lax.rsqrt(jnp.mean(xn * xn, axis=-1, keepdims=True) + EPS)
        h = (xn * r * g_ln) * (1.0 + sc) + sh
        return ([h] if y is None else [xn, h]), []

    if y is None:
        h, = _rowwise(fn, name, rows=t, tile=tile, tiled=[(x, 0, d)], bcast=[ln_g, scale, shift], outs=[(d, BF16)])
        return x, h
    xn, h = _rowwise(fn, name, rows=t, tile=tile, tiled=[(x, 0, d), (y, 0, d)], bcast=[gate, ln_g, scale, shift],
                     outs=[(d, F32), (d, BF16)])
    return xn, h


def _norm_backward(dh, x, dres, ln_g, scale, name, tile):
    t, d = x.shape

    def fn(pid, tv, hv, bv):
        dhv, xv, dr = tv
        g_ln, sc = bv
        r = lax.rsqrt(jnp.mean(xv * xv, axis=-1, keepdims=True) + EPS)
        xn = xv * r
        dxn = dhv * (1.0 + sc) * g_ln
        dx = dr + r * (dxn - xn * jnp.mean(dxn * xn, axis=-1, keepdims=True))
        return [dx], [_colsum(dhv), _colsum(dhv * (xn * g_ln)), _colsum(dhv * (1.0 + sc) * xn)]

    return _rowwise(fn, name, rows=t, tile=tile, tiled=[(dh, 0, d), (x, 0, d), (dres, 0, d)], bcast=[ln_g, scale],
                    outs=[(d, F32)], accs=[(1, d)] * 3)


def _shift_rows(u, halo, k, pid, first_tile_zero):
    rows = lax.broadcasted_iota(jnp.int32, u.shape, 0)
    halo = halo * jnp.where(pid == 0, 0.0, 1.0) if first_tile_zero else halo
    out = pltpu.roll(u, k, axis=0)
    for j in range(k):
        out = jnp.where(rows == j, halo[8 - k + j:8 - k + j + 1, :], out)
    return out


def _shift_rows_up(u, halo, k, pid, n_tiles):
    tile = u.shape[0]
    rows = lax.broadcasted_iota(jnp.int32, u.shape, 0)
    halo = halo * jnp.where(pid == n_tiles - 1, 0.0, 1.0)
    out = pltpu.roll(u, tile - k, axis=0)
    for j in range(k):
        out = jnp.where(rows == tile - k + j, halo[j:j + 1, :], out)
    return out


def _layer_forward(x_in, y_prev, gate_prev, mod, w, l, tile):
    sh1, sc1, g1, sh2, sc2, g2 = mod
    t, d = x_in.shape
    f = w["w_g"].shape[1]
    bd2 = _head_sum_matrix()
    s = {}
    s["x"], s["h"] = _norm_modulate(x_in, y_prev, gate_prev, w["ln1"], sc1, sh1, f"l{l}_norm1", tile)
    p = _matmul([(s["h"], 0, w["w_in"], 0)], "nn", m=t, n=8 * d, k=d, tm=1024, tn=512, out_dtype=F32, name=f"l{l}_in_proj")
    s["p"] = p

    def qk_norm(pid, tv, hv, bv):
        qr, kr, vr = tv
        qg, kg, bd = bv

        def nrm(xv, g):
            r = lax.rsqrt(_head_sums(xv * xv, bd) * (1.0 / HEAD_DIM) + EPS)
            return xv * r * g
        return [nrm(qr, qg) * 0.125, nrm(kr, kg), vr], []

    s["qn"], s["kn"], s["vb"] = _rowwise(qk_norm, f"l{l}_qk_norm", rows=t, tile=tile, tiled=[(p, 0, d), (p, 1, d), (p, 2, d)],
                                         bcast=[w["qg"], w["kg"], bd2], outs=[(d, BF16)] * 3)
    s["ya"] = _attention_fwd(s["qn"], s["kn"], s["vb"], f"l{l}_attention")

    def conv_fwd(pid, tv, hv, bv):
        cb, cc, cx = tv
        u = cc * cx
        hu = hv[0] * hv[1]
        cw = bv[0]
        conv = cw[0:1, :] * _shift_rows(u, hu, 2, pid, True) + cw[1:2, :] * _shift_rows(u, hu, 1, pid, True) + cw[2:3, :] * u
        return [cb * conv], []

    s["yb"], = _rowwise(conv_fwd, f"l{l}_conv", rows=t, tile=tile, tiled=[(p, 3, d), (p, 4, d), (p, 5, d)],
                        halos=[(p, 4, d, "prev"), (p, 5, d, "prev")], bcast=[w["conv"]], outs=[(d, BF16)])
    s["a"] = _matmul([(s["ya"], 0, w["w_a"], 0)], "nn", m=t, n=d, k=d, tm=1024, tn=512, out_dtype=F32, name=f"l{l}_branch_a")
    s["b"] = _matmul([(s["yb"], 0, w["w_b"], 0)], "nn", m=t, n=d, k=d, tm=1024, tn=512, out_dtype=F32, name=f"l{l}_branch_b")

    def merge(pid, tv, hv, bv):
        av, bvv, ga, gb = tv
        return [_sigmoid(ga) * av + _sigmoid(gb) * bvv], []

    s["merged"], = _rowwise(merge, f"l{l}_merge", rows=t, tile=tile, tiled=[(s["a"], 0, d), (s["b"], 0, d), (p, 6, d), (p, 7, d)],
                            outs=[(d, BF16)])
    s["mo"] = _matmul([(s["merged"], 0, w["w_o"], 0)], "nn", m=t, n=d, k=d, tm=1024, tn=512, out_dtype=F32, name=f"l{l}_out_proj")
    s["x2"], s["h2"] = _norm_modulate(s["x"], s["mo"], g1, w["ln2"], sc2, sh2, f"l{l}_norm2", tile)
    fn_tile = f // 2
    s["g"] = _matmul([(s["h2"], 0, w["w_g"], 0)], "nn", m=t, n=f, k=d, tm=512, tn=fn_tile, out_dtype=F32, name=f"l{l}_ffn_gate")
    s["u"] = _matmul([(s["h2"], 0, w["w_u"], 0)], "nn", m=t, n=f, k=d, tm=512, tn=fn_tile, out_dtype=F32, name=f"l{l}_ffn_up")

    def swiglu(pid, tv, hv, bv):
        gv, uv = tv
        return [gv * _sigmoid(gv) * uv], []

    s["s"], = _rowwise(swiglu, f"l{l}_swiglu", rows=t, tile=tile // 2, tiled=[(s["g"], 0, f), (s["u"], 0, f)], outs=[(f, BF16)])
    s["f"] = _matmul([(s["s"], 0, w["w_d"], 0)], "nn", m=t, n=d, k=f, tm=512, tn=512, out_dtype=F32, name=f"l{l}_ffn_down")
    return s


def _layer_backward(dx3, s, mod, w, l, tile):
    sh1, sc1, g1, sh2, sc2, g2 = mod
    t, d = dx3.shape
    f = w["w_g"].shape[1]
    p = s["p"]
    bd2 = _head_sum_matrix()
    n_tiles = t // tile
    grads = {}

    def gate_bwd(pid, tv, hv, bv):
        return [tv[0] * bv[0]], [_colsum(tv[0] * tv[1])]

    df, dg2 = _rowwise(gate_bwd, f"l{l}_bwd_gate2", rows=t, tile=tile, tiled=[(dx3, 0, d), (s["f"], 0, d)], bcast=[g2],
                       outs=[(d, BF16)], accs=[(1, d)])
    grads["w_d"] = _matmul([(s["s"], 0, df, 0)], "tn", m=f, n=d, k=t, tm=512, tn=512, out_dtype=BF16, name=f"l{l}_dw_down")
    ds = _matmul([(df, 0, w["w_d"], 0)], "nt", m=t, n=f, k=d, tm=512, tn=f // 2, out_dtype=F32, name=f"l{l}_d_swiglu")

    def swiglu_bwd(pid, tv, hv, bv):
        dsv, gv, uv = tv
        sig = _sigmoid(gv)
        return [dsv * uv * (sig * (1.0 + gv * (1.0 - sig))), dsv * (gv * sig)], []

    dgt, dup = _rowwise(swiglu_bwd, f"l{l}_bwd_swiglu", rows=t, tile=tile // 2, tiled=[(ds, 0, f), (s["g"], 0, f), (s["u"], 0, f)],
                        outs=[(f, BF16)] * 2)
    grads["w_g"] = _matmul([(s["h2"], 0, dgt, 0)], "tn", m=d, n=f, k=t, tm=512, tn=f // 2, out_dtype=BF16, name=f"l{l}_dw_gate")
    grads["w_u"] = _matmul([(s["h2"], 0, dup, 0)], "tn", m=d, n=f, k=t, tm=512, tn=f // 2, out_dtype=BF16, name=f"l{l}_dw_up")
    dh2 = _matmul([(dgt, 0, w["w_g"], 0), (dup, 0, w["w_u"], 0)], "nt", m=t, n=d, k=f, tm=512, tn=512, out_dtype=F32,
                  name=f"l{l}_dh2")
    dx2, dsh2, dsc2, grads["ln2"] = _norm_backward(dh2, s["x2"], dx3, w["ln2"], sc2, f"l{l}_bwd_norm2", tile)

    dmo, dg1 = _rowwise(gate_bwd, f"l{l}_bwd_gate1", rows=t, tile=tile, tiled=[(dx2, 0, d), (s["mo"], 0, d)], bcast=[g1],
                        outs=[(d, BF16)], accs=[(1, d)])
    grads["w_o"] = _matmul([(s["merged"], 0, dmo, 0)], "tn", m=d, n=d, k=t, tm=512, tn=512, out_dtype=BF16, name=f"l{l}_dw_out")
    dmerged = _matmul([(dmo, 0, w["w_o"], 0)], "nt", m=t, n=d, k=d, tm=1024, tn=512, out_dtype=F32, name=f"l{l}_d_merged")

    def merge_bwd(pid, tv, hv, bv):
        dm, av, bvv, ga, gb = tv
        sa, sb = _sigmoid(ga), _sigmoid(gb)
        return [dm * sa, dm * sb, dm * av * (sa * (1.0 - sa)), dm * bvv * (sb * (1.0 - sb))], []

    d_a, d_b, dga, dgb = _rowwise(merge_bwd, f"l{l}_bwd_merge", rows=t, tile=tile,
                                  tiled=[(dmerged, 0, d), (s["a"], 0, d), (s["b"], 0, d), (p, 6, d), (p, 7, d)], outs=[(d, BF16)] * 4)
    grads["w_a"] = _matmul([(s["ya"], 0, d_a, 0)], "tn", m=d, n=d, k=t, tm=512, tn=512, out_dtype=BF16, name=f"l{l}_dw_a")
    grads["w_b"] = _matmul([(s["yb"], 0, d_b, 0)], "tn", m=d, n=d, k=t, tm=512, tn=512, out_dtype=BF16, name=f"l{l}_dw_b")
    dya = _matmul([(d_a, 0, w["w_a"], 0)], "nt", m=t, n=d, k=d, tm=1024, tn=512, out_dtype=BF16, name=f"l{l}_d_ya")
    dyb = _matmul([(d_b, 0, w["w_b"], 0)], "nt", m=t, n=d, k=d, tm=1024, tn=512, out_dtype=F32, name=f"l{l}_d_yb")

    def conv_bwd(pid, tv, hv, bv):
        dy, cb, cc, cx = tv
        cw = bv[0]
        u, hu = cc * cx, hv[0] * hv[1]
        u1, u2 = _shift_rows(u, hu, 1, pid, True), _shift_rows(u, hu, 2, pid, True)
        conv = cw[0:1, :] * u2 + cw[1:2, :] * u1 + cw[2:3, :] * u
        dconv, hd = dy * cb, hv[2] * hv[3]
        du = (cw[2:3, :] * dconv + cw[1:2, :] * _shift_rows_up(dconv, hd, 1, pid, n_tiles)
              + cw[0:1, :] * _shift_rows_up(dconv, hd, 2, pid, n_tiles))
        return [dy * conv, du * cx, du * cc], [_colsum(dconv * u2), _colsum(dconv * u1), _colsum(dconv * u)]

    dcb, dcc, dcx, dcw0, dcw1, dcw2 = _rowwise(
        conv_bwd, f"l{l}_bwd_conv", rows=t, tile=tile, tiled=[(dyb, 0, d), (p, 3, d), (p, 4, d), (p, 5, d)],
        halos=[(p, 4, d, "prev"), (p, 5, d, "prev"), (dyb, 0, d, "next"), (p, 3, d, "next")], bcast=[w["conv"]],
        outs=[(d, BF16)] * 3, accs=[(1, d)] * 3)
    grads["conv"] = jnp.concatenate([dcw0, dcw1, dcw2], axis=0)

    dqs, dkn, dv = _attention_bwd(s["qn"], s["kn"], s["vb"], dya, f"l{l}_bwd_attention")

    def qk_norm_bwd(pid, tv, hv, bv):
        dq, dk, qr, kr, dvv = tv
        qg, kg, bd = bv

        def bwd(dy, xv, g):
            r = lax.rsqrt(_head_sums(xv * xv, bd) * (1.0 / HEAD_DIM) + EPS)
            yv = xv * r
            dyn = dy * g
            dx = r * (dyn - yv * (_head_sums(dyn * yv, bd) * (1.0 / HEAD_DIM)))
            return dx, _colsum(dy * yv)

        dxq, dgq = bwd(dq * 0.125, qr, qg)
        dxk, dgk = bwd(dk, kr, kg)
        return [dxq, dxk, dvv], [dgq, dgk]

    dqr, dkr, dvb, grads["qg"], grads["kg"] = _rowwise(
        qk_norm_bwd, f"l{l}_bwd_qk_norm", rows=t, tile=tile, tiled=[(dqs, 0, d), (dkn, 0, d), (p, 0, d), (p, 1, d), (dv, 0, d)],
        bcast=[w["qg"], w["kg"], bd2], outs=[(d, BF16)] * 3, accs=[(1, d)] * 2)

    dp = [dqr, dkr, dvb, dcb, dcc, dcx, dga, dgb]
    grads["w_in"] = [_matmul([(s["h"], 0, dpk, 0)], "tn", m=d, n=d, k=t, tm=512, tn=512, out_dtype=BF16, name=f"l{l}_dw_in{k}")
                     for k, dpk in enumerate(dp)]
    dh = _matmul([(dpk, 0, w["w_in"], k) for k, dpk in enumerate(dp)], "nt", m=t, n=d, k=d, tm=512, tn=512, out_dtype=F32,
                 name=f"l{l}_dh")
    dx, dsh1, dsc1, grads["ln1"] = _norm_backward(dh, s["x"], dx2, w["ln1"], sc1, f"l{l}_bwd_norm1", tile)
    return dx, grads, [dsh1, dsc1, dg1, dsh2, dsc2, dg2]


_BIG = ["w_in", "w_branch_a", "w_branch_b", "w_out", "w_ffn_gate", "w_ffn_up", "w_ffn_down"]
_SHORT = dict(w_in="w_in", w_branch_a="w_a", w_branch_b="w_b", w_out="w_o", w_ffn_gate="w_g", w_ffn_up="w_u", w_ffn_down="w_d")
_COL_SHARDED = {"w_in", "w_ffn_gate", "w_ffn_up"}


def kernel(x, c, ada_w, ada_b, ln1_g, w_in, q_norm_g, k_norm_g, conv_w, w_branch_a, w_branch_b, w_out, ln2_g, w_ffn_gate, w_ffn_up, w_ffn_down, loss_target, m_ada_w, m_ada_b, m_ln1_g, m_w_in, m_q_norm_g, m_k_norm_g, m_conv_w, m_w_branch_a, m_w_branch_b, m_w_out, m_ln2_g, m_w_ffn_gate, m_w_ffn_up, m_w_ffn_down, v_ada_w, v_ada_b, v_ln1_g, v_w_in, v_q_norm_g, v_k_norm_g, v_conv_w, v_w_branch_a, v_w_branch_b, v_w_out, v_ln2_g, v_w_ffn_gate, v_w_ffn_up, v_w_ffn_down):
    weights = dict(ada_w=ada_w, ada_b=ada_b, ln1_g=ln1_g, w_in=w_in, q_norm_g=q_norm_g, k_norm_g=k_norm_g, conv_w=conv_w,
                   w_branch_a=w_branch_a, w_branch_b=w_branch_b, w_out=w_out, ln2_g=ln2_g, w_ffn_gate=w_ffn_gate,
                   w_ffn_up=w_ffn_up, w_ffn_down=w_ffn_down)
    m_in = dict(ada_w=m_ada_w, ada_b=m_ada_b, ln1_g=m_ln1_g, w_in=m_w_in, q_norm_g=m_q_norm_g, k_norm_g=m_k_norm_g,
                conv_w=m_conv_w, w_branch_a=m_w_branch_a, w_branch_b=m_w_branch_b, w_out=m_w_out, ln2_g=m_ln2_g,
                w_ffn_gate=m_w_ffn_gate, w_ffn_up=m_w_ffn_up, w_ffn_down=m_w_ffn_down)
    v_in = dict(ada_w=v_ada_w, ada_b=v_ada_b, ln1_g=v_ln1_g, w_in=v_w_in, q_norm_g=v_q_norm_g, k_norm_g=v_k_norm_g,
                conv_w=v_conv_w, w_branch_a=v_w_branch_a, w_branch_b=v_w_branch_b, w_out=v_w_out, ln2_g=v_ln2_g,
                w_ffn_gate=v_w_ffn_gate, w_ffn_up=v_w_ffn_up, w_ffn_down=v_w_ffn_down)
    names = list(weights)

    mx, my, mc = lax.axis_index("x"), lax.axis_index("y"), lax.axis_index("c")
    me = 4 * mx + 2 * my + mc
    xs, target = x[0], loss_target[0]
    t, d = xs.shape
    depth = ada_w.shape[0]
    mod_cols = ada_w.shape[2]
    conv_cols = conv_w.shape[2]
    row_w = 1024
    tile = 512 if t % 512 == 0 else t

    small = jnp.concatenate([c.reshape(-1), conv_w.reshape(-1)])
    small_n = -(-small.shape[0] // row_w) * row_w
    small = jnp.pad(small, (0, small_n - small.shape[0])).reshape(-1, row_w)
    small_all = _all_gather([small], "gather_cond")[0].reshape(N_DEV, -1)
    c_all = small_all[:, :d]
    conv_full = jnp.transpose(small_all[:, d:d + depth * 3 * conv_cols].reshape(N_DEV, depth, 3, conv_cols), (1, 2, 0, 3)
                              ).reshape(depth, 3, N_DEV * conv_cols)

    ada_b_cols = lax.dynamic_slice_in_dim(ada_b, me * mod_cols, mod_cols, axis=1)
    mod_part = _ada_forward(c_all, ada_w, ada_b_cols)
    mod_all = _all_gather([mod_part.reshape(-1, row_w)], "gather_mod")[0].reshape(N_DEV, depth, N_DEV, mod_cols)
    mod_mine = lax.dynamic_index_in_dim(mod_all, me, axis=2, keepdims=False)
    mod = jnp.transpose(mod_mine, (1, 0, 2)).reshape(depth, 6, 1, d)

    f_shard = w_ffn_gate.shape[2]
    f_pad = -(-f_shard // LANES) * LANES - f_shard
    pads = dict(w_ffn_gate=((0, 0), (0, 0), (0, f_pad)), w_ffn_up=((0, 0), (0, 0), (0, f_pad)), w_ffn_down=((0, 0), (0, f_pad), (0, 0)))
    shards = [weights[n].astype(BF16) for n in _BIG]
    shards = [jnp.pad(a, pads[n]) if n in pads else a for n, a in zip(_BIG, shards)]
    gathered = dict(zip(_BIG, _all_gather(shards, "gather_weights")))
    layer_w = []
    for l in range(depth):
        wl = {}
        for n in _BIG:
            al = gathered[n][:, l]
            wl[_SHORT[n]] = (jnp.transpose(al, (1, 0, 2)).reshape(al.shape[1], -1) if n in _COL_SHARDED
                             else al.reshape(-1, al.shape[2]))
        wl["ln1"], wl["ln2"] = ln1_g[l][None], ln2_g[l][None]
        wl["qg"] = jnp.tile(q_norm_g[l], d // HEAD_DIM)[None]
        wl["kg"] = jnp.tile(k_norm_g[l], d // HEAD_DIM)[None]
        wl["conv"] = conv_full[l]
        layer_w.append(wl)

    saved = []
    x_cur, y_prev, gate_prev = xs, None, None
    for l in range(depth):
        mods = [mod[l, k] for k in range(6)]
        s = _layer_forward(x_cur, y_prev, gate_prev, mods, layer_w[l], l, tile)
        saved.append(s)
        x_cur, y_prev, gate_prev = s["x2"], s["f"], mods[5]

    def loss_head(pid, tv, hv, bv):
        diff = tv[0] + bv[0] * tv[1] - tv[2]
        return [diff * (1.0 / d)], [_colsum(diff * diff) * (0.5 / d)]

    dx, loss_cols = _rowwise(loss_head, "loss_head", rows=t, tile=tile, tiled=[(x_cur, 0, d), (y_prev, 0, d), (target, 0, d)],
                             bcast=[gate_prev], outs=[(d, F32)], accs=[(1, d)])

    layer_g, dmods = [None] * depth, [None] * depth
    for l in reversed(range(depth)):
        mods = [mod[l, k] for k in range(6)]
        dx, layer_g[l], dmods[l] = _layer_backward(dx, saved[l], mods, layer_w[l], l, tile)
    grad_x = dx[None]

    pieces = [jnp.concatenate(dmods[l], axis=1) for l in range(depth)]
    for key in ("ln1", "ln2", "qg", "kg"):
        pieces += [layer_g[l][key] for l in range(depth)]
    pieces += [layer_g[l]["conv"].reshape(1, -1) for l in range(depth)]
    pieces.append(loss_cols)
    part_small = jnp.concatenate(pieces, axis=1).reshape(-1, row_w)
    part_all = _all_gather([part_small], "gather_small_grads")[0]
    summed = _device_sum(part_all).reshape(-1)
    n_mod = depth * 6 * d
    dmod_all = part_all.reshape(N_DEV, -1)[:, :n_mod].reshape(N_DEV, depth, 6 * d)
    dmod_cols = jnp.transpose(lax.dynamic_slice_in_dim(dmod_all, me * mod_cols, mod_cols, axis=2), (1, 0, 2))
    g = {"ada_w": _ada_backward(c_all, dmod_cols), "ada_b": summed[:n_mod].reshape(depth, 6 * d)}
    off = n_mod
    g["ln1_g"] = summed[off:off + depth * d].reshape(depth, d)
    g["ln2_g"] = summed[off + depth * d:off + 2 * depth * d].reshape(depth, d)
    g["q_norm_g"] = summed[off + 2 * depth * d:off + 3 * depth * d].reshape(depth, d // HEAD_DIM, HEAD_DIM).sum(axis=1)
    g["k_norm_g"] = summed[off + 3 * depth * d:off + 4 * depth * d].reshape(depth, d // HEAD_DIM, HEAD_DIM).sum(axis=1)
    off += 4 * depth * d
    conv_g = summed[off:off + depth * 3 * d].reshape(depth, 3, N_DEV, conv_cols)
    g["conv_w"] = lax.dynamic_index_in_dim(conv_g, me, axis=2, keepdims=False)
    off += depth * 3 * d
    loss = jnp.sum(summed[off:off + d])

    def by_owner(n, full):
        if n == "w_in":
            return jnp.stack(full)
        if n in _COL_SHARDED:
            return jnp.transpose(full.reshape(full.shape[0], N_DEV, -1), (1, 0, 2))
        return full.reshape(N_DEV, -1, full.shape[1])

    tensors = [by_owner(n, layer_g[l][_SHORT[n]]) for l in range(depth) for n in _BIG]
    recv1 = _exchange_sibling(tensors, "rs_exchange_sibling")
    parts = [_add_sibling(a, r, f"rs_add_sibling_{i}") for i, (a, r) in enumerate(zip(tensors, recv1))]
    recv2 = _exchange_chips(parts, "rs_exchange_chips")

    def shard_of(n, a):
        if n in ("w_ffn_gate", "w_ffn_up"):
            return a[:, :f_shard]
        return a[:f_shard] if n == "w_ffn_down" else a

    outs = {}
    for k, n in enumerate(_BIG):
        g_layers = [[shard_of(n, a) for a in (parts[l * len(_BIG) + k][0], *recv2[l * len(_BIG) + k])] for l in range(depth)]
        outs[n] = _adamw(weights[n], g_layers, m_in[n], v_in[n], f"adamw_{n}")
    for n in names:
        if n not in outs:
            outs[n] = _adamw(weights[n], [[g[n]]], m_in[n], v_in[n], f"adamw_{n}")
    return (loss, grad_x, *[outs[n][0] for n in names], *[outs[n][1] for n in names], *[outs[n][2] for n in names],
            *[outs[n][3] for n in names])
```

```python
import functools

import numpy as np
import jax
import jax.numpy as jnp
from jax import lax
from jax.experimental import pallas as pl
from jax.experimental.pallas import tpu as pltpu

F32, BF16 = jnp.float32, jnp.bfloat16
MESH = pl.DeviceIdType.MESH
N_DEV = 8
LANES = 128
HEAD_DIM = 64
HEAD_PAIR = 2 * HEAD_DIM
ATT_TILE = 256
ATT_UNROLL = 4
EPS = 1e-6
VMEM_LIMIT = 56 * 1024 * 1024

ADAM_LR, ADAM_B1, ADAM_B2, ADAM_EPS, ADAM_WD, ADAM_STEP = 0.001, 0.9, 0.999, 1e-08, 0.01, 10

_NT = (((1,), (1,)), ((), ()))
_TN = (((0,), (0,)), ((), ()))
_NN = (((1,), (0,)), ((), ()))


def _params(n_grid):
    return pltpu.CompilerParams(dimension_semantics=("arbitrary",) * n_grid, vmem_limit_bytes=VMEM_LIMIT)


def _sigmoid(x):
    return 1.0 / (1.0 + jnp.exp(-x))


def _rowwise(fn, name, *, rows, tile, tiled=(), halos=(), bcast=(), outs=(), accs=()):
    n = rows // tile
    assert n * tile == rows
    in_specs, args = [], []
    for t in tiled:
        arr, cb, w = t[:3]
        rowmap = t[3] if len(t) > 3 else (lambda i: i)
        in_specs.append(pl.BlockSpec((tile, w), functools.partial(lambda i, cb, rowmap: (rowmap(i), cb), cb=cb, rowmap=rowmap)))
        args.append(arr)
    per_tile8 = tile // 8
    for arr, cb, w, side in halos:
        last8 = arr.shape[0] // 8 - 1
        if side == "prev":
            imap = functools.partial(lambda i, cb: (jnp.maximum(i * per_tile8 - 1, 0), cb), cb=cb)
        else:
            imap = functools.partial(lambda i, cb, last8: (jnp.minimum((i + 1) * per_tile8, last8), cb), cb=cb, last8=last8)
        in_specs.append(pl.BlockSpec((8, w), imap))
        args.append(arr)
    for arr in bcast:
        in_specs.append(pl.BlockSpec(arr.shape, functools.partial(lambda i, nd: (0,) * nd, nd=arr.ndim)))
        args.append(arr)
    out_shape = [jax.ShapeDtypeStruct((rows, w), dt) for w, dt in outs] + [jax.ShapeDtypeStruct(s, F32) for s in accs]
    out_specs = [pl.BlockSpec((tile, w), lambda i: (i, 0)) for w, _ in outs] + [pl.BlockSpec(s, lambda i: (0, 0)) for s in accs]
    nt, nh, nb, no, na = len(tiled), len(halos), len(bcast), len(outs), len(accs)

    def body(*refs):
        pid = pl.program_id(0)
        tv = [r[...] for r in refs[:nt]]
        hv = [r[...] for r in refs[nt:nt + nh]]
        bv = [r[...] for r in refs[nt + nh:nt + nh + nb]]
        out_refs = refs[nt + nh + nb:nt + nh + nb + no]
        acc_refs = refs[nt + nh + nb + no:]
        ov, av = fn(pid, tv, hv, bv)
        for r, v in zip(out_refs, ov):
            r[...] = v.astype(r.dtype)
        if na:
            @pl.when(pid == 0)
            def _():
                for r in acc_refs:
                    r[...] = jnp.zeros(r.shape, F32)
            for r, v in zip(acc_refs, av):
                r[...] += v

    res = pl.pallas_call(body, name=name, grid=(n,), in_specs=in_specs, out_specs=out_specs, out_shape=out_shape,
                         compiler_params=_params(1))(*args)
    return res


def _colsum(v):
    return jnp.sum(v, axis=0, keepdims=True)


def _matmul(pairs, mode, *, m, n, k, tm, tn, out_dtype, name):
    tm, tn = min(tm, m), min(tn, n)
    assert m % tm == 0 and n % tn == 0
    in_specs, args = [], []
    for a, acb, b, bcb in pairs:
        if mode == "tn":
            in_specs.append(pl.BlockSpec((k, tm), functools.partial(lambda i, j, o: (0, o + i), o=acb)))
            in_specs.append(pl.BlockSpec((k, tn), functools.partial(lambda i, j, o: (0, o + j), o=bcb)))
        elif mode == "nn":
            in_specs.append(pl.BlockSpec((tm, k), functools.partial(lambda i, j, o: (i, o), o=acb)))
            in_specs.append(pl.BlockSpec((k, tn), functools.partial(lambda i, j, o: (0, o + j), o=bcb)))
        else:
            in_specs.append(pl.BlockSpec((tm, k), functools.partial(lambda i, j, o: (i, o), o=acb)))
            in_specs.append(pl.BlockSpec((tn, k), functools.partial(lambda i, j, o: (j, o), o=bcb)))
        args += [a, b]
    dims = {"nn": _NN, "nt": _NT, "tn": _TN}[mode]
    npairs = len(pairs)

    def body(*refs):
        o_ref = refs[2 * npairs]
        acc = None
        for p in range(npairs):
            d = lax.dot_general(refs[2 * p][...].astype(BF16), refs[2 * p + 1][...].astype(BF16), dims,
                                preferred_element_type=F32)
            acc = d if acc is None else acc + d
        o_ref[...] = acc.astype(o_ref.dtype)

    return pl.pallas_call(body, name=name, grid=(m // tm, n // tn), in_specs=in_specs,
                          out_specs=pl.BlockSpec((tm, tn), lambda i, j: (i, j)),
                          out_shape=jax.ShapeDtypeStruct((m, n), out_dtype), compiler_params=_params(2))(*args)


def _scan_matrix(kind, n):
    r = np.arange(n)
    tri = (r[:, None] > r[None, :]) if kind == "suffix" else (r[:, None] < r[None, :])
    half = np.concatenate([tri.astype(np.float32), np.ones((n, LANES), np.float32)], axis=1)
    return jnp.asarray(np.concatenate([half, half], axis=0), BF16)


def _head_sum_matrix():
    r = np.arange(LANES)
    bd = (r[:, None] // HEAD_DIM == r[None, :] // HEAD_DIM).astype(np.float32)
    return jnp.asarray(np.concatenate([bd, bd], axis=0), BF16)


def _split_cat(v):
    hi = v.astype(BF16)
    lo = (v - hi.astype(F32)).astype(BF16)
    return jnp.concatenate([hi, lo], axis=1)


def _head_sums(v, bd2):
    hi = v.astype(BF16)
    lo = (v - hi.astype(F32)).astype(BF16)
    parts = []
    for g in range(v.shape[1] // LANES):
        sl = slice(g * LANES, (g + 1) * LANES)
        parts.append(jnp.dot(jnp.concatenate([hi[:, sl], lo[:, sl]], axis=1), bd2, preferred_element_type=F32))
    return jnp.concatenate(parts, axis=1)


def _scan_parts(v, scan):
    return [jnp.dot(_split_cat(v[:, b * LANES:(b + 1) * LANES]), scan, preferred_element_type=F32)
            for b in range(v.shape[1] // LANES)]


def _chain_sums(parts, carry, reverse):
    nb = len(parts)
    outs = [None] * nb
    for b in (reversed(range(nb)) if reverse else range(nb)):
        outs[b] = carry + parts[b][:, :LANES]
        carry = carry + parts[b][:, LANES:]
    return jnp.concatenate(outs, axis=1), carry


def _log_weights(z, mask):
    sp = jnp.log(1.0 + jnp.exp(-jnp.abs(z)))
    log_not = -(jnp.maximum(z, 0.0) + sp)
    log_beta = z + log_not
    return (log_not if mask is None else jnp.where(mask, log_not, 0.0)), log_beta


def _pipeline(n_chains, stages):
    for step in range(n_chains + len(stages) - 1):
        for s, stage in enumerate(stages):
            if 0 <= step - s < n_chains:
                stage(step - s)


def _head_masks(tq):
    lane = lax.broadcasted_iota(jnp.int32, (tq, HEAD_PAIR), 1)
    return [(lane // HEAD_DIM) == hh for hh in range(2)]


def _diag_mask(tq):
    return lax.broadcasted_iota(jnp.int32, (tq, tq), 1) < lax.broadcasted_iota(jnp.int32, (tq, tq), 0)


def _attention_fwd(qn, kn, vb, name):
    t, d = qn.shape
    tq = min(ATT_TILE, t)
    nq, hp = t // tq, d // HEAD_PAIR
    scan_suffix = _scan_matrix("suffix", LANES)

    def body(q_ref, k_ref, v_ref, sc_ref, o_ref):
        qi = pl.program_id(1)
        q2 = q_ref[...].astype(F32)
        scan = sc_ref[...]
        heads, diag = _head_masks(tq), _diag_mask(tq)
        qms = [jnp.where(h, q2, 0.0).astype(BF16) for h in heads]

        def rows_of(j):
            return pl.ds(pl.multiple_of(j * tq, tq), tq)

        def step(js, state, mask):
            carry, acc = [state[0], state[2]], [state[1], state[3]]
            kts, vts = [k_ref[rows_of(j), :] for j in js], [v_ref[rows_of(j), :] for j in js]
            z, lw, parts, a = {}, {}, {}, {}

            def s_scores(c):
                z[c] = lax.dot_general(qms[c % 2], kts[c // 2], _NT, preferred_element_type=F32)

            def s_logs(c):
                lw[c] = _log_weights(z.pop(c), mask)

            def s_scan(c):
                parts[c] = _scan_parts(lw[c][0], scan)

            def s_weights(c):
                tail, carry[c % 2] = _chain_sums(parts.pop(c), carry[c % 2], True)
                av = jnp.exp(lw.pop(c)[1] + tail)
                a[c] = (av if mask is None else jnp.where(mask, av, 0.0)).astype(BF16)

            def s_values(c):
                acc[c % 2] = acc[c % 2] + jnp.dot(a.pop(c), vts[c // 2], preferred_element_type=F32)

            _pipeline(2 * len(js), [s_scores, s_logs, s_scan, s_weights, s_values])
            return (carry[0], acc[0], carry[1], acc[1])

        zero, zq = jnp.zeros((tq, LANES), F32), jnp.zeros((tq, HEAD_PAIR), F32)
        state = step([qi], (zero, zq, zero, zq), diag)
        rem = qi % ATT_UNROLL
        state = lax.fori_loop(0, rem, lambda p, c: step([qi - 1 - p], c, None), state)
        state = lax.fori_loop(0, qi // ATT_UNROLL,
                              lambda p, c: step([qi - 1 - rem - ATT_UNROLL * p - u for u in range(ATT_UNROLL)], c, None), state)
        o_ref[...] = jnp.where(heads[0], state[1], state[3]).astype(o_ref.dtype)

    return pl.pallas_call(
        body, name=name, grid=(hp, nq),
        in_specs=[pl.BlockSpec((tq, HEAD_PAIR), lambda h, i: (i, h)),
                  pl.BlockSpec((t, HEAD_PAIR), lambda h, i: (0, h)),
                  pl.BlockSpec((t, HEAD_PAIR), lambda h, i: (0, h)),
                  pl.BlockSpec(scan_suffix.shape, lambda h, i: (0, 0))],
        out_specs=pl.BlockSpec((tq, HEAD_PAIR), lambda h, i: (i, h)),
        out_shape=jax.ShapeDtypeStruct((t, d), BF16), compiler_params=_params(2))(qn, kn, vb, scan_suffix)


def _attention_bwd(qn, kn, vb, dob, name):
    t, d = qn.shape
    tq = min(ATT_TILE, t)
    nq, hp = t // tq, d // HEAD_PAIR
    scan_suffix, scan_prefix = _scan_matrix("suffix", LANES), _scan_matrix("prefix", LANES)

    def body(q_ref, k_ref, v_ref, do_ref, ss_ref, sp_ref, dq_ref, dk_ref, dv_ref, g_s, b_s):
        qi = pl.program_id(1)

        @pl.when(qi == 0)
        def _():
            dk_ref[...] = jnp.zeros(dk_ref.shape, F32)
            dv_ref[...] = jnp.zeros(dv_ref.shape, F32)

        q2, do2 = q_ref[...].astype(F32), do_ref[...].astype(F32)
        ssuf, spre = ss_ref[...], sp_ref[...]
        heads, diag = _head_masks(tq), _diag_mask(tq)
        qms = [jnp.where(h, q2, 0.0).astype(BF16) for h in heads]
        doms = [jnp.where(h, do2, 0.0).astype(BF16) for h in heads]

        def rows_of(j):
            return pl.ds(pl.multiple_of(j * tq, tq), tq)

        def pass_one(js, carries, mask):
            carry = list(carries)
            kts, vts = [k_ref[rows_of(j), :] for j in js], [v_ref[rows_of(j), :] for j in js]
            z, d_a, lw, parts, a, dv = {}, {}, {}, {}, {}, {}

            def s_scores(c):
                z[c] = lax.dot_general(qms[c % 2], kts[c // 2], _NT, preferred_element_type=F32)
                d_a[c] = lax.dot_general(doms[c % 2], vts[c // 2], _NT, preferred_element_type=F32)

            def s_logs(c):
                lw[c] = _log_weights(z.pop(c), mask)

            def s_scan(c):
                parts[c] = _scan_parts(lw[c][0], ssuf)

            def s_weights(c):
                tail, carry[c % 2] = _chain_sums(parts.pop(c), carry[c % 2], True)
                log_beta = lw.pop(c)[1]
                av, beta = jnp.exp(log_beta + tail), jnp.exp(log_beta)
                if mask is not None:
                    av, beta = jnp.where(mask, av, 0.0), jnp.where(mask, beta, 0.0)
                g_s[c % 2, js[c // 2]] = av * d_a.pop(c)
                b_s[c % 2, js[c // 2]] = beta
                a[c] = av.astype(BF16)

            def s_values(c):
                dv[c] = lax.dot_general(a.pop(c), doms[c % 2], _TN, preferred_element_type=F32)
                if c % 2 == 1:
                    dv_ref[rows_of(js[c // 2]), :] += dv.pop(c - 1) + dv.pop(c)

            _pipeline(2 * len(js), [s_scores, s_logs, s_scan, s_weights, s_values])
            return tuple(carry)

        zero = jnp.zeros((tq, LANES), F32)
        rem = qi % ATT_UNROLL
        carries = pass_one([qi], (zero, zero), diag)
        carries = lax.fori_loop(0, rem, lambda p, c: pass_one([qi - 1 - p], c, None), carries)
        lax.fori_loop(0, qi // ATT_UNROLL,
                      lambda p, c: pass_one([qi - 1 - rem - ATT_UNROLL * p - u for u in range(ATT_UNROLL)], c, None), carries)

        def pass_two(js, state):
            prefix, dq = [state[0], state[2]], [state[1], state[3]]
            kts = [k_ref[rows_of(j), :] for j in js]
            gb, parts, dz, dk = {}, {}, {}, {}

            def s_scan(c):
                gb[c] = (g_s[c % 2, js[c // 2]], b_s[c % 2, js[c // 2]])
                parts[c] = _scan_parts(gb[c][0], spre)

            def s_dz(c):
                before, prefix[c % 2] = _chain_sums(parts.pop(c), prefix[c % 2], False)
                g, beta = gb.pop(c)
                dz[c] = (g - beta * (g + before)).astype(BF16)

            def s_grads(c):
                dzc = dz.pop(c)
                dq[c % 2] = dq[c % 2] + jnp.dot(dzc, kts[c // 2], preferred_element_type=F32)
                dk[c] = lax.dot_general(dzc, qms[c % 2], _TN, preferred_element_type=F32)
                if c % 2 == 1:
                    dk_ref[rows_of(js[c // 2]), :] += dk.pop(c - 1) + dk.pop(c)

            _pipeline(2 * len(js), [s_scan, s_dz, s_grads])
            return (prefix[0], dq[0], prefix[1], dq[1])

        zq = jnp.zeros((tq, HEAD_PAIR), F32)
        first = (qi + 1) % ATT_UNROLL
        out = lax.fori_loop(0, first, lambda p, c: pass_two([p], c), (zero, zq, zero, zq))
        out = lax.fori_loop(0, (qi + 1) // ATT_UNROLL,
                            lambda p, c: pass_two([first + ATT_UNROLL * p + u for u in range(ATT_UNROLL)], c), out)
        dq_ref[...] = jnp.where(heads[0], out[1], out[3])

    blk = pl.BlockSpec((tq, HEAD_PAIR), lambda h, i: (i, h))
    col_spec = pl.BlockSpec((t, HEAD_PAIR), lambda h, i: (0, h))
    const = pl.BlockSpec(scan_suffix.shape, lambda h, i: (0, 0))
    full = jax.ShapeDtypeStruct((t, d), F32)
    return pl.pallas_call(
        body, name=name, grid=(hp, nq), in_specs=[blk, col_spec, col_spec, blk, const, const],
        out_specs=[blk, col_spec, col_spec], out_shape=[full, full, full],
        scratch_shapes=[pltpu.VMEM((2, nq, tq, tq), F32), pltpu.VMEM((2, nq, tq, tq), F32)],
        compiler_params=_params(2))(qn, kn, vb, dob, scan_suffix, scan_prefix)


def _chip_at(x, y, j):
    return (1 - x if j & 2 else x, 1 - y if j & 1 else y)


def _all_gather(shards, name):
    n = len(shards)

    def body(*refs):
        x_refs, out_refs = refs[:n], refs[n:2 * n]
        send_sems, recv_sems, local_sems = refs[2 * n:]
        x, y, c = lax.axis_index("x"), lax.axis_index("y"), lax.axis_index("c")
        me, sibling = (x, y, c), (x, y, 1 - c)
        chips = [_chip_at(x, y, j) for j in (1, 2, 3)]

        def rows(t, px, py, pc):
            return out_refs[t].at[4 * px + 2 * py + pc]

        def copy(t, k, block, to, src=None):
            return pltpu.make_async_remote_copy(src_ref=rows(t, *block) if src is None else src, dst_ref=rows(t, *block),
                                                send_sem=send_sems.at[t, k], recv_sem=recv_sems.at[t, k],
                                                device_id=to, device_id_type=MESH)

        mine = [pltpu.make_async_copy(x_refs[t], rows(t, *me), local_sems.at[t]) for t in range(n)]
        for cp in mine:
            cp.start()
        first = []
        for t in range(n):
            first.append(copy(t, 0, me, sibling, src=x_refs[t]))
            first += [copy(t, 1 + j, me, (*chip, c), src=x_refs[t]) for j, chip in enumerate(chips)]
        for cp in first:
            cp.start()
        passed = []
        for t in range(n):
            for j, chip in enumerate(chips):
                copy(t, 1 + j, (*chip, c), me).wait_recv()
                passed.append(copy(t, 4 + j, (*chip, c), sibling))
                passed[-1].start()
        for t in range(n):
            copy(t, 0, sibling, me).wait_recv()
            for j, chip in enumerate(chips):
                copy(t, 4 + j, (*chip, 1 - c), me).wait_recv()
        for cp in first + passed:
            cp.wait_send()
        for cp in mine:
            cp.wait()

    any_spec = pl.BlockSpec(memory_space=pl.ANY)
    return pl.pallas_call(
        body, name=name, out_shape=[jax.ShapeDtypeStruct((N_DEV,) + a.shape, a.dtype) for a in shards],
        in_specs=[any_spec] * n, out_specs=[any_spec] * n,
        scratch_shapes=[pltpu.SemaphoreType.DMA((n, 7)), pltpu.SemaphoreType.DMA((n, 7)), pltpu.SemaphoreType.DMA((n,))])(*shards)


def _exchange_sibling(tensors, name):
    n = len(tensors)

    def body(*refs):
        g_refs, recv_refs = refs[:n], refs[n:2 * n]
        send_sems, recv_sems = refs[2 * n:]
        x, y, c = lax.axis_index("x"), lax.axis_index("y"), lax.axis_index("c")
        copies = []
        for t in range(n):
            for j in range(4):
                cx, cy = _chip_at(x, y, j)
                copies.append(pltpu.make_async_remote_copy(
                    src_ref=g_refs[t].at[4 * cx + 2 * cy + (1 - c)], dst_ref=recv_refs[t].at[j], send_sem=send_sems.at[t, j],
                    recv_sem=recv_sems.at[t, j], device_id=(x, y, 1 - c), device_id_type=MESH))
        for cp in copies:
            cp.start()
        for cp in copies:
            cp.wait_recv()
        for cp in copies:
            cp.wait_send()

    any_spec = pl.BlockSpec(memory_space=pl.ANY)
    return pl.pallas_call(
        body, name=name, out_shape=[jax.ShapeDtypeStruct((4,) + a.shape[1:], a.dtype) for a in tensors],
        in_specs=[any_spec] * n, out_specs=[any_spec] * n,
        scratch_shapes=[pltpu.SemaphoreType.DMA((n, 4)), pltpu.SemaphoreType.DMA((n, 4))])(*tensors)


def _exchange_chips(parts, name):
    n = len(parts)

    def body(*refs):
        p_refs, recv_refs = refs[:n], refs[n:2 * n]
        send_sems, recv_sems = refs[2 * n:]
        x, y, c = lax.axis_index("x"), lax.axis_index("y"), lax.axis_index("c")
        copies = []
        for t in range(n):
            for j in (1, 2, 3):
                copies.append(pltpu.make_async_remote_copy(
                    src_ref=p_refs[t].at[j], dst_ref=recv_refs[t].at[j - 1], send_sem=send_sems.at[t, j - 1],
                    recv_sem=recv_sems.at[t, j - 1], device_id=(*_chip_at(x, y, j), c), device_id_type=MESH))
        for cp in copies:
            cp.start()
        for cp in copies:
            cp.wait_recv()
        for cp in copies:
            cp.wait_send()

    any_spec = pl.BlockSpec(memory_space=pl.ANY)
    return pl.pallas_call(
        body, name=name, out_shape=[jax.ShapeDtypeStruct((3,) + a.shape[1:], a.dtype) for a in parts],
        in_specs=[any_spec] * n, out_specs=[any_spec] * n,
        scratch_shapes=[pltpu.SemaphoreType.DMA((n, 3)), pltpu.SemaphoreType.DMA((n, 3))])(*parts)


_HBM = pl.BlockSpec(memory_space=pltpu.HBM)
_SEM = pl.BlockSpec(memory_space=pltpu.SEMAPHORE)
_DATAFLOW = pltpu.SideEffectType.DATAFLOW_SIDE_EFFECTING


def _peer(x, y, c, mask):
    return (1 - x if mask & 4 else x, 1 - y if mask & 2 else y, 1 - c if mask & 1 else c)


def _gather_copies(x_refs, land_refs, send_sems, recv_sems, waiting):
    x, y, c = lax.axis_index("x"), lax.axis_index("y"), lax.axis_index("c")
    copies = []
    for t in range(len(x_refs)):
        for mask in range(1, N_DEV):
            px, py, pc = _peer(x, y, c, mask)
            block = 4 * px + 2 * py + pc if waiting else 4 * x + 2 * y + c
            copies.append(pltpu.make_async_remote_copy(
                src_ref=x_refs[t], dst_ref=land_refs[t].at[block], send_sem=send_sems.at[7 * t + mask - 1],
                recv_sem=recv_sems.at[7 * t + mask - 1], device_id=(px, py, pc), device_id_type=MESH))
    return copies


def _chip_copies(p_refs, land_refs, send_sems, recv_sems, waiting):
    x, y, c = lax.axis_index("x"), lax.axis_index("y"), lax.axis_index("c")
    return [pltpu.make_async_remote_copy(src_ref=p_refs[t].at[j], dst_ref=land_refs[t].at[j - 1], send_sem=send_sems.at[3 * t + j - 1],
                                         recv_sem=recv_sems.at[3 * t + j - 1], device_id=(*_chip_at(x, y, j), c), device_id_type=MESH)
            for t in range(len(p_refs)) for j in (1, 2, 3)]


def _copies_start(srcs, land_shapes, per_src, copies, name):
    n = len(srcs)

    def body(*refs):
        for cp in copies(refs[:n], refs[n:2 * n], refs[2 * n], refs[2 * n + 1], False):
            cp.start()
        refs[-1][...] = jnp.zeros(refs[-1].shape, F32)

    lands = [lax.empty(shape, a.dtype) for shape, a in zip(land_shapes, srcs)]
    args = [pltpu.with_memory_space_constraint(a, pltpu.HBM) for a in (*srcs, *lands)]
    sems = pltpu.SemaphoreType.DMA((n * per_src,))
    res = pl.pallas_call(
        body, name=name, out_shape=(sems, sems, *[pltpu.HBM(a.shape, a.dtype) for a in args], jax.ShapeDtypeStruct((8, LANES), F32)),
        in_specs=[_HBM] * (2 * n), out_specs=(_SEM, _SEM, *[_HBM] * (2 * n), pl.BlockSpec(memory_space=pltpu.VMEM)),
        input_output_aliases={i: 2 + i for i in range(2 * n)}, compiler_params=pltpu.CompilerParams(has_side_effects=_DATAFLOW))(*args)
    return res[0], res[1], list(res[2:2 + n]), list(res[2 + n:2 + 2 * n]), res[-1]


def _copies_wait(send_sems, recv_sems, srcs, lands, after, copies, name):
    n = len(srcs)

    def body(*refs):
        for cp in copies(refs[:n], refs[n:2 * n], refs[2 * n], refs[2 * n + 1], True):
            cp.wait_send()
            cp.wait_recv()

    res = pl.pallas_call(
        body, name=name, out_shape=tuple(pltpu.HBM(a.shape, a.dtype) for a in (*srcs, *lands)),
        in_specs=[_HBM] * (2 * n) + [_SEM, _SEM, pl.BlockSpec(memory_space=pl.ANY)], out_specs=tuple([_HBM] * (2 * n)),
        input_output_aliases={i: i for i in range(2 * n)}, compiler_params=pltpu.CompilerParams(has_side_effects=_DATAFLOW))(
            *srcs, *lands, send_sems, recv_sems, after)
    return list(res[:n]), list(res[n:])


def _add_sibling(g8, recv1, name):
    _, a, b = g8.shape
    ta = next(cand for cand in (512, 384, 256, 128) if a % cand == 0)

    def body(g_ref, r_ref, o_ref):
        o_ref[...] = (g_ref[...].astype(F32) + r_ref[...].astype(F32)).astype(o_ref.dtype)

    def own(j, i):
        x, y, c = lax.axis_index("x"), lax.axis_index("y"), lax.axis_index("c")
        return (4 * (x ^ (j >> 1)) + 2 * (y ^ (j & 1)) + c, i, 0)

    slot = pl.BlockSpec((None, ta, b), lambda j, i: (j, i, 0))
    return pl.pallas_call(body, name=name, grid=(4, a // ta), in_specs=[pl.BlockSpec((None, ta, b), own), slot],
                          out_specs=slot, out_shape=jax.ShapeDtypeStruct((4, a, b), g8.dtype),
                          compiler_params=_params(2))(g8, recv1)


def _ada_forward(c_all, ada_w, ada_b_cols):
    depth, d, cols = ada_w.shape

    def body(c_ref, w_ref, b_ref, o_ref):
        cv = c_ref[...]
        act = cv * _sigmoid(cv)
        o_ref[...] = jnp.dot(act, w_ref[...], preferred_element_type=F32, precision=lax.Precision.HIGHEST) + b_ref[...]

    return pl.pallas_call(
        body, name="ada_forward", grid=(depth,),
        in_specs=[pl.BlockSpec((N_DEV, d), lambda l: (0, 0)), pl.BlockSpec((None, d, cols), lambda l: (l, 0, 0)),
                  pl.BlockSpec((None, 1, cols), lambda l: (l, 0, 0))],
        out_specs=pl.BlockSpec((None, N_DEV, cols), lambda l: (l, 0, 0)),
        out_shape=jax.ShapeDtypeStruct((depth, N_DEV, cols), F32), compiler_params=_params(1))(
            c_all, ada_w, ada_b_cols.reshape(depth, 1, cols))


def _ada_backward(c_all, dmod_cols):
    depth, _, cols = dmod_cols.shape
    d = c_all.shape[1]

    def body(c_ref, g_ref, o_ref):
        cv = c_ref[...]
        act = cv * _sigmoid(cv)
        o_ref[...] = lax.dot_general(act, g_ref[...], _TN, preferred_element_type=F32, precision=lax.Precision.HIGHEST)

    return pl.pallas_call(
        body, name="ada_backward", grid=(depth,),
        in_specs=[pl.BlockSpec((N_DEV, d), lambda l: (0, 0)), pl.BlockSpec((None, N_DEV, cols), lambda l: (l, 0, 0))],
        out_specs=pl.BlockSpec((None, d, cols), lambda l: (l, 0, 0)),
        out_shape=jax.ShapeDtypeStruct((depth, d, cols), F32), compiler_params=_params(1))(c_all, dmod_cols)


def _device_sum(a):
    _, r, w = a.shape

    def body(a_ref, o_ref):
        acc = a_ref[0]
        for dev in range(1, N_DEV):
            acc = acc + a_ref[dev]
        o_ref[...] = acc

    return pl.pallas_call(body, name="device_sum", out_shape=jax.ShapeDtypeStruct((r, w), F32),
                          in_specs=[pl.BlockSpec(memory_space=pltpu.VMEM)], out_specs=pl.BlockSpec(memory_space=pltpu.VMEM))(a)


def _adamw(w, g_layers, m, v, name):
    shape = w.shape
    cols = shape[-1]
    rows = int(np.prod(shape[:-1]))
    layers = len(g_layers)
    per_layer = rows // layers
    tile = next((cand for cand in (512, 352, 256, 128) if per_layer % cand == 0), per_layer)
    n_l = per_layer // tile
    n_parts = len(g_layers[0])

    def fn(pid, tv, hv, bv):
        wv, mv, vv = tv[:3]
        gv = None
        for l in range(layers):
            gl = None
            for part in tv[3 + l * n_parts:3 + (l + 1) * n_parts]:
                gl = part.astype(F32) if gl is None else gl + part.astype(F32)
            gv = gl if gv is None else jnp.where(pid >= l * n_l, gl, gv)
        mn = ADAM_B1 * mv + (1.0 - ADAM_B1) * gv
        vn = ADAM_B2 * vv + (1.0 - ADAM_B2) * (gv * gv)
        m_hat = mn / (1.0 - ADAM_B1 ** ADAM_STEP)
        v_hat = vn / (1.0 - ADAM_B2 ** ADAM_STEP)
        delta = -ADAM_LR * (m_hat / (jnp.sqrt(v_hat) + ADAM_EPS) + ADAM_WD * wv)
        return [gv, delta, mn, vn], []

    tiled = [(a.reshape(rows, cols), 0, cols) for a in (w, m, v)]
    for l, parts in enumerate(g_layers):
        clamp = functools.partial(lambda i, l: jnp.clip(i - l * n_l, 0, n_l - 1), l=l)
        tiled += [(p.reshape(per_layer, cols), 0, cols, clamp) for p in parts]
    res = _rowwise(fn, name, rows=rows, tile=tile, tiled=tiled, outs=[(cols, F32)] * 4)
    return [r.reshape(shape) for r in res]


def _norm_modulate(x, y, gate, ln_g, scale, shift, name, tile):
    t, d = x.shape

    def fn(pid, tv, hv, bv):
        if y is None:
            xn = tv[0]
            g_ln, sc, sh = bv
        else:
            g_gate, g_ln, sc, sh = bv
            xn = tv[0] + g_gate * tv[1]
        r = lax.rsqrt(jnp.mean(xn * xn, axis=-1, keepdims=True) + EPS)
        h = (xn * r * g_ln) * (1.0 + sc) + sh
        return ([h] if y is None else [xn, h]), []

    if y is None:
        h, = _rowwise(fn, name, rows=t, tile=tile, tiled=[(x, 0, d)], bcast=[ln_g, scale, shift], outs=[(d, BF16)])
        return x, h
    xn, h = _rowwise(fn, name, rows=t, tile=tile, tiled=[(x, 0, d), (y, 0, d)], bcast=[gate, ln_g, scale, shift],
                     outs=[(d, F32), (d, BF16)])
    return xn, h


def _norm_backward(dh, x, dres, ln_g, scale, name, tile):
    t, d = x.shape

    def fn(pid, tv, hv, bv):
        dhv, xv, dr = tv
        g_ln, sc = bv
        r = lax.rsqrt(jnp.mean(xv * xv, axis=-1, keepdims=True) + EPS)
        xn = xv * r
        dxn = dhv * (1.0 + sc) * g_ln
        dx = dr + r * (dxn - xn * jnp.mean(dxn * xn, axis=-1, keepdims=True))
        return [dx], [_colsum(dhv), _colsum(dhv * (xn * g_ln)), _colsum(dhv * (1.0 + sc) * xn)]

    return _rowwise(fn, name, rows=t, tile=tile, tiled=[(dh, 0, d), (x, 0, d), (dres, 0, d)], bcast=[ln_g, scale],
                    outs=[(d, F32)], accs=[(1, d)] * 3)


def _shift_rows(u, halo, k, pid, first_tile_zero):
    rows = lax.broadcasted_iota(jnp.int32, u.shape, 0)
    halo = halo * jnp.where(pid == 0, 0.0, 1.0) if first_tile_zero else halo
    out = pltpu.roll(u, k, axis=0)
    for j in range(k):
        out = jnp.where(rows == j, halo[8 - k + j:8 - k + j + 1, :], out)
    return out


def _shift_rows_up(u, halo, k, pid, n_tiles):
    tile = u.shape[0]
    rows = lax.broadcasted_iota(jnp.int32, u.shape, 0)
    halo = halo * jnp.where(pid == n_tiles - 1, 0.0, 1.0)
    out = pltpu.roll(u, tile - k, axis=0)
    for j in range(k):
        out = jnp.where(rows == tile - k + j, halo[j:j + 1, :], out)
    return out


def _layer_forward(x_in, y_prev, gate_prev, mod, w, l, tile):
    sh1, sc1, g1, sh2, sc2, g2 = mod
    t, d = x_in.shape
    f = w["w_g"].shape[1]
    bd2 = _head_sum_matrix()
    s = {}
    s["x"], s["h"] = _norm_modulate(x_in, y_prev, gate_prev, w["ln1"], sc1, sh1, f"l{l}_norm1", tile)
    p = _matmul([(s["h"], 0, w["w_in"], 0)], "nn", m=t, n=8 * d, k=d, tm=1024, tn=512, out_dtype=F32, name=f"l{l}_in_proj")
    s["p"] = p

    def qk_norm(pid, tv, hv, bv):
        qr, kr, vr = tv
        qg, kg, bd = bv

        def nrm(xv, g):
            r = lax.rsqrt(_head_sums(xv * xv, bd) * (1.0 / HEAD_DIM) + EPS)
            return xv * r * g
        return [nrm(qr, qg) * 0.125, nrm(kr, kg), vr], []

    s["qn"], s["kn"], s["vb"] = _rowwise(qk_norm, f"l{l}_qk_norm", rows=t, tile=tile, tiled=[(p, 0, d), (p, 1, d), (p, 2, d)],
                                         bcast=[w["qg"], w["kg"], bd2], outs=[(d, BF16)] * 3)
    s["ya"] = _attention_fwd(s["qn"], s["kn"], s["vb"], f"l{l}_attention")

    def conv_fwd(pid, tv, hv, bv):
        cb, cc, cx = tv
        u = cc * cx
        hu = hv[0] * hv[1]
        cw = bv[0]
        conv = cw[0:1, :] * _shift_rows(u, hu, 2, pid, True) + cw[1:2, :] * _shift_rows(u, hu, 1, pid, True) + cw[2:3, :] * u
        return [cb * conv], []

    s["yb"], = _rowwise(conv_fwd, f"l{l}_conv", rows=t, tile=tile, tiled=[(p, 3, d), (p, 4, d), (p, 5, d)],
                        halos=[(p, 4, d, "prev"), (p, 5, d, "prev")], bcast=[w["conv"]], outs=[(d, BF16)])
    s["a"] = _matmul([(s["ya"], 0, w["w_a"], 0)], "nn", m=t, n=d, k=d, tm=1024, tn=512, out_dtype=F32, name=f"l{l}_branch_a")
    s["b"] = _matmul([(s["yb"], 0, w["w_b"], 0)], "nn", m=t, n=d, k=d, tm=1024, tn=512, out_dtype=F32, name=f"l{l}_branch_b")

    def merge(pid, tv, hv, bv):
        av, bvv, ga, gb = tv
        return [_sigmoid(ga) * av + _sigmoid(gb) * bvv], []

    s["merged"], = _rowwise(merge, f"l{l}_merge", rows=t, tile=tile, tiled=[(s["a"], 0, d), (s["b"], 0, d), (p, 6, d), (p, 7, d)],
                            outs=[(d, BF16)])
    s["mo"] = _matmul([(s["merged"], 0, w["w_o"], 0)], "nn", m=t, n=d, k=d, tm=1024, tn=512, out_dtype=F32, name=f"l{l}_out_proj")
    s["x2"], s["h2"] = _norm_modulate(s["x"], s["mo"], g1, w["ln2"], sc2, sh2, f"l{l}_norm2", tile)
    fn_tile = f // 2
    s["g"] = _matmul([(s["h2"], 0, w["w_g"], 0)], "nn", m=t, n=f, k=d, tm=512, tn=fn_tile, out_dtype=F32, name=f"l{l}_ffn_gate")
    s["u"] = _matmul([(s["h2"], 0, w["w_u"], 0)], "nn", m=t, n=f, k=d, tm=512, tn=fn_tile, out_dtype=F32, name=f"l{l}_ffn_up")

    def swiglu(pid, tv, hv, bv):
        gv, uv = tv
        return [gv * _sigmoid(gv) * uv], []

    s["s"], = _rowwise(swiglu, f"l{l}_swiglu", rows=t, tile=tile // 2, tiled=[(s["g"], 0, f), (s["u"], 0, f)], outs=[(f, BF16)])
    s["f"] = _matmul([(s["s"], 0, w["w_d"], 0)], "nn", m=t, n=d, k=f, tm=512, tn=512, out_dtype=F32, name=f"l{l}_ffn_down")
    return s


def _layer_backward(dx3, s, mod, w, l, tile):
    sh1, sc1, g1, sh2, sc2, g2 = mod
    t, d = dx3.shape
    f = w["w_g"].shape[1]
    p = s["p"]
    bd2 = _head_sum_matrix()
    n_tiles = t // tile
    grads = {}

    def gate_bwd(pid, tv, hv, bv):
        return [tv[0] * bv[0]], [_colsum(tv[0] * tv[1])]

    df, dg2 = _rowwise(gate_bwd, f"l{l}_bwd_gate2", rows=t, tile=tile, tiled=[(dx3, 0, d), (s["f"], 0, d)], bcast=[g2],
                       outs=[(d, BF16)], accs=[(1, d)])
    grads["w_d"] = _matmul([(s["s"], 0, df, 0)], "tn", m=f, n=d, k=t, tm=512, tn=512, out_dtype=BF16, name=f"l{l}_dw_down")
    ds = _matmul([(df, 0, w["w_d"], 0)], "nt", m=t, n=f, k=d, tm=512, tn=f // 2, out_dtype=F32, name=f"l{l}_d_swiglu")

    def swiglu_bwd(pid, tv, hv, bv):
        dsv, gv, uv = tv
        sig = _sigmoid(gv)
        return [dsv * uv * (sig * (1.0 + gv * (1.0 - sig))), dsv * (gv * sig)], []

    dgt, dup = _rowwise(swiglu_bwd, f"l{l}_bwd_swiglu", rows=t, tile=tile // 2, tiled=[(ds, 0, f), (s["g"], 0, f), (s["u"], 0, f)],
                        outs=[(f, BF16)] * 2)
    grads["w_g"] = _matmul([(s["h2"], 0, dgt, 0)], "tn", m=d, n=f, k=t, tm=512, tn=f // 2, out_dtype=BF16, name=f"l{l}_dw_gate")
    grads["w_u"] = _matmul([(s["h2"], 0, dup, 0)], "tn", m=d, n=f, k=t, tm=512, tn=f // 2, out_dtype=BF16, name=f"l{l}_dw_up")
    dh2 = _matmul([(dgt, 0, w["w_g"], 0), (dup, 0, w["w_u"], 0)], "nt", m=t, n=d, k=f, tm=512, tn=512, out_dtype=F32,
                  name=f"l{l}_dh2")
    dx2, dsh2, dsc2, grads["ln2"] = _norm_backward(dh2, s["x2"], dx3, w["ln2"], sc2, f"l{l}_bwd_norm2", tile)

    dmo, dg1 = _rowwise(gate_bwd, f"l{l}_bwd_gate1", rows=t, tile=tile, tiled=[(dx2, 0, d), (s["mo"], 0, d)], bcast=[g1],
                        outs=[(d, BF16)], accs=[(1, d)])
    grads["w_o"] = _matmul([(s["merged"], 0, dmo, 0)], "tn", m=d, n=d, k=t, tm=512, tn=512, out_dtype=BF16, name=f"l{l}_dw_out")
    dmerged = _matmul([(dmo, 0, w["w_o"], 0)], "nt", m=t, n=d, k=d, tm=1024, tn=512, out_dtype=F32, name=f"l{l}_d_merged")

    def merge_bwd(pid, tv, hv, bv):
        dm, av, bvv, ga, gb = tv
        sa, sb = _sigmoid(ga), _sigmoid(gb)
        return [dm * sa, dm * sb, dm * av * (sa * (1.0 - sa)), dm * bvv * (sb * (1.0 - sb))], []

    d_a, d_b, dga, dgb = _rowwise(merge_bwd, f"l{l}_bwd_merge", rows=t, tile=tile,
                                  tiled=[(dmerged, 0, d), (s["a"], 0, d), (s["b"], 0, d), (p, 6, d), (p, 7, d)], outs=[(d, BF16)] * 4)
    grads["w_a"] = _matmul([(s["ya"], 0, d_a, 0)], "tn", m=d, n=d, k=t, tm=512, tn=512, out_dtype=BF16, name=f"l{l}_dw_a")
    grads["w_b"] = _matmul([(s["yb"], 0, d_b, 0)], "tn", m=d, n=d, k=t, tm=512, tn=512, out_dtype=BF16, name=f"l{l}_dw_b")
    dya = _matmul([(d_a, 0, w["w_a"], 0)], "nt", m=t, n=d, k=d, tm=1024, tn=512, out_dtype=BF16, name=f"l{l}_d_ya")
    dyb = _matmul([(d_b, 0, w["w_b"], 0)], "nt", m=t, n=d, k=d, tm=1024, tn=512, out_dtype=F32, name=f"l{l}_d_yb")

    def conv_bwd(pid, tv, hv, bv):
        dy, cb, cc, cx = tv
        cw = bv[0]
        u, hu = cc * cx, hv[0] * hv[1]
        u1, u2 = _shift_rows(u, hu, 1, pid, True), _shift_rows(u, hu, 2, pid, True)
        conv = cw[0:1, :] * u2 + cw[1:2, :] * u1 + cw[2:3, :] * u
        dconv, hd = dy * cb, hv[2] * hv[3]
        du = (cw[2:3, :] * dconv + cw[1:2, :] * _shift_rows_up(dconv, hd, 1, pid, n_tiles)
              + cw[0:1, :] * _shift_rows_up(dconv, hd, 2, pid, n_tiles))
        return [dy * conv, du * cx, du * cc], [_colsum(dconv * u2), _colsum(dconv * u1), _colsum(dconv * u)]

    dcb, dcc, dcx, dcw0, dcw1, dcw2 = _rowwise(
        conv_bwd, f"l{l}_bwd_conv", rows=t, tile=tile, tiled=[(dyb, 0, d), (p, 3, d), (p, 4, d), (p, 5, d)],
        halos=[(p, 4, d, "prev"), (p, 5, d, "prev"), (dyb, 0, d, "next"), (p, 3, d, "next")], bcast=[w["conv"]],
        outs=[(d, BF16)] * 3, accs=[(1, d)] * 3)
    grads["conv"] = jnp.concatenate([dcw0, dcw1, dcw2], axis=0)

    dqs, dkn, dv = _attention_bwd(s["qn"], s["kn"], s["vb"], dya, f"l{l}_bwd_attention")

    def qk_norm_bwd(pid, tv, hv, bv):
        dq, dk, qr, kr, dvv = tv
        qg, kg, bd = bv

        def bwd(dy, xv, g):
            r = lax.rsqrt(_head_sums(xv * xv, bd) * (1.0 / HEAD_DIM) + EPS)
            yv = xv * r
            dyn = dy * g
            dx = r * (dyn - yv * (_head_sums(dyn * yv, bd) * (1.0 / HEAD_DIM)))
            return dx, _colsum(dy * yv)

        dxq, dgq = bwd(dq * 0.125, qr, qg)
        dxk, dgk = bwd(dk, kr, kg)
        return [dxq, dxk, dvv], [dgq, dgk]

    dqr, dkr, dvb, grads["qg"], grads["kg"] = _rowwise(
        qk_norm_bwd, f"l{l}_bwd_qk_norm", rows=t, tile=tile, tiled=[(dqs, 0, d), (dkn, 0, d), (p, 0, d), (p, 1, d), (dv, 0, d)],
        bcast=[w["qg"], w["kg"], bd2], outs=[(d, BF16)] * 3, accs=[(1, d)] * 2)

    dp = [dqr, dkr, dvb, dcb, dcc, dcx, dga, dgb]
    grads["w_in"] = [_matmul([(s["h"], 0, dpk, 0)], "tn", m=d, n=d, k=t, tm=512, tn=512, out_dtype=BF16, name=f"l{l}_dw_in{k}")
                     for k, dpk in enumerate(dp)]
    dh = _matmul([(dpk, 0, w["w_in"], k) for k, dpk in enumerate(dp)], "nt", m=t, n=d, k=d, tm=512, tn=512, out_dtype=F32,
                 name=f"l{l}_dh")
    dx, dsh1, dsc1, grads["ln1"] = _norm_backward(dh, s["x"], dx2, w["ln1"], sc1, f"l{l}_bwd_norm1", tile)
    return dx, grads, [dsh1, dsc1, dg1, dsh2, dsc2, dg2]


_BIG = ["w_in", "w_branch_a", "w_branch_b", "w_out", "w_ffn_gate", "w_ffn_up", "w_ffn_down"]
_SHORT = dict(w_in="w_in", w_branch_a="w_a", w_branch_b="w_b", w_out="w_o", w_ffn_gate="w_g", w_ffn_up="w_u", w_ffn_down="w_d")
_COL_SHARDED = {"w_in", "w_ffn_gate", "w_ffn_up"}


def kernel(x, c, ada_w, ada_b, ln1_g, w_in, q_norm_g, k_norm_g, conv_w, w_branch_a, w_branch_b, w_out, ln2_g, w_ffn_gate, w_ffn_up, w_ffn_down, loss_target, m_ada_w, m_ada_b, m_ln1_g, m_w_in, m_q_norm_g, m_k_norm_g, m_conv_w, m_w_branch_a, m_w_branch_b, m_w_out, m_ln2_g, m_w_ffn_gate, m_w_ffn_up, m_w_ffn_down, v_ada_w, v_ada_b, v_ln1_g, v_w_in, v_q_norm_g, v_k_norm_g, v_conv_w, v_w_branch_a, v_w_branch_b, v_w_out, v_ln2_g, v_w_ffn_gate, v_w_ffn_up, v_w_ffn_down):
    weights = dict(ada_w=ada_w, ada_b=ada_b, ln1_g=ln1_g, w_in=w_in, q_norm_g=q_norm_g, k_norm_g=k_norm_g, conv_w=conv_w,
                   w_branch_a=w_branch_a, w_branch_b=w_branch_b, w_out=w_out, ln2_g=ln2_g, w_ffn_gate=w_ffn_gate,
                   w_ffn_up=w_ffn_up, w_ffn_down=w_ffn_down)
    m_in = dict(ada_w=m_ada_w, ada_b=m_ada_b, ln1_g=m_ln1_g, w_in=m_w_in, q_norm_g=m_q_norm_g, k_norm_g=m_k_norm_g,
                conv_w=m_conv_w, w_branch_a=m_w_branch_a, w_branch_b=m_w_branch_b, w_out=m_w_out, ln2_g=m_ln2_g,
                w_ffn_gate=m_w_ffn_gate, w_ffn_up=m_w_ffn_up, w_ffn_down=m_w_ffn_down)
    v_in = dict(ada_w=v_ada_w, ada_b=v_ada_b, ln1_g=v_ln1_g, w_in=v_w_in, q_norm_g=v_q_norm_g, k_norm_g=v_k_norm_g,
                conv_w=v_conv_w, w_branch_a=v_w_branch_a, w_branch_b=v_w_branch_b, w_out=v_w_out, ln2_g=v_ln2_g,
                w_ffn_gate=v_w_ffn_gate, w_ffn_up=v_w_ffn_up, w_ffn_down=v_w_ffn_down)
    names = list(weights)

    mx, my, mc = lax.axis_index("x"), lax.axis_index("y"), lax.axis_index("c")
    me = 4 * mx + 2 * my + mc
    xs, target = x[0], loss_target[0]
    t, d = xs.shape
    depth = ada_w.shape[0]
    mod_cols = ada_w.shape[2]
    conv_cols = conv_w.shape[2]
    row_w = 1024
    tile = 512 if t % 512 == 0 else t

    small = jnp.concatenate([c.reshape(-1), conv_w.reshape(-1)])
    small_n = -(-small.shape[0] // row_w) * row_w
    small = jnp.pad(small, (0, small_n - small.shape[0])).reshape(-1, row_w)
    small_all = _all_gather([small], "gather_cond")[0].reshape(N_DEV, -1)
    c_all = small_all[:, :d]
    conv_full = jnp.transpose(small_all[:, d:d + depth * 3 * conv_cols].reshape(N_DEV, depth, 3, conv_cols), (1, 2, 0, 3)
                              ).reshape(depth, 3, N_DEV * conv_cols)

    ada_b_cols = lax.dynamic_slice_in_dim(ada_b, me * mod_cols, mod_cols, axis=1)
    mod_part = _ada_forward(c_all, ada_w, ada_b_cols)
    mod_all = _all_gather([mod_part.reshape(-1, row_w)], "gather_mod")[0].reshape(N_DEV, depth, N_DEV, mod_cols)
    mod_mine = lax.dynamic_index_in_dim(mod_all, me, axis=2, keepdims=False)
    mod = jnp.transpose(mod_mine, (1, 0, 2)).reshape(depth, 6, 1, d)

    f_shard = w_ffn_gate.shape[2]
    f_pad = -(-f_shard // LANES) * LANES - f_shard
    pads = dict(w_ffn_gate=((0, 0), (0, 0), (0, f_pad)), w_ffn_up=((0, 0), (0, 0), (0, f_pad)), w_ffn_down=((0, 0), (0, f_pad), (0, 0)))
    shards = [weights[n].astype(BF16) for n in _BIG]
    shards = [jnp.pad(a, pads[n]) if n in pads else a for n, a in zip(_BIG, shards)]
    gathered = dict(zip(_BIG, _all_gather(shards, "gather_weights")))
    layer_w = []
    for l in range(depth):
        wl = {}
        for n in _BIG:
            al = gathered[n][:, l]
            wl[_SHORT[n]] = (jnp.transpose(al, (1, 0, 2)).reshape(al.shape[1], -1) if n in _COL_SHARDED
                             else al.reshape(-1, al.shape[2]))
        wl["ln1"], wl["ln2"] = ln1_g[l][None], ln2_g[l][None]
        wl["qg"] = jnp.tile(q_norm_g[l], d // HEAD_DIM)[None]
        wl["kg"] = jnp.tile(k_norm_g[l], d // HEAD_DIM)[None]
        wl["conv"] = conv_full[l]
        layer_w.append(wl)

    saved = []
    x_cur, y_prev, gate_prev = xs, None, None
    for l in range(depth):
        mods = [mod[l, k] for k in range(6)]
        s = _layer_forward(x_cur, y_prev, gate_prev, mods, layer_w[l], l, tile)
        saved.append(s)
        x_cur, y_prev, gate_prev = s["x2"], s["f"], mods[5]

    def loss_head(pid, tv, hv, bv):
        diff = tv[0] + bv[0] * tv[1] - tv[2]
        return [diff * (1.0 / d)], [_colsum(diff * diff) * (0.5 / d)]

    dx, loss_cols = _rowwise(loss_head, "loss_head", rows=t, tile=tile, tiled=[(x_cur, 0, d), (y_prev, 0, d), (target, 0, d)],
                             bcast=[gate_prev], outs=[(d, F32)], accs=[(1, d)])

    def by_owner(n, full):
        if n == "w_in":
            return jnp.stack(full)
        if n in _COL_SHARDED:
            return jnp.transpose(full.reshape(full.shape[0], N_DEV, -1), (1, 0, 2))
        return full.reshape(N_DEV, -1, full.shape[1])

    layer_g, dmods, parts, recv2, pending = [None] * depth, [None] * depth, [None] * depth, [None] * depth, {}
    started = None
    for l in reversed(range(depth)):
        mods = [mod[l, k] for k in range(6)]
        if started is not None:
            mods[5] = mods[5] + started[:1, :1]
        dx, layer_g[l], dmods[l] = _layer_backward(dx, saved[l], mods, layer_w[l], l, tile)
        tensors = [by_owner(n, layer_g[l][_SHORT[n]]) for n in _BIG]
        recv1 = _exchange_sibling(tensors, f"rs_exchange_sibling_l{l}")
        parts[l] = [_add_sibling(a, r, f"rs_add_sibling_l{l}_{i}") for i, (a, r) in enumerate(zip(tensors, recv1))]
        if l > 0:
            *pending[l], started = _copies_start(parts[l], [(3,) + a.shape[1:] for a in parts[l]], 3, _chip_copies,
                                                 f"rs_exchange_chips_l{l}_start")
        else:
            recv2[l] = _exchange_chips(parts[l], "rs_exchange_chips_l0")
    for l, (s_sems, r_sems, srcs, lands) in pending.items():
        parts[l], recv2[l] = _copies_wait(s_sems, r_sems, srcs, lands, dx, _chip_copies, f"rs_exchange_chips_l{l}_wait")
    grad_x = dx[None]

    pieces = [jnp.concatenate(dmods[l], axis=1) for l in range(depth)]
    for key in ("ln1", "ln2", "qg", "kg"):
        pieces += [layer_g[l][key] for l in range(depth)]
    pieces += [layer_g[l]["conv"].reshape(1, -1) for l in range(depth)]
    pieces.append(loss_cols)
    part_small = jnp.concatenate(pieces, axis=1).reshape(-1, row_w)
    part_all = _all_gather([part_small], "gather_small_grads")[0]
    summed = _device_sum(part_all).reshape(-1)
    n_mod = depth * 6 * d
    dmod_all = part_all.reshape(N_DEV, -1)[:, :n_mod].reshape(N_DEV, depth, 6 * d)
    dmod_cols = jnp.transpose(lax.dynamic_slice_in_dim(dmod_all, me * mod_cols, mod_cols, axis=2), (1, 0, 2))
    g = {"ada_w": _ada_backward(c_all, dmod_cols), "ada_b": summed[:n_mod].reshape(depth, 6 * d)}
    off = n_mod
    g["ln1_g"] = summed[off:off + depth * d].reshape(depth, d)
    g["ln2_g"] = summed[off + depth * d:off + 2 * depth * d].reshape(depth, d)
    g["q_norm_g"] = summed[off + 2 * depth * d:off + 3 * depth * d].reshape(depth, d // HEAD_DIM, HEAD_DIM).sum(axis=1)
    g["k_norm_g"] = summed[off + 3 * depth * d:off + 4 * depth * d].reshape(depth, d // HEAD_DIM, HEAD_DIM).sum(axis=1)
    off += 4 * depth * d
    conv_g = summed[off:off + depth * 3 * d].reshape(depth, 3, N_DEV, conv_cols)
    g["conv_w"] = lax.dynamic_index_in_dim(conv_g, me, axis=2, keepdims=False)
    off += depth * 3 * d
    loss = jnp.sum(summed[off:off + d])

    def shard_of(n, a):
        if n in ("w_ffn_gate", "w_ffn_up"):
            return a[:, :f_shard]
        return a[:f_shard] if n == "w_ffn_down" else a

    outs = {}
    for k, n in enumerate(_BIG):
        g_layers = [[shard_of(n, a) for a in (parts[l][k][0], *recv2[l][k])] for l in range(depth)]
        outs[n] = _adamw(weights[n], g_layers, m_in[n], v_in[n], f"adamw_{n}")
    for n in names:
        if n not in outs:
            outs[n] = _adamw(weights[n], [[g[n]]], m_in[n], v_in[n], f"adamw_{n}")
    return (loss, grad_x, *[outs[n][0] for n in names], *[outs[n][1] for n in names], *[outs[n][2] for n in names],
            *[outs[n][3] for n in names])
```

```python
import functools

import numpy as np
import jax
import jax.numpy as jnp
from jax import lax
from jax.experimental import pallas as pl
from jax.experimental.pallas import tpu as pltpu

F32, BF16 = jnp.float32, jnp.bfloat16
MESH = pl.DeviceIdType.MESH
N_DEV = 8
LANES = 128
HEAD_DIM = 64
HEAD_PAIR = 2 * HEAD_DIM
ATT_TILE = 256
ATT_UNROLL = 4
EPS = 1e-6
VMEM_LIMIT = 56 * 1024 * 1024

ADAM_LR, ADAM_B1, ADAM_B2, ADAM_EPS, ADAM_WD, ADAM_STEP = 0.001, 0.9, 0.999, 1e-08, 0.01, 10

_NT = (((1,), (1,)), ((), ()))
_TN = (((0,), (0,)), ((), ()))
_NN = (((1,), (0,)), ((), ()))


def _params(n_grid):
    return pltpu.CompilerParams(dimension_semantics=("arbitrary",) * n_grid, vmem_limit_bytes=VMEM_LIMIT)


def _sigmoid(x):
    return 1.0 / (1.0 + jnp.exp(-x))


def _rowwise(fn, name, *, rows, tile, tiled=(), halos=(), bcast=(), outs=(), accs=()):
    n = rows // tile
    assert n * tile == rows
    in_specs, args = [], []
    for t in tiled:
        arr, cb, w = t[:3]
        rowmap = t[3] if len(t) > 3 else (lambda i: i)
        in_specs.append(pl.BlockSpec((tile, w), functools.partial(lambda i, cb, rowmap: (rowmap(i), cb), cb=cb, rowmap=rowmap)))
        args.append(arr)
    per_tile8 = tile // 8
    for arr, cb, w, side in halos:
        last8 = arr.shape[0] // 8 - 1
        if side == "prev":
            imap = functools.partial(lambda i, cb: (jnp.maximum(i * per_tile8 - 1, 0), cb), cb=cb)
        else:
            imap = functools.partial(lambda i, cb, last8: (jnp.minimum((i + 1) * per_tile8, last8), cb), cb=cb, last8=last8)
        in_specs.append(pl.BlockSpec((8, w), imap))
        args.append(arr)
    for arr in bcast:
        in_specs.append(pl.BlockSpec(arr.shape, functools.partial(lambda i, nd: (0,) * nd, nd=arr.ndim)))
        args.append(arr)
    out_shape = [jax.ShapeDtypeStruct((rows, w), dt) for w, dt in outs] + [jax.ShapeDtypeStruct(s, F32) for s in accs]
    out_specs = [pl.BlockSpec((tile, w), lambda i: (i, 0)) for w, _ in outs] + [pl.BlockSpec(s, lambda i: (0, 0)) for s in accs]
    nt, nh, nb, no, na = len(tiled), len(halos), len(bcast), len(outs), len(accs)

    def body(*refs):
        pid = pl.program_id(0)
        tv = [r[...] for r in refs[:nt]]
        hv = [r[...] for r in refs[nt:nt + nh]]
        bv = [r[...] for r in refs[nt + nh:nt + nh + nb]]
        out_refs = refs[nt + nh + nb:nt + nh + nb + no]
        acc_refs = refs[nt + nh + nb + no:]
        ov, av = fn(pid, tv, hv, bv)
        for r, v in zip(out_refs, ov):
            r[...] = v.astype(r.dtype)
        if na:
            @pl.when(pid == 0)
            def _():
                for r in acc_refs:
                    r[...] = jnp.zeros(r.shape, F32)
            for r, v in zip(acc_refs, av):
                r[...] += v

    res = pl.pallas_call(body, name=name, grid=(n,), in_specs=in_specs, out_specs=out_specs, out_shape=out_shape,
                         compiler_params=_params(1))(*args)
    return res


def _colsum(v):
    return jnp.sum(v, axis=0, keepdims=True)


def _matmul(pairs, mode, *, m, n, k, tm, tn, out_dtype, name):
    tm, tn = min(tm, m), min(tn, n)
    assert m % tm == 0 and n % tn == 0
    in_specs, args = [], []
    for a, acb, b, bcb in pairs:
        if mode == "tn":
            in_specs.append(pl.BlockSpec((k, tm), functools.partial(lambda i, j, o: (0, o + i), o=acb)))
            in_specs.append(pl.BlockSpec((k, tn), functools.partial(lambda i, j, o: (0, o + j), o=bcb)))
        elif mode == "nn":
            in_specs.append(pl.BlockSpec((tm, k), functools.partial(lambda i, j, o: (i, o), o=acb)))
            in_specs.append(pl.BlockSpec((k, tn), functools.partial(lambda i, j, o: (0, o + j), o=bcb)))
        else:
            in_specs.append(pl.BlockSpec((tm, k), functools.partial(lambda i, j, o: (i, o), o=acb)))
            in_specs.append(pl.BlockSpec((tn, k), functools.partial(lambda i, j, o: (j, o), o=bcb)))
        args += [a, b]
    dims = {"nn": _NN, "nt": _NT, "tn": _TN}[mode]
    npairs = len(pairs)

    def body(*refs):
        o_ref = refs[2 * npairs]
        acc = None
        for p in range(npairs):
            d = lax.dot_general(refs[2 * p][...].astype(BF16), refs[2 * p + 1][...].astype(BF16), dims,
                                preferred_element_type=F32)
            acc = d if acc is None else acc + d
        o_ref[...] = acc.astype(o_ref.dtype)

    return pl.pallas_call(body, name=name, grid=(m // tm, n // tn), in_specs=in_specs,
                          out_specs=pl.BlockSpec((tm, tn), lambda i, j: (i, j)),
                          out_shape=jax.ShapeDtypeStruct((m, n), out_dtype), compiler_params=_params(2))(*args)


def _scan_matrix(kind, n):
    r = np.arange(n)
    tri = (r[:, None] > r[None, :]) if kind == "suffix" else (r[:, None] < r[None, :])
    half = np.concatenate([tri.astype(np.float32), np.ones((n, LANES), np.float32)], axis=1)
    return jnp.asarray(np.concatenate([half, half], axis=0), BF16)


def _head_sum_matrix():
    r = np.arange(LANES)
    bd = (r[:, None] // HEAD_DIM == r[None, :] // HEAD_DIM).astype(np.float32)
    return jnp.asarray(np.concatenate([bd, bd], axis=0), BF16)


def _split_cat(v):
    hi = v.astype(BF16)
    lo = (v - hi.astype(F32)).astype(BF16)
    return jnp.concatenate([hi, lo], axis=1)


def _head_sums(v, bd2):
    hi = v.astype(BF16)
    lo = (v - hi.astype(F32)).astype(BF16)
    parts = []
    for g in range(v.shape[1] // LANES):
        sl = slice(g * LANES, (g + 1) * LANES)
        parts.append(jnp.dot(jnp.concatenate([hi[:, sl], lo[:, sl]], axis=1), bd2, preferred_element_type=F32))
    return jnp.concatenate(parts, axis=1)


def _scan_parts(v, scan):
    return [jnp.dot(_split_cat(v[:, b * LANES:(b + 1) * LANES]), scan, preferred_element_type=F32)
            for b in range(v.shape[1] // LANES)]


def _chain_sums(parts, carry, reverse):
    nb = len(parts)
    outs = [None] * nb
    for b in (reversed(range(nb)) if reverse else range(nb)):
        outs[b] = carry + parts[b][:, :LANES]
        carry = carry + parts[b][:, LANES:]
    return jnp.concatenate(outs, axis=1), carry


def _log_weights(z, mask):
    sp = jnp.log(1.0 + jnp.exp(-jnp.abs(z)))
    log_not = -(jnp.maximum(z, 0.0) + sp)
    log_beta = z + log_not
    return (log_not if mask is None else jnp.where(mask, log_not, 0.0)), log_beta


def _pipeline(n_chains, stages):
    for step in range(n_chains + len(stages) - 1):
        for s, stage in enumerate(stages):
            if 0 <= step - s < n_chains:
                stage(step - s)


def _head_masks(tq):
    lane = lax.broadcasted_iota(jnp.int32, (tq, HEAD_PAIR), 1)
    return [(lane // HEAD_DIM) == hh for hh in range(2)]


def _diag_mask(tq):
    return lax.broadcasted_iota(jnp.int32, (tq, tq), 1) < lax.broadcasted_iota(jnp.int32, (tq, tq), 0)


def _attention_fwd(qn, kn, vb, name):
    t, d = qn.shape
    tq = min(ATT_TILE, t)
    nq, hp = t // tq, d // HEAD_PAIR
    scan_suffix = _scan_matrix("suffix", LANES)

    def body(q_ref, k_ref, v_ref, sc_ref, o_ref):
        qi = pl.program_id(1)
        q2 = q_ref[...].astype(F32)
        scan = sc_ref[...]
        heads, diag = _head_masks(tq), _diag_mask(tq)
        qms = [jnp.where(h, q2, 0.0).astype(BF16) for h in heads]

        def rows_of(j):
            return pl.ds(pl.multiple_of(j * tq, tq), tq)

        def step(js, state, mask):
            carry, acc = [state[0], state[2]], [state[1], state[3]]
            kts, vts = [k_ref[rows_of(j), :] for j in js], [v_ref[rows_of(j), :] for j in js]
            z, lw, parts, a = {}, {}, {}, {}

            def s_scores(c):
                z[c] = lax.dot_general(qms[c % 2], kts[c // 2], _NT, preferred_element_type=F32)

            def s_logs(c):
                lw[c] = _log_weights(z.pop(c), mask)

            def s_scan(c):
                parts[c] = _scan_parts(lw[c][0], scan)

            def s_weights(c):
                tail, carry[c % 2] = _chain_sums(parts.pop(c), carry[c % 2], True)
                av = jnp.exp(lw.pop(c)[1] + tail)
                a[c] = (av if mask is None else jnp.where(mask, av, 0.0)).astype(BF16)

            def s_values(c):
                acc[c % 2] = acc[c % 2] + jnp.dot(a.pop(c), vts[c // 2], preferred_element_type=F32)

            _pipeline(2 * len(js), [s_scores, s_logs, s_scan, s_weights, s_values])
            return (carry[0], acc[0], carry[1], acc[1])

        zero, zq = jnp.zeros((tq, LANES), F32), jnp.zeros((tq, HEAD_PAIR), F32)
        state = step([qi], (zero, zq, zero, zq), diag)
        rem = qi % ATT_UNROLL
        state = lax.fori_loop(0, rem, lambda p, c: step([qi - 1 - p], c, None), state)
        state = lax.fori_loop(0, qi // ATT_UNROLL,
                              lambda p, c: step([qi - 1 - rem - ATT_UNROLL * p - u for u in range(ATT_UNROLL)], c, None), state)
        o_ref[...] = jnp.where(heads[0], state[1], state[3]).astype(o_ref.dtype)

    return pl.pallas_call(
        body, name=name, grid=(hp, nq),
        in_specs=[pl.BlockSpec((tq, HEAD_PAIR), lambda h, i: (i, h)),
                  pl.BlockSpec((t, HEAD_PAIR), lambda h, i: (0, h)),
                  pl.BlockSpec((t, HEAD_PAIR), lambda h, i: (0, h)),
                  pl.BlockSpec(scan_suffix.shape, lambda h, i: (0, 0))],
        out_specs=pl.BlockSpec((tq, HEAD_PAIR), lambda h, i: (i, h)),
        out_shape=jax.ShapeDtypeStruct((t, d), BF16), compiler_params=_params(2))(qn, kn, vb, scan_suffix)


def _attention_bwd(qn, kn, vb, dob, name):
    t, d = qn.shape
    tq = min(ATT_TILE, t)
    nq, hp = t // tq, d // HEAD_PAIR
    scan_suffix, scan_prefix = _scan_matrix("suffix", LANES), _scan_matrix("prefix", LANES)

    def body(q_ref, k_ref, v_ref, do_ref, ss_ref, sp_ref, dq_ref, dk_ref, dv_ref, g_s, b_s):
        qi = pl.program_id(1)

        @pl.when(qi == 0)
        def _():
            dk_ref[...] = jnp.zeros(dk_ref.shape, F32)
            dv_ref[...] = jnp.zeros(dv_ref.shape, F32)

        q2, do2 = q_ref[...].astype(F32), do_ref[...].astype(F32)
        ssuf, spre = ss_ref[...], sp_ref[...]
        heads, diag = _head_masks(tq), _diag_mask(tq)
        qms = [jnp.where(h, q2, 0.0).astype(BF16) for h in heads]
        doms = [jnp.where(h, do2, 0.0).astype(BF16) for h in heads]

        def rows_of(j):
            return pl.ds(pl.multiple_of(j * tq, tq), tq)

        def pass_one(js, carries, mask):
            carry = list(carries)
            kts, vts = [k_ref[rows_of(j), :] for j in js], [v_ref[rows_of(j), :] for j in js]
            z, d_a, lw, parts, a, dv = {}, {}, {}, {}, {}, {}

            def s_scores(c):
                z[c] = lax.dot_general(qms[c % 2], kts[c // 2], _NT, preferred_element_type=F32)
                d_a[c] = lax.dot_general(doms[c % 2], vts[c // 2], _NT, preferred_element_type=F32)

            def s_logs(c):
                lw[c] = _log_weights(z.pop(c), mask)

            def s_scan(c):
                parts[c] = _scan_parts(lw[c][0], ssuf)

            def s_weights(c):
                tail, carry[c % 2] = _chain_sums(parts.pop(c), carry[c % 2], True)
                log_beta = lw.pop(c)[1]
                av, beta = jnp.exp(log_beta + tail), jnp.exp(log_beta)
                if mask is not None:
                    av, beta = jnp.where(mask, av, 0.0), jnp.where(mask, beta, 0.0)
                g_s[c % 2, js[c // 2]] = av * d_a.pop(c)
                b_s[c % 2, js[c // 2]] = beta
                a[c] = av.astype(BF16)

            def s_values(c):
                dv[c] = lax.dot_general(a.pop(c), doms[c % 2], _TN, preferred_element_type=F32)
                if c % 2 == 1:
                    dv_ref[rows_of(js[c // 2]), :] += dv.pop(c - 1) + dv.pop(c)

            _pipeline(2 * len(js), [s_scores, s_logs, s_scan, s_weights, s_values])
            return tuple(carry)

        zero = jnp.zeros((tq, LANES), F32)
        rem = qi % ATT_UNROLL
        carries = pass_one([qi], (zero, zero), diag)
        carries = lax.fori_loop(0, rem, lambda p, c: pass_one([qi - 1 - p], c, None), carries)
        lax.fori_loop(0, qi // ATT_UNROLL,
                      lambda p, c: pass_one([qi - 1 - rem - ATT_UNROLL * p - u for u in range(ATT_UNROLL)], c, None), carries)

        def pass_two(js, state):
            prefix, dq = [state[0], state[2]], [state[1], state[3]]
            kts = [k_ref[rows_of(j), :] for j in js]
            gb, parts, dz, dk = {}, {}, {}, {}

            def s_scan(c):
                gb[c] = (g_s[c % 2, js[c // 2]], b_s[c % 2, js[c // 2]])
                parts[c] = _scan_parts(gb[c][0], spre)

            def s_dz(c):
                before, prefix[c % 2] = _chain_sums(parts.pop(c), prefix[c % 2], False)
                g, beta = gb.pop(c)
                dz[c] = (g - beta * (g + before)).astype(BF16)

            def s_grads(c):
                dzc = dz.pop(c)
                dq[c % 2] = dq[c % 2] + jnp.dot(dzc, kts[c // 2], preferred_element_type=F32)
                dk[c] = lax.dot_general(dzc, qms[c % 2], _TN, preferred_element_type=F32)
                if c % 2 == 1:
                    dk_ref[rows_of(js[c // 2]), :] += dk.pop(c - 1) + dk.pop(c)

            _pipeline(2 * len(js), [s_scan, s_dz, s_grads])
            return (prefix[0], dq[0], prefix[1], dq[1])

        zq = jnp.zeros((tq, HEAD_PAIR), F32)
        first = (qi + 1) % ATT_UNROLL
        out = lax.fori_loop(0, first, lambda p, c: pass_two([p], c), (zero, zq, zero, zq))
        out = lax.fori_loop(0, (qi + 1) // ATT_UNROLL,
                            lambda p, c: pass_two([first + ATT_UNROLL * p + u for u in range(ATT_UNROLL)], c), out)
        dq_ref[...] = jnp.where(heads[0], out[1], out[3])

    blk = pl.BlockSpec((tq, HEAD_PAIR), lambda h, i: (i, h))
    col_spec = pl.BlockSpec((t, HEAD_PAIR), lambda h, i: (0, h))
    const = pl.BlockSpec(scan_suffix.shape, lambda h, i: (0, 0))
    full = jax.ShapeDtypeStruct((t, d), F32)
    return pl.pallas_call(
        body, name=name, grid=(hp, nq), in_specs=[blk, col_spec, col_spec, blk, const, const],
        out_specs=[blk, col_spec, col_spec], out_shape=[full, full, full],
        scratch_shapes=[pltpu.VMEM((2, nq, tq, tq), F32), pltpu.VMEM((2, nq, tq, tq), F32)],
        compiler_params=_params(2))(qn, kn, vb, dob, scan_suffix, scan_prefix)


def _chip_at(x, y, j):
    return (1 - x if j & 2 else x, 1 - y if j & 1 else y)


def _block_of(ref, dev, shape, axis):
    if axis is None:
        return ref.at[dev]
    if axis == 0:
        return ref.at[pl.ds(pl.multiple_of(dev * shape[0], shape[0]), shape[0]), :]
    return ref.at[:, pl.ds(pl.multiple_of(dev * shape[1], shape[1]), shape[1])]


def _gathered_shape(shape, axis):
    if axis is None:
        return (N_DEV,) + shape
    return (N_DEV * shape[0], shape[1]) if axis == 0 else (shape[0], N_DEV * shape[1])


def _all_gather(shards, name, axes=None):
    n = len(shards)
    axes = [None] * n if axes is None else axes

    def body(*refs):
        x_refs, out_refs = refs[:n], refs[n:2 * n]
        send_sems, recv_sems, local_sems = refs[2 * n:]
        x, y, c = lax.axis_index("x"), lax.axis_index("y"), lax.axis_index("c")
        me, sibling = (x, y, c), (x, y, 1 - c)
        chips = [_chip_at(x, y, j) for j in (1, 2, 3)]

        def rows(t, px, py, pc):
            return _block_of(out_refs[t], 4 * px + 2 * py + pc, shards[t].shape, axes[t])

        def copy(t, k, block, to, src=None):
            return pltpu.make_async_remote_copy(src_ref=rows(t, *block) if src is None else src, dst_ref=rows(t, *block),
                                                send_sem=send_sems.at[t, k], recv_sem=recv_sems.at[t, k],
                                                device_id=to, device_id_type=MESH)

        mine = [pltpu.make_async_copy(x_refs[t], rows(t, *me), local_sems.at[t]) for t in range(n)]
        for cp in mine:
            cp.start()
        first = []
        for t in range(n):
            first.append(copy(t, 0, me, sibling, src=x_refs[t]))
            first += [copy(t, 1 + j, me, (*chip, c), src=x_refs[t]) for j, chip in enumerate(chips)]
        for cp in first:
            cp.start()
        passed = []
        for t in range(n):
            for j, chip in enumerate(chips):
                copy(t, 1 + j, (*chip, c), me).wait_recv()
                passed.append(copy(t, 4 + j, (*chip, c), sibling))
                passed[-1].start()
        for t in range(n):
            copy(t, 0, sibling, me).wait_recv()
            for j, chip in enumerate(chips):
                copy(t, 4 + j, (*chip, 1 - c), me).wait_recv()
        for cp in first + passed:
            cp.wait_send()
        for cp in mine:
            cp.wait()

    any_spec = pl.BlockSpec(memory_space=pl.ANY)
    return pl.pallas_call(
        body, name=name, out_shape=[jax.ShapeDtypeStruct(_gathered_shape(a.shape, ax), a.dtype) for a, ax in zip(shards, axes)],
        in_specs=[any_spec] * n, out_specs=[any_spec] * n,
        scratch_shapes=[pltpu.SemaphoreType.DMA((n, 7)), pltpu.SemaphoreType.DMA((n, 7)), pltpu.SemaphoreType.DMA((n,))])(*shards)


def _place_own(shards, axes, name):
    n = len(shards)

    def body(*refs):
        x_refs, out_refs, sems = refs[:n], refs[n:2 * n], refs[2 * n]
        dev = 4 * lax.axis_index("x") + 2 * lax.axis_index("y") + lax.axis_index("c")
        copies = [pltpu.make_async_copy(x_refs[t], _block_of(out_refs[t], dev, shards[t].shape, axes[t]), sems.at[t]) for t in range(n)]
        for cp in copies:
            cp.start()
        for cp in copies:
            cp.wait()

    any_spec = pl.BlockSpec(memory_space=pl.ANY)
    return pl.pallas_call(
        body, name=name, out_shape=[jax.ShapeDtypeStruct(_gathered_shape(a.shape, ax), a.dtype) for a, ax in zip(shards, axes)],
        in_specs=[any_spec] * n, out_specs=[any_spec] * n, scratch_shapes=[pltpu.SemaphoreType.DMA((n,))])(*shards)


def _exchange_sibling(tensors, name):
    n = len(tensors)

    def body(*refs):
        g_refs, recv_refs = refs[:n], refs[n:2 * n]
        send_sems, recv_sems = refs[2 * n:]
        x, y, c = lax.axis_index("x"), lax.axis_index("y"), lax.axis_index("c")
        copies = []
        for t in range(n):
            for j in range(4):
                cx, cy = _chip_at(x, y, j)
                copies.append(pltpu.make_async_remote_copy(
                    src_ref=g_refs[t].at[4 * cx + 2 * cy + (1 - c)], dst_ref=recv_refs[t].at[j], send_sem=send_sems.at[t, j],
                    recv_sem=recv_sems.at[t, j], device_id=(x, y, 1 - c), device_id_type=MESH))
        for cp in copies:
            cp.start()
        for cp in copies:
            cp.wait_recv()
        for cp in copies:
            cp.wait_send()

    any_spec = pl.BlockSpec(memory_space=pl.ANY)
    return pl.pallas_call(
        body, name=name, out_shape=[jax.ShapeDtypeStruct((4,) + a.shape[1:], a.dtype) for a in tensors],
        in_specs=[any_spec] * n, out_specs=[any_spec] * n,
        scratch_shapes=[pltpu.SemaphoreType.DMA((n, 4)), pltpu.SemaphoreType.DMA((n, 4))])(*tensors)


def _exchange_chips(parts, name):
    n = len(parts)

    def body(*refs):
        p_refs, recv_refs = refs[:n], refs[n:2 * n]
        send_sems, recv_sems = refs[2 * n:]
        x, y, c = lax.axis_index("x"), lax.axis_index("y"), lax.axis_index("c")
        copies = []
        for t in range(n):
            for j in (1, 2, 3):
                copies.append(pltpu.make_async_remote_copy(
                    src_ref=p_refs[t].at[j], dst_ref=recv_refs[t].at[j - 1], send_sem=send_sems.at[t, j - 1],
                    recv_sem=recv_sems.at[t, j - 1], device_id=(*_chip_at(x, y, j), c), device_id_type=MESH))
        for cp in copies:
            cp.start()
        for cp in copies:
            cp.wait_recv()
        for cp in copies:
            cp.wait_send()

    any_spec = pl.BlockSpec(memory_space=pl.ANY)
    return pl.pallas_call(
        body, name=name, out_shape=[jax.ShapeDtypeStruct((3,) + a.shape[1:], a.dtype) for a in parts],
        in_specs=[any_spec] * n, out_specs=[any_spec] * n,
        scratch_shapes=[pltpu.SemaphoreType.DMA((n, 3)), pltpu.SemaphoreType.DMA((n, 3))])(*parts)


_HBM = pl.BlockSpec(memory_space=pltpu.HBM)
_SEM = pl.BlockSpec(memory_space=pltpu.SEMAPHORE)
_DATAFLOW = pltpu.SideEffectType.DATAFLOW_SIDE_EFFECTING


def _peer(x, y, c, mask):
    return (1 - x if mask & 4 else x, 1 - y if mask & 2 else y, 1 - c if mask & 1 else c)


def _gather_copies(x_refs, land_refs, send_sems, recv_sems, waiting, shapes, axes):
    x, y, c = lax.axis_index("x"), lax.axis_index("y"), lax.axis_index("c")
    copies = []
    for t in range(len(x_refs)):
        for mask in range(1, N_DEV):
            px, py, pc = _peer(x, y, c, mask)
            block = 4 * px + 2 * py + pc if waiting else 4 * x + 2 * y + c
            copies.append(pltpu.make_async_remote_copy(
                src_ref=x_refs[t], dst_ref=_block_of(land_refs[t], block, shapes[t], axes[t]), send_sem=send_sems.at[7 * t + mask - 1],
                recv_sem=recv_sems.at[7 * t + mask - 1], device_id=(px, py, pc), device_id_type=MESH))
    return copies


def _chip_copies(p_refs, land_refs, send_sems, recv_sems, waiting):
    x, y, c = lax.axis_index("x"), lax.axis_index("y"), lax.axis_index("c")
    return [pltpu.make_async_remote_copy(src_ref=p_refs[t].at[j], dst_ref=land_refs[t].at[j - 1], send_sem=send_sems.at[3 * t + j - 1],
                                         recv_sem=recv_sems.at[3 * t + j - 1], device_id=(*_chip_at(x, y, j), c), device_id_type=MESH)
            for t in range(len(p_refs)) for j in (1, 2, 3)]


def _copies_start(srcs, land_shapes, per_src, copies, name, lands=None):
    n = len(srcs)

    def body(*refs):
        for cp in copies(refs[:n], refs[n:2 * n], refs[2 * n], refs[2 * n + 1], False):
            cp.start()
        refs[-1][...] = jnp.zeros(refs[-1].shape, F32)

    if lands is None:
        lands = [lax.empty(shape, a.dtype) for shape, a in zip(land_shapes, srcs)]
    args =[pltpu.with_memory_space_constraint(a, pltpu.HBM) for a in (*srcs, *lands)]
    sems = pltpu.SemaphoreType.DMA((n * per_src,))
    res = pl.pallas_call(
        body, name=name, out_shape=(sems, sems, *[pltpu.HBM(a.shape, a.dtype) for a in args], jax.ShapeDtypeStruct((8, LANES), F32)),
        in_specs=[_HBM] * (2 * n), out_specs=(_SEM, _SEM, *[_HBM] * (2 * n), pl.BlockSpec(memory_space=pltpu.VMEM)),
        input_output_aliases={i: 2 + i for i in range(2 * n)}, compiler_params=pltpu.CompilerParams(has_side_effects=_DATAFLOW))(*args)
    return res[0], res[1], list(res[2:2 + n]), list(res[2 + n:2 + 2 * n]), res[-1]


def _copies_wait(send_sems, recv_sems, srcs, lands, after, copies, name):
    n = len(srcs)

    def body(*refs):
        for cp in copies(refs[:n], refs[n:2 * n], refs[2 * n], refs[2 * n + 1], True):
            cp.wait_send()
            cp.wait_recv()

    res = pl.pallas_call(
        body, name=name, out_shape=tuple(pltpu.HBM(a.shape, a.dtype) for a in (*srcs, *lands)),
        in_specs=[_HBM] * (2 * n) + [_SEM, _SEM, pl.BlockSpec(memory_space=pl.ANY)], out_specs=tuple([_HBM] * (2 * n)),
        input_output_aliases={i: i for i in range(2 * n)}, compiler_params=pltpu.CompilerParams(has_side_effects=_DATAFLOW))(
            *srcs, *lands, send_sems, recv_sems, after)
    return list(res[:n]), list(res[n:])


def _add_sibling(g8, recv1, name):
    _, a, b = g8.shape
    ta = next(cand for cand in (512, 384, 256, 128) if a % cand == 0)

    def body(g_ref, r_ref, o_ref):
        o_ref[...] = (g_ref[...].astype(F32) + r_ref[...].astype(F32)).astype(o_ref.dtype)

    def own(j, i):
        x, y, c = lax.axis_index("x"), lax.axis_index("y"), lax.axis_index("c")
        return (4 * (x ^ (j >> 1)) + 2 * (y ^ (j & 1)) + c, i, 0)

    slot = pl.BlockSpec((None, ta, b), lambda j, i: (j, i, 0))
    return pl.pallas_call(body, name=name, grid=(4, a // ta), in_specs=[pl.BlockSpec((None, ta, b), own), slot],
                          out_specs=slot, out_shape=jax.ShapeDtypeStruct((4, a, b), g8.dtype),
                          compiler_params=_params(2))(g8, recv1)


def _ada_forward(c_all, ada_w, ada_b_cols):
    depth, d, cols = ada_w.shape

    def body(c_ref, w_ref, b_ref, o_ref):
        cv = c_ref[...]
        act = cv * _sigmoid(cv)
        o_ref[...] = jnp.dot(act, w_ref[...], preferred_element_type=F32, precision=lax.Precision.HIGHEST) + b_ref[...]

    return pl.pallas_call(
        body, name="ada_forward", grid=(depth,),
        in_specs=[pl.BlockSpec((N_DEV, d), lambda l: (0, 0)), pl.BlockSpec((None, d, cols), lambda l: (l, 0, 0)),
                  pl.BlockSpec((None, 1, cols), lambda l: (l, 0, 0))],
        out_specs=pl.BlockSpec((None, N_DEV, cols), lambda l: (l, 0, 0)),
        out_shape=jax.ShapeDtypeStruct((depth, N_DEV, cols), F32), compiler_params=_params(1))(
            c_all, ada_w, ada_b_cols.reshape(depth, 1, cols))


def _ada_backward(c_all, dmod_cols):
    depth, _, cols = dmod_cols.shape
    d = c_all.shape[1]

    def body(c_ref, g_ref, o_ref):
        cv = c_ref[...]
        act = cv * _sigmoid(cv)
        o_ref[...] = lax.dot_general(act, g_ref[...], _TN, preferred_element_type=F32, precision=lax.Precision.HIGHEST)

    return pl.pallas_call(
        body, name="ada_backward", grid=(depth,),
        in_specs=[pl.BlockSpec((N_DEV, d), lambda l: (0, 0)), pl.BlockSpec((None, N_DEV, cols), lambda l: (l, 0, 0))],
        out_specs=pl.BlockSpec((None, d, cols), lambda l: (l, 0, 0)),
        out_shape=jax.ShapeDtypeStruct((depth, d, cols), F32), compiler_params=_params(1))(c_all, dmod_cols)


def _device_sum(a):
    _, r, w = a.shape

    def body(a_ref, o_ref):
        acc = a_ref[0]
        for dev in range(1, N_DEV):
            acc = acc + a_ref[dev]
        o_ref[...] = acc

    return pl.pallas_call(body, name="device_sum", out_shape=jax.ShapeDtypeStruct((r, w), F32),
                          in_specs=[pl.BlockSpec(memory_space=pltpu.VMEM)], out_specs=pl.BlockSpec(memory_space=pltpu.VMEM))(a)


def _adamw(w, g_layers, m, v, name):
    shape = w.shape
    cols = shape[-1]
    rows = int(np.prod(shape[:-1]))
    layers = len(g_layers)
    per_layer = rows // layers
    tile = next((cand for cand in (512, 352, 256, 128) if per_layer % cand == 0), per_layer)
    n_l = per_layer // tile
    n_parts = len(g_layers[0])

    def fn(pid, tv, hv, bv):
        wv, mv, vv = tv[:3]
        gv = None
        for l in range(layers):
            gl = None
            for part in tv[3 + l * n_parts:3 + (l + 1) * n_parts]:
                gl = part.astype(F32) if gl is None else gl + part.astype(F32)
            gv = gl if gv is None else jnp.where(pid >= l * n_l, gl, gv)
        mn = ADAM_B1 * mv + (1.0 - ADAM_B1) * gv
        vn = ADAM_B2 * vv + (1.0 - ADAM_B2) * (gv * gv)
        m_hat = mn / (1.0 - ADAM_B1 ** ADAM_STEP)
        v_hat = vn / (1.0 - ADAM_B2 ** ADAM_STEP)
        delta = -ADAM_LR * (m_hat / (jnp.sqrt(v_hat) + ADAM_EPS) + ADAM_WD * wv)
        return [gv, delta, mn, vn], []

    tiled = [(a.reshape(rows, cols), 0, cols) for a in (w, m, v)]
    for l, parts in enumerate(g_layers):
        clamp = functools.partial(lambda i, l: jnp.clip(i - l * n_l, 0, n_l - 1), l=l)
        tiled += [(p.reshape(per_layer, cols), 0, cols, clamp) for p in parts]
    res = _rowwise(fn, name, rows=rows, tile=tile, tiled=tiled, outs=[(cols, F32)] * 4)
    return [r.reshape(shape) for r in res]


def _norm_modulate(x, y, gate, ln_g, scale, shift, name, tile):
    t, d = x.shape

    def fn(pid, tv, hv, bv):
        if y is None:
            xn = tv[0]
            g_ln, sc, sh = bv
        else:
            g_gate, g_ln, sc, sh = bv
            xn = tv[0] + g_gate * tv[1]
        r = lax.rsqrt(jnp.mean(xn * xn, axis=-1, keepdims=True) + EPS)
        h = (xn * r * g_ln) * (1.0 + sc) + sh
        return ([h] if y is None else [xn, h]), []

    if y is None:
        h, = _rowwise(fn, name, rows=t, tile=tile, tiled=[(x, 0, d)], bcast=[ln_g, scale, shift], outs=[(d, BF16)])
        return x, h
    xn, h = _rowwise(fn, name, rows=t, tile=tile, tiled=[(x, 0, d), (y, 0, d)], bcast=[gate, ln_g, scale, shift],
                     outs=[(d, F32), (d, BF16)])
    return xn, h


def _norm_backward(dh, x, dres, ln_g, scale, name, tile):
    t, d = x.shape

    def fn(pid, tv, hv, bv):
        dhv, xv, dr = tv
        g_ln, sc = bv
        r = lax.rsqrt(jnp.mean(xv * xv, axis=-1, keepdims=True) + EPS)
        xn = xv * r
        dxn = dhv * (1.0 + sc) * g_ln
        dx = dr + r * (dxn - xn * jnp.mean(dxn * xn, axis=-1, keepdims=True))
        return [dx], [_colsum(dhv), _colsum(dhv * (xn * g_ln)), _colsum(dhv * (1.0 + sc) * xn)]

    return _rowwise(fn, name, rows=t, tile=tile, tiled=[(dh, 0, d), (x, 0, d), (dres, 0, d)], bcast=[ln_g, scale],
                    outs=[(d, F32)], accs=[(1, d)] * 3)


def _shift_rows(u, halo, k, pid, first_tile_zero):
    rows = lax.broadcasted_iota(jnp.int32, u.shape, 0)
    halo = halo * jnp.where(pid == 0, 0.0, 1.0) if first_tile_zero else halo
    out = pltpu.roll(u, k, axis=0)
    for j in range(k):
        out = jnp.where(rows == j, halo[8 - k + j:8 - k + j + 1, :], out)
    return out


def _shift_rows_up(u, halo, k, pid, n_tiles):
    tile = u.shape[0]
    rows = lax.broadcasted_iota(jnp.int32, u.shape, 0)
    halo = halo * jnp.where(pid == n_tiles - 1, 0.0, 1.0)
    out = pltpu.roll(u, tile - k, axis=0)
    for j in range(k):
        out = jnp.where(rows == tile - k + j, halo[j:j + 1, :], out)
    return out


def _layer_forward(x_in, y_prev, gate_prev, mod, w, l, tile):
    sh1, sc1, g1, sh2, sc2, g2 = mod
    t, d = x_in.shape
    f = w["w_g"].shape[1]
    bd2 = _head_sum_matrix()
    s = {}
    s["x"], s["h"] = _norm_modulate(x_in, y_prev, gate_prev, w["ln1"], sc1, sh1, f"l{l}_norm1", tile)
    p = _matmul([(s["h"], 0, w["w_in"], 0)], "nn", m=t, n=8 * d, k=d, tm=1024, tn=512, out_dtype=F32, name=f"l{l}_in_proj")
    s["p"] = p

    def qk_norm(pid, tv, hv, bv):
        qr, kr, vr = tv
        qg, kg, bd = bv

        def nrm(xv, g):
            r = lax.rsqrt(_head_sums(xv * xv, bd) * (1.0 / HEAD_DIM) + EPS)
            return xv * r * g
        return [nrm(qr, qg) * 0.125, nrm(kr, kg), vr], []

    s["qn"], s["kn"], s["vb"] = _rowwise(qk_norm, f"l{l}_qk_norm", rows=t, tile=tile, tiled=[(p, 0, d), (p, 1, d), (p, 2, d)],
                                         bcast=[w["qg"], w["kg"], bd2], outs=[(d, BF16)] * 3)
    s["ya"] = _attention_fwd(s["qn"], s["kn"], s["vb"], f"l{l}_attention")

    def conv_fwd(pid, tv, hv, bv):
        cb, cc, cx = tv
        u = cc * cx
        hu = hv[0] * hv[1]
        cw = bv[0]
        conv = cw[0:1, :] * _shift_rows(u, hu, 2, pid, True) + cw[1:2, :] * _shift_rows(u, hu, 1, pid, True) + cw[2:3, :] * u
        return [cb * conv], []

    s["yb"], = _rowwise(conv_fwd, f"l{l}_conv", rows=t, tile=tile, tiled=[(p, 3, d), (p, 4, d), (p, 5, d)],
                        halos=[(p, 4, d, "prev"), (p, 5, d, "prev")], bcast=[w["conv"]], outs=[(d, BF16)])
    s["a"] = _matmul([(s["ya"], 0, w["w_a"], 0)], "nn", m=t, n=d, k=d, tm=1024, tn=512, out_dtype=F32, name=f"l{l}_branch_a")
    s["b"] = _matmul([(s["yb"], 0, w["w_b"], 0)], "nn", m=t, n=d, k=d, tm=1024, tn=512, out_dtype=F32, name=f"l{l}_branch_b")

    def merge(pid, tv, hv, bv):
        av, bvv, ga, gb = tv
        return [_sigmoid(ga) * av + _sigmoid(gb) * bvv], []

    s["merged"], = _rowwise(merge, f"l{l}_merge", rows=t, tile=tile, tiled=[(s["a"], 0, d), (s["b"], 0, d), (p, 6, d), (p, 7, d)],
                            outs=[(d, BF16)])
    s["mo"] = _matmul([(s["merged"], 0, w["w_o"], 0)], "nn", m=t, n=d, k=d, tm=1024, tn=512, out_dtype=F32, name=f"l{l}_out_proj")
    s["x2"], s["h2"] = _norm_modulate(s["x"], s["mo"], g1, w["ln2"], sc2, sh2, f"l{l}_norm2", tile)
    fn_tile = f // 2
    s["g"] = _matmul([(s["h2"], 0, w["w_g"], 0)], "nn", m=t, n=f, k=d, tm=512, tn=fn_tile, out_dtype=F32, name=f"l{l}_ffn_gate")
    s["u"] = _matmul([(s["h2"], 0, w["w_u"], 0)], "nn", m=t, n=f, k=d, tm=512, tn=fn_tile, out_dtype=F32, name=f"l{l}_ffn_up")

    def swiglu(pid, tv, hv, bv):
        gv, uv = tv
        return [gv * _sigmoid(gv) * uv], []

    s["s"], = _rowwise(swiglu, f"l{l}_swiglu", rows=t, tile=tile // 2, tiled=[(s["g"], 0, f), (s["u"], 0, f)], outs=[(f, BF16)])
    s["f"] = _matmul([(s["s"], 0, w["w_d"], 0)], "nn", m=t, n=d, k=f, tm=512, tn=512, out_dtype=F32, name=f"l{l}_ffn_down")
    return s


def _layer_backward(dx3, s, mod, w, l, tile):
    sh1, sc1, g1, sh2, sc2, g2 = mod
    t, d = dx3.shape
    f = w["w_g"].shape[1]
    p = s["p"]
    bd2 = _head_sum_matrix()
    n_tiles = t // tile
    grads = {}

    def gate_bwd(pid, tv, hv, bv):
        return [tv[0] * bv[0]], [_colsum(tv[0] * tv[1])]

    df, dg2 = _rowwise(gate_bwd, f"l{l}_bwd_gate2", rows=t, tile=tile, tiled=[(dx3, 0, d), (s["f"], 0, d)], bcast=[g2],
                       outs=[(d, BF16)], accs=[(1, d)])
    grads["w_d"] = _matmul([(s["s"], 0, df, 0)], "tn", m=f, n=d, k=t, tm=512, tn=512, out_dtype=BF16, name=f"l{l}_dw_down")
    ds = _matmul([(df, 0, w["w_d"], 0)], "nt", m=t, n=f, k=d, tm=512, tn=f // 2, out_dtype=F32, name=f"l{l}_d_swiglu")

    def swiglu_bwd(pid, tv, hv, bv):
        dsv, gv, uv = tv
        sig = _sigmoid(gv)
        return [dsv * uv * (sig * (1.0 + gv * (1.0 - sig))), dsv * (gv * sig)], []

    dgt, dup = _rowwise(swiglu_bwd, f"l{l}_bwd_swiglu", rows=t, tile=tile // 2, tiled=[(ds, 0, f), (s["g"], 0, f), (s["u"], 0, f)],
                        outs=[(f, BF16)] * 2)
    grads["w_g"] = _matmul([(s["h2"], 0, dgt, 0)], "tn", m=d, n=f, k=t, tm=512, tn=f // 2, out_dtype=BF16, name=f"l{l}_dw_gate")
    grads["w_u"] = _matmul([(s["h2"], 0, dup, 0)], "tn", m=d, n=f, k=t, tm=512, tn=f // 2, out_dtype=BF16, name=f"l{l}_dw_up")
    dh2 = _matmul([(dgt, 0, w["w_g"], 0), (dup, 0, w["w_u"], 0)], "nt", m=t, n=d, k=f, tm=512, tn=512, out_dtype=F32,
                  name=f"l{l}_dh2")
    dx2, dsh2, dsc2, grads["ln2"] = _norm_backward(dh2, s["x2"], dx3, w["ln2"], sc2, f"l{l}_bwd_norm2", tile)

    dmo, dg1 = _rowwise(gate_bwd, f"l{l}_bwd_gate1", rows=t, tile=tile, tiled=[(dx2, 0, d), (s["mo"], 0, d)], bcast=[g1],
                        outs=[(d, BF16)], accs=[(1, d)])
    grads["w_o"] = _matmul([(s["merged"], 0, dmo, 0)], "tn", m=d, n=d, k=t, tm=512, tn=512, out_dtype=BF16, name=f"l{l}_dw_out")
    dmerged = _matmul([(dmo, 0, w["w_o"], 0)], "nt", m=t, n=d, k=d, tm=1024, tn=512, out_dtype=F32, name=f"l{l}_d_merged")

    def merge_bwd(pid, tv, hv, bv):
        dm, av, bvv, ga, gb = tv
        sa, sb = _sigmoid(ga), _sigmoid(gb)
        return [dm * sa, dm * sb, dm * av * (sa * (1.0 - sa)), dm * bvv * (sb * (1.0 - sb))], []

    d_a, d_b, dga, dgb = _rowwise(merge_bwd, f"l{l}_bwd_merge", rows=t, tile=tile,
                                  tiled=[(dmerged, 0, d), (s["a"], 0, d), (s["b"], 0, d), (p, 6, d), (p, 7, d)], outs=[(d, BF16)] * 4)
    grads["w_a"] = _matmul([(s["ya"], 0, d_a, 0)], "tn", m=d, n=d, k=t, tm=512, tn=512, out_dtype=BF16, name=f"l{l}_dw_a")
    grads["w_b"] = _matmul([(s["yb"], 0, d_b, 0)], "tn", m=d, n=d, k=t, tm=512, tn=512, out_dtype=BF16, name=f"l{l}_dw_b")
    dya = _matmul([(d_a, 0, w["w_a"], 0)], "nt", m=t, n=d, k=d, tm=1024, tn=512, out_dtype=BF16, name=f"l{l}_d_ya")
    dyb = _matmul([(d_b, 0, w["w_b"], 0)], "nt", m=t, n=d, k=d, tm=1024, tn=512, out_dtype=F32, name=f"l{l}_d_yb")

    def conv_bwd(pid, tv, hv, bv):
        dy, cb, cc, cx = tv
        cw = bv[0]
        u, hu = cc * cx, hv[0] * hv[1]
        u1, u2 = _shift_rows(u, hu, 1, pid, True), _shift_rows(u, hu, 2, pid, True)
        conv = cw[0:1, :] * u2 + cw[1:2, :] * u1 + cw[2:3, :] * u
        dconv, hd = dy * cb, hv[2] * hv[3]
        du = (cw[2:3, :] * dconv + cw[1:2, :] * _shift_rows_up(dconv, hd, 1, pid, n_tiles)
              + cw[0:1, :] * _shift_rows_up(dconv, hd, 2, pid, n_tiles))
        return [dy * conv, du * cx, du * cc], [_colsum(dconv * u2), _colsum(dconv * u1), _colsum(dconv * u)]

    dcb, dcc, dcx, dcw0, dcw1, dcw2 = _rowwise(
        conv_bwd, f"l{l}_bwd_conv", rows=t, tile=tile, tiled=[(dyb, 0, d), (p, 3, d), (p, 4, d), (p, 5, d)],
        halos=[(p, 4, d, "prev"), (p, 5, d, "prev"), (dyb, 0, d, "next"), (p, 3, d, "next")], bcast=[w["conv"]],
        outs=[(d, BF16)] * 3, accs=[(1, d)] * 3)
    grads["conv"] = jnp.concatenate([dcw0, dcw1, dcw2], axis=0)

    dqs, dkn, dv = _attention_bwd(s["qn"], s["kn"], s["vb"], dya, f"l{l}_bwd_attention")

    def qk_norm_bwd(pid, tv, hv, bv):
        dq, dk, qr, kr, dvv = tv
        qg, kg, bd = bv

        def bwd(dy, xv, g):
            r = lax.rsqrt(_head_sums(xv * xv, bd) * (1.0 / HEAD_DIM) + EPS)
            yv = xv * r
            dyn = dy * g
            dx = r * (dyn - yv * (_head_sums(dyn * yv, bd) * (1.0 / HEAD_DIM)))
            return dx, _colsum(dy * yv)

        dxq, dgq = bwd(dq * 0.125, qr, qg)
        dxk, dgk = bwd(dk, kr, kg)
        return [dxq, dxk, dvv], [dgq, dgk]

    dqr, dkr, dvb, grads["qg"], grads["kg"] = _rowwise(
        qk_norm_bwd, f"l{l}_bwd_qk_norm", rows=t, tile=tile, tiled=[(dqs, 0, d), (dkn, 0, d), (p, 0, d), (p, 1, d), (dv, 0, d)],
        bcast=[w["qg"], w["kg"], bd2], outs=[(d, BF16)] * 3, accs=[(1, d)] * 2)

    dp = [dqr, dkr, dvb, dcb, dcc, dcx, dga, dgb]
    grads["w_in"] = [_matmul([(s["h"], 0, dpk, 0)], "tn", m=d, n=d, k=t, tm=512, tn=512, out_dtype=BF16, name=f"l{l}_dw_in{k}")
                     for k, dpk in enumerate(dp)]
    dh = _matmul([(dpk, 0, w["w_in"], k) for k, dpk in enumerate(dp)], "nt", m=t, n=d, k=d, tm=512, tn=512, out_dtype=F32,
                 name=f"l{l}_dh")
    dx, dsh1, dsc1, grads["ln1"] = _norm_backward(dh, s["x"], dx2, w["ln1"], sc1, f"l{l}_bwd_norm1", tile)
    return dx, grads, [dsh1, dsc1, dg1, dsh2, dsc2, dg2]


_BIG = ["w_in", "w_branch_a", "w_branch_b", "w_out", "w_ffn_gate", "w_ffn_up", "w_ffn_down"]
_SHORT = dict(w_in="w_in", w_branch_a="w_a", w_branch_b="w_b", w_out="w_o", w_ffn_gate="w_g", w_ffn_up="w_u", w_ffn_down="w_d")
_COL_SHARDED = {"w_in", "w_ffn_gate", "w_ffn_up"}


def kernel(x, c, ada_w, ada_b, ln1_g, w_in, q_norm_g, k_norm_g, conv_w, w_branch_a, w_branch_b, w_out, ln2_g, w_ffn_gate, w_ffn_up, w_ffn_down, loss_target, m_ada_w, m_ada_b, m_ln1_g, m_w_in, m_q_norm_g, m_k_norm_g, m_conv_w, m_w_branch_a, m_w_branch_b, m_w_out, m_ln2_g, m_w_ffn_gate, m_w_ffn_up, m_w_ffn_down, v_ada_w, v_ada_b, v_ln1_g, v_w_in, v_q_norm_g, v_k_norm_g, v_conv_w, v_w_branch_a, v_w_branch_b, v_w_out, v_ln2_g, v_w_ffn_gate, v_w_ffn_up, v_w_ffn_down):
    weights = dict(ada_w=ada_w, ada_b=ada_b, ln1_g=ln1_g, w_in=w_in, q_norm_g=q_norm_g, k_norm_g=k_norm_g, conv_w=conv_w,
                   w_branch_a=w_branch_a, w_branch_b=w_branch_b, w_out=w_out, ln2_g=ln2_g, w_ffn_gate=w_ffn_gate,
                   w_ffn_up=w_ffn_up, w_ffn_down=w_ffn_down)
    m_in = dict(ada_w=m_ada_w, ada_b=m_ada_b, ln1_g=m_ln1_g, w_in=m_w_in, q_norm_g=m_q_norm_g, k_norm_g=m_k_norm_g,
                conv_w=m_conv_w, w_branch_a=m_w_branch_a, w_branch_b=m_w_branch_b, w_out=m_w_out, ln2_g=m_ln2_g,
                w_ffn_gate=m_w_ffn_gate, w_ffn_up=m_w_ffn_up, w_ffn_down=m_w_ffn_down)
    v_in = dict(ada_w=v_ada_w, ada_b=v_ada_b, ln1_g=v_ln1_g, w_in=v_w_in, q_norm_g=v_q_norm_g, k_norm_g=v_k_norm_g,
                conv_w=v_conv_w, w_branch_a=v_w_branch_a, w_branch_b=v_w_branch_b, w_out=v_w_out, ln2_g=v_ln2_g,
                w_ffn_gate=v_w_ffn_gate, w_ffn_up=v_w_ffn_up, w_ffn_down=v_w_ffn_down)
    names = list(weights)

    mx, my, mc = lax.axis_index("x"), lax.axis_index("y"), lax.axis_index("c")
    me = 4 * mx + 2 * my + mc
    xs, target = x[0], loss_target[0]
    t, d = xs.shape
    depth = ada_w.shape[0]
    mod_cols = ada_w.shape[2]
    conv_cols = conv_w.shape[2]
    row_w = 1024
    tile = 512 if t % 512 == 0 else t

    small = jnp.concatenate([c.reshape(-1), conv_w.reshape(-1)])
    small_n = -(-small.shape[0] // row_w) * row_w
    small = jnp.pad(small, (0, small_n - small.shape[0])).reshape(-1, row_w)
    small_all = _all_gather([small], "gather_cond")[0].reshape(N_DEV, -1)
    c_all = small_all[:, :d]
    conv_full = jnp.transpose(small_all[:, d:d + depth * 3 * conv_cols].reshape(N_DEV, depth, 3, conv_cols), (1, 2, 0, 3)
                              ).reshape(depth, 3, N_DEV * conv_cols)

    ada_b_cols = lax.dynamic_slice_in_dim(ada_b, me * mod_cols, mod_cols, axis=1)
    mod_part = _ada_forward(c_all, ada_w, ada_b_cols)
    mod_all = _all_gather([mod_part.reshape(-1, row_w)], "gather_mod")[0].reshape(N_DEV, depth, N_DEV, mod_cols)
    mod_mine = lax.dynamic_index_in_dim(mod_all, me, axis=2, keepdims=False)
    mod = jnp.transpose(mod_mine, (1, 0, 2)).reshape(depth, 6, 1, d)

    f_shard = w_ffn_gate.shape[2]
    f_pad = -(-f_shard // LANES) * LANES - f_shard
    pads = dict(w_ffn_gate=((0, 0), (0, 0), (0, f_pad)), w_ffn_up=((0, 0), (0, 0), (0, f_pad)), w_ffn_down=((0, 0), (0, f_pad), (0, 0)))
    shards = [weights[n].astype(BF16) for n in _BIG]
    shards = [jnp.pad(a, pads[n]) if n in pads else a for n, a in zip(_BIG, shards)]
    axes = [1 if n in _COL_SHARDED else 0 for n in _BIG]
    first = _all_gather([a[0] for a in shards], "gather_weights_first", axes)
    rest = [a[l] for l in range(1, depth) for a in shards]
    rest_axes = axes * (depth - 1)
    rest_copies = functools.partial(_gather_copies, shapes=[a.shape for a in rest], axes=rest_axes)
    send_sems, recv_sems, rest, rest_land, started = _copies_start(
        rest, None, N_DEV - 1, rest_copies, "gather_weights_rest_start",
        lands=_place_own(rest, rest_axes, "gather_weights_rest_own"))

    def layer_weights(full, l):
        wl = {_SHORT[n]: a for n, a in zip(_BIG, full)}
        wl["ln1"], wl["ln2"] = ln1_g[l][None], ln2_g[l][None]
        wl["qg"] = jnp.tile(q_norm_g[l], d // HEAD_DIM)[None]
        wl["kg"] = jnp.tile(k_norm_g[l], d // HEAD_DIM)[None]
        wl["conv"] = conv_full[l]
        return wl

    layer_w = [layer_weights(first, 0)]
    layer_w[0]["ln1"] = layer_w[0]["ln1"] + started[:1, :1]

    saved = []
    x_cur, y_prev, gate_prev = xs, None, None
    for l in range(depth):
        if l == 1:
            _, landed = _copies_wait(send_sems, recv_sems, rest, rest_land, saved[0]["f"], rest_copies, "gather_weights_rest_wait")
            layer_w += [layer_weights(landed[(k - 1) * len(_BIG):k * len(_BIG)], k) for k in range(1, depth)]
        mods = [mod[l, k] for k in range(6)]
        s = _layer_forward(x_cur, y_prev, gate_prev, mods, layer_w[l], l, tile)
        saved.append(s)
        x_cur, y_prev, gate_prev = s["x2"], s["f"], mods[5]

    def loss_head(pid, tv, hv, bv):
        diff = tv[0] + bv[0] * tv[1] - tv[2]
        return [diff * (1.0 / d)], [_colsum(diff * diff) * (0.5 / d)]

    dx, loss_cols = _rowwise(loss_head, "loss_head", rows=t, tile=tile, tiled=[(x_cur, 0, d), (y_prev, 0, d), (target, 0, d)],
                             bcast=[gate_prev], outs=[(d, F32)], accs=[(1, d)])

    def by_owner(n, full):
        if n == "w_in":
            return jnp.stack(full)
        if n in _COL_SHARDED:
            return jnp.transpose(full.reshape(full.shape[0], N_DEV, -1), (1, 0, 2))
        return full.reshape(N_DEV, -1, full.shape[1])

    layer_g, dmods, parts, recv2, pending = [None] * depth, [None] * depth, [None] * depth, [None] * depth, {}
    started = None
    for l in reversed(range(depth)):
        mods = [mod[l, k] for k in range(6)]
        if started is not None:
            mods[5] = mods[5] + started[:1, :1]
        dx, layer_g[l], dmods[l] = _layer_backward(dx, saved[l], mods, layer_w[l], l, tile)
        tensors = [by_owner(n, layer_g[l][_SHORT[n]]) for n in _BIG]
        recv1 = _exchange_sibling(tensors, f"rs_exchange_sibling_l{l}")
        parts[l] = [_add_sibling(a, r, f"rs_add_sibling_l{l}_{i}") for i, (a, r) in enumerate(zip(tensors, recv1))]
        if l > 0:
            *pending[l], started = _copies_start(parts[l], [(3,) + a.shape[1:] for a in parts[l]], 3, _chip_copies,
                                                 f"rs_exchange_chips_l{l}_start")
        else:
            recv2[l] = _exchange_chips(parts[l], "rs_exchange_chips_l0")
    for l, (s_sems, r_sems, srcs, lands) in pending.items():
        parts[l], recv2[l] = _copies_wait(s_sems, r_sems, srcs, lands, dx, _chip_copies, f"rs_exchange_chips_l{l}_wait")
    grad_x = dx[None]

    pieces = [jnp.concatenate(dmods[l], axis=1) for l in range(depth)]
    for key in ("ln1", "ln2", "qg", "kg"):
        pieces += [layer_g[l][key] for l in range(depth)]
    pieces += [layer_g[l]["conv"].reshape(1, -1) for l in range(depth)]
    pieces.append(loss_cols)
    part_small = jnp.concatenate(pieces, axis=1).reshape(-1, row_w)
    part_all = _all_gather([part_small], "gather_small_grads")[0]
    summed = _device_sum(part_all).reshape(-1)
    n_mod = depth * 6 * d
    dmod_all = part_all.reshape(N_DEV, -1)[:, :n_mod].reshape(N_DEV, depth, 6 * d)
    dmod_cols = jnp.transpose(lax.dynamic_slice_in_dim(dmod_all, me * mod_cols, mod_cols, axis=2), (1, 0, 2))
    g = {"ada_w": _ada_backward(c_all, dmod_cols), "ada_b": summed[:n_mod].reshape(depth, 6 * d)}
    off = n_mod
    g["ln1_g"] = summed[off:off + depth * d].reshape(depth, d)
    g["ln2_g"] = summed[off + depth * d:off + 2 * depth * d].reshape(depth, d)
    g["q_norm_g"] = summed[off + 2 * depth * d:off + 3 * depth * d].reshape(depth, d // HEAD_DIM, HEAD_DIM).sum(axis=1)
    g["k_norm_g"] = summed[off + 3 * depth * d:off + 4 * depth * d].reshape(depth, d // HEAD_DIM, HEAD_DIM).sum(axis=1)
    off += 4 * depth * d
    conv_g = summed[off:off + depth * 3 * d].reshape(depth, 3, N_DEV, conv_cols)
    g["conv_w"] = lax.dynamic_index_in_dim(conv_g, me, axis=2, keepdims=False)
    off += depth * 3 * d
    loss = jnp.sum(summed[off:off + d])

    def shard_of(n, a):
        if n in ("w_ffn_gate", "w_ffn_up"):
            return a[:, :f_shard]
        return a[:f_shard] if n == "w_ffn_down" else a

    outs = {}
    for k, n in enumerate(_BIG):
        g_layers = [[shard_of(n, a) for a in (parts[l][k][0], *recv2[l][k])] for l in range(depth)]
        outs[n] = _adamw(weights[n], g_layers, m_in[n], v_in[n], f"adamw_{n}")
    for n in names:
        if n not in outs:
            outs[n] = _adamw(weights[n], [[g[n]]], m_in[n], v_in[n], f"adamw_{n}")
    return (loss, grad_x, *[outs[n][0] for n in names], *[outs[n][1] for n in names], *[outs[n][2] for n in names],
            *[outs[n][3] for n in names])
```

```python
import functools

import numpy as np
import jax
import jax.numpy as jnp
from jax import lax
from jax.experimental import pallas as pl
from jax.experimental.pallas import tpu as pltpu

F32, BF16 = jnp.float32, jnp.bfloat16
MESH = pl.DeviceIdType.MESH
N_DEV = 8
LANES = 128
HEAD_DIM = 64
HEAD_PAIR = 2 * HEAD_DIM
ATT_TILE = 256
ATT_UNROLL = 4
EPS = 1e-6
VMEM_LIMIT = 56 * 1024 * 1024

ADAM_LR, ADAM_B1, ADAM_B2, ADAM_EPS, ADAM_WD, ADAM_STEP = 0.001, 0.9, 0.999, 1e-08, 0.01, 10

_NT = (((1,), (1,)), ((), ()))
_TN = (((0,), (0,)), ((), ()))
_NN = (((1,), (0,)), ((), ()))


def _params(n_grid):
    return pltpu.CompilerParams(dimension_semantics=("arbitrary",) * n_grid, vmem_limit_bytes=VMEM_LIMIT)


def _sigmoid(x):
    return 1.0 / (1.0 + jnp.exp(-x))


def _rowwise(fn, name, *, rows, tile, tiled=(), halos=(), bcast=(), outs=(), accs=()):
    n = rows // tile
    assert n * tile == rows
    in_specs, args = [], []
    for t in tiled:
        arr, cb, w = t[:3]
        rowmap = t[3] if len(t) > 3 else (lambda i: i)
        in_specs.append(pl.BlockSpec((tile, w), functools.partial(lambda i, cb, rowmap: (rowmap(i), cb), cb=cb, rowmap=rowmap)))
        args.append(arr)
    per_tile8 = tile // 8
    for arr, cb, w, side in halos:
        last8 = arr.shape[0] // 8 - 1
        if side == "prev":
            imap = functools.partial(lambda i, cb: (jnp.maximum(i * per_tile8 - 1, 0), cb), cb=cb)
        else:
            imap = functools.partial(lambda i, cb, last8: (jnp.minimum((i + 1) * per_tile8, last8), cb), cb=cb, last8=last8)
        in_specs.append(pl.BlockSpec((8, w), imap))
        args.append(arr)
    for arr in bcast:
        in_specs.append(pl.BlockSpec(arr.shape, functools.partial(lambda i, nd: (0,) * nd, nd=arr.ndim)))
        args.append(arr)
    out_shape = [jax.ShapeDtypeStruct((rows, w), dt) for w, dt in outs] + [jax.ShapeDtypeStruct(s, F32) for s in accs]
    out_specs = [pl.BlockSpec((tile, w), lambda i: (i, 0)) for w, _ in outs] + [pl.BlockSpec(s, lambda i: (0, 0)) for s in accs]
    nt, nh, nb, no, na = len(tiled), len(halos), len(bcast), len(outs), len(accs)

    def body(*refs):
        pid = pl.program_id(0)
        tv = [r[...] for r in refs[:nt]]
        hv = [r[...] for r in refs[nt:nt + nh]]
        bv = [r[...] for r in refs[nt + nh:nt + nh + nb]]
        out_refs = refs[nt + nh + nb:nt + nh + nb + no]
        acc_refs = refs[nt + nh + nb + no:]
        ov, av = fn(pid, tv, hv, bv)
        for r, v in zip(out_refs, ov):
            r[...] = v.astype(r.dtype)
        if na:
            @pl.when(pid == 0)
            def _():
                for r in acc_refs:
                    r[...] = jnp.zeros(r.shape, F32)
            for r, v in zip(acc_refs, av):
                r[...] += v

    res = pl.pallas_call(body, name=name, grid=(n,), in_specs=in_specs, out_specs=out_specs, out_shape=out_shape,
                         compiler_params=_params(1))(*args)
    return res


def _colsum(v):
    return jnp.sum(v, axis=0, keepdims=True)


def _matmul(pairs, mode, *, m, n, k, tm, tn, out_dtype, name):
    tm, tn = min(tm, m), min(tn, n)
    assert m % tm == 0 and n % tn == 0
    in_specs, args = [], []
    for a, acb, b, bcb in pairs:
        if mode == "tn":
            in_specs.append(pl.BlockSpec((k, tm), functools.partial(lambda i, j, o: (0, o + i), o=acb)))
            in_specs.append(pl.BlockSpec((k, tn), functools.partial(lambda i, j, o: (0, o + j), o=bcb)))
        elif mode == "nn":
            in_specs.append(pl.BlockSpec((tm, k), functools.partial(lambda i, j, o: (i, o), o=acb)))
            in_specs.append(pl.BlockSpec((k, tn), functools.partial(lambda i, j, o: (0, o + j), o=bcb)))
        else:
            in_specs.append(pl.BlockSpec((tm, k), functools.partial(lambda i, j, o: (i, o), o=acb)))
            in_specs.append(pl.BlockSpec((tn, k), functools.partial(lambda i, j, o: (j, o), o=bcb)))
        args += [a, b]
    dims = {"nn": _NN, "nt": _NT, "tn": _TN}[mode]
    npairs = len(pairs)

    def body(*refs):
        o_ref = refs[2 * npairs]
        acc = None
        for p in range(npairs):
            d = lax.dot_general(refs[2 * p][...].astype(BF16), refs[2 * p + 1][...].astype(BF16), dims,
                                preferred_element_type=F32)
            acc = d if acc is None else acc + d
        o_ref[...] = acc.astype(o_ref.dtype)

    return pl.pallas_call(body, name=name, grid=(m // tm, n // tn), in_specs=in_specs,
                          out_specs=pl.BlockSpec((tm, tn), lambda i, j: (i, j)),
                          out_shape=jax.ShapeDtypeStruct((m, n), out_dtype), compiler_params=_params(2))(*args)


def _scan_matrix(kind, n):
    r = np.arange(n)
    tri = (r[:, None] > r[None, :]) if kind == "suffix" else (r[:, None] < r[None, :])
    half = np.concatenate([tri.astype(np.float32), np.ones((n, LANES), np.float32)], axis=1)
    return jnp.asarray(np.concatenate([half, half], axis=0), BF16)


def _head_sum_matrix():
    r = np.arange(LANES)
    bd = (r[:, None] // HEAD_DIM == r[None, :] // HEAD_DIM).astype(np.float32)
    return jnp.asarray(np.concatenate([bd, bd], axis=0), BF16)


def _split_cat(v):
    hi = v.astype(BF16)
    lo = (v - hi.astype(F32)).astype(BF16)
    return jnp.concatenate([hi, lo], axis=1)


def _head_sums(v, bd2):
    hi = v.astype(BF16)
    lo = (v - hi.astype(F32)).astype(BF16)
    parts = []
    for g in range(v.shape[1] // LANES):
        sl = slice(g * LANES, (g + 1) * LANES)
        parts.append(jnp.dot(jnp.concatenate([hi[:, sl], lo[:, sl]], axis=1), bd2, preferred_element_type=F32))
    return jnp.concatenate(parts, axis=1)


def _scan_parts(v, scan):
    return [jnp.dot(_split_cat(v[:, b * LANES:(b + 1) * LANES]), scan, preferred_element_type=F32)
            for b in range(v.shape[1] // LANES)]


def _chain_sums(parts, carry, reverse):
    nb = len(parts)
    outs = [None] * nb
    for b in (reversed(range(nb)) if reverse else range(nb)):
        outs[b] = carry + parts[b][:, :LANES]
        carry = carry + parts[b][:, LANES:]
    return jnp.concatenate(outs, axis=1), carry


def _log_weights(z, mask):
    sp = jnp.log(1.0 + jnp.exp(-jnp.abs(z)))
    log_not = -(jnp.maximum(z, 0.0) + sp)
    log_beta = z + log_not
    return (log_not if mask is None else jnp.where(mask, log_not, 0.0)), log_beta


def _pipeline(n_chains, stages):
    for step in range(n_chains + len(stages) - 1):
        for s, stage in enumerate(stages):
            if 0 <= step - s < n_chains:
                stage(step - s)


def _head_masks(tq):
    lane = lax.broadcasted_iota(jnp.int32, (tq, HEAD_PAIR), 1)
    return [(lane // HEAD_DIM) == hh for hh in range(2)]


def _diag_mask(tq):
    return lax.broadcasted_iota(jnp.int32, (tq, tq), 1) < lax.broadcasted_iota(jnp.int32, (tq, tq), 0)


def _attention_fwd(qn, kn, vb, name):
    t, d = qn.shape
    tq = min(ATT_TILE, t)
    nq, hp = t // tq, d // HEAD_PAIR
    scan_suffix = _scan_matrix("suffix", LANES)

    def body(q_ref, k_ref, v_ref, sc_ref, o_ref):
        qi = pl.program_id(1)
        q2 = q_ref[...].astype(F32)
        scan = sc_ref[...]
        heads, diag = _head_masks(tq), _diag_mask(tq)
        qms = [jnp.where(h, q2, 0.0).astype(BF16) for h in heads]

        def rows_of(j):
            return pl.ds(pl.multiple_of(j * tq, tq), tq)

        def step(js, state, mask):
            carry, acc = [state[0], state[2]], [state[1], state[3]]
            kts, vts = [k_ref[rows_of(j), :] for j in js], [v_ref[rows_of(j), :] for j in js]
            z, lw, parts, a = {}, {}, {}, {}

            def s_scores(c):
                z[c] = lax.dot_general(qms[c % 2], kts[c // 2], _NT, preferred_element_type=F32)

            def s_logs(c):
                lw[c] = _log_weights(z.pop(c), mask)

            def s_scan(c):
                parts[c] = _scan_parts(lw[c][0], scan)

            def s_weights(c):
                tail, carry[c % 2] = _chain_sums(parts.pop(c), carry[c % 2], True)
                av = jnp.exp(lw.pop(c)[1] + tail)
                a[c] = (av if mask is None else jnp.where(mask, av, 0.0)).astype(BF16)

            def s_values(c):
                acc[c % 2] = acc[c % 2] + jnp.dot(a.pop(c), vts[c // 2], preferred_element_type=F32)

            _pipeline(2 * len(js), [s_scores, s_logs, s_scan, s_weights, s_values])
            return (carry[0], acc[0], carry[1], acc[1])

        zero, zq = jnp.zeros((tq, LANES), F32), jnp.zeros((tq, HEAD_PAIR), F32)
        state = step([qi], (zero, zq, zero, zq), diag)
        rem = qi % ATT_UNROLL
        state = lax.fori_loop(0, rem, lambda p, c: step([qi - 1 - p], c, None), state)
        state = lax.fori_loop(0, qi // ATT_UNROLL,
                              lambda p, c: step([qi - 1 - rem - ATT_UNROLL * p - u for u in range(ATT_UNROLL)], c, None), state)
        o_ref[...] = jnp.where(heads[0], state[1], state[3]).astype(o_ref.dtype)

    return pl.pallas_call(
        body, name=name, grid=(hp, nq),
        in_specs=[pl.BlockSpec((tq, HEAD_PAIR), lambda h, i: (i, h)),
                  pl.BlockSpec((t, HEAD_PAIR), lambda h, i: (0, h)),
                  pl.BlockSpec((t, HEAD_PAIR), lambda h, i: (0, h)),
                  pl.BlockSpec(scan_suffix.shape, lambda h, i: (0, 0))],
        out_specs=pl.BlockSpec((tq, HEAD_PAIR), lambda h, i: (i, h)),
        out_shape=jax.ShapeDtypeStruct((t, d), BF16), compiler_params=_params(2))(qn, kn, vb, scan_suffix)


def _attention_bwd(qn, kn, vb, dob, name):
    t, d = qn.shape
    tq = min(ATT_TILE, t)
    nq, hp = t // tq, d // HEAD_PAIR
    scan_suffix, scan_prefix = _scan_matrix("suffix", LANES), _scan_matrix("prefix", LANES)

    def body(q_ref, k_ref, v_ref, do_ref, ss_ref, sp_ref, dq_ref, dk_ref, dv_ref, g_s, b_s):
        qi = pl.program_id(1)

        @pl.when(qi == 0)
        def _():
            dk_ref[...] = jnp.zeros(dk_ref.shape, F32)
            dv_ref[...] = jnp.zeros(dv_ref.shape, F32)

        q2, do2 = q_ref[...].astype(F32), do_ref[...].astype(F32)
        ssuf, spre = ss_ref[...], sp_ref[...]
        heads, diag = _head_masks(tq), _diag_mask(tq)
        qms = [jnp.where(h, q2, 0.0).astype(BF16) for h in heads]
        doms = [jnp.where(h, do2, 0.0).astype(BF16) for h in heads]

        def rows_of(j):
            return pl.ds(pl.multiple_of(j * tq, tq), tq)

        def pass_one(js, carries, mask):
            carry = list(carries)
            kts, vts = [k_ref[rows_of(j), :] for j in js], [v_ref[rows_of(j), :] for j in js]
            z, d_a, lw, parts, a, dv = {}, {}, {}, {}, {}, {}

            def s_scores(c):
                z[c] = lax.dot_general(qms[c % 2], kts[c // 2], _NT, preferred_element_type=F32)
                d_a[c] = lax.dot_general(doms[c % 2], vts[c // 2], _NT, preferred_element_type=F32)

            def s_logs(c):
                lw[c] = _log_weights(z.pop(c), mask)

            def s_scan(c):
                parts[c] = _scan_parts(lw[c][0], ssuf)

            def s_weights(c):
                tail, carry[c % 2] = _chain_sums(parts.pop(c), carry[c % 2], True)
                log_beta = lw.pop(c)[1]
                av, beta = jnp.exp(log_beta + tail), jnp.exp(log_beta)
                if mask is not None:
                    av, beta = jnp.where(mask, av, 0.0), jnp.where(mask, beta, 0.0)
                g_s[c % 2, js[c // 2]] = av * d_a.pop(c)
                b_s[c % 2, js[c // 2]] = beta
                a[c] = av.astype(BF16)

            def s_values(c):
                dv[c] = lax.dot_general(a.pop(c), doms[c % 2], _TN, preferred_element_type=F32)
                if c % 2 == 1:
                    dv_ref[rows_of(js[c // 2]), :] += dv.pop(c - 1) + dv.pop(c)

            _pipeline(2 * len(js), [s_scores, s_logs, s_scan, s_weights, s_values])
            return tuple(carry)

        zero = jnp.zeros((tq, LANES), F32)
        rem = qi % ATT_UNROLL
        carries = pass_one([qi], (zero, zero), diag)
        carries = lax.fori_loop(0, rem, lambda p, c: pass_one([qi - 1 - p], c, None), carries)
        lax.fori_loop(0, qi // ATT_UNROLL,
                      lambda p, c: pass_one([qi - 1 - rem - ATT_UNROLL * p - u for u in range(ATT_UNROLL)], c, None), carries)

        def pass_two(js, state):
            prefix, dq = [state[0], state[2]], [state[1], state[3]]
            kts = [k_ref[rows_of(j), :] for j in js]
            gb, parts, dz, dk = {}, {}, {}, {}

            def s_scan(c):
                gb[c] = (g_s[c % 2, js[c // 2]], b_s[c % 2, js[c // 2]])
                parts[c] = _scan_parts(gb[c][0], spre)

            def s_dz(c):
                before, prefix[c % 2] = _chain_sums(parts.pop(c), prefix[c % 2], False)
                g, beta = gb.pop(c)
                dz[c] = (g - beta * (g + before)).astype(BF16)

            def s_grads(c):
                dzc = dz.pop(c)
                dq[c % 2] = dq[c % 2] + jnp.dot(dzc, kts[c // 2], preferred_element_type=F32)
                dk[c] = lax.dot_general(dzc, qms[c % 2], _TN, preferred_element_type=F32)
                if c % 2 == 1:
                    dk_ref[rows_of(js[c // 2]), :] += dk.pop(c - 1) + dk.pop(c)

            _pipeline(2 * len(js), [s_scan, s_dz, s_grads])
            return (prefix[0], dq[0], prefix[1], dq[1])

        zq = jnp.zeros((tq, HEAD_PAIR), F32)
        first = (qi + 1) % ATT_UNROLL
        out = lax.fori_loop(0, first, lambda p, c: pass_two([p], c), (zero, zq, zero, zq))
        out = lax.fori_loop(0, (qi + 1) // ATT_UNROLL,
                            lambda p, c: pass_two([first + ATT_UNROLL * p + u for u in range(ATT_UNROLL)], c), out)
        dq_ref[...] = jnp.where(heads[0], out[1], out[3])

    blk = pl.BlockSpec((tq, HEAD_PAIR), lambda h, i: (i, h))
    col_spec = pl.BlockSpec((t, HEAD_PAIR), lambda h, i: (0, h))
    const = pl.BlockSpec(scan_suffix.shape, lambda h, i: (0, 0))
    full = jax.ShapeDtypeStruct((t, d), F32)
    return pl.pallas_call(
        body, name=name, grid=(hp, nq), in_specs=[blk, col_spec, col_spec, blk, const, const],
        out_specs=[blk, col_spec, col_spec], out_shape=[full, full, full],
        scratch_shapes=[pltpu.VMEM((2, nq, tq, tq), F32), pltpu.VMEM((2, nq, tq, tq), F32)],
        compiler_params=_params(2))(qn, kn, vb, dob, scan_suffix, scan_prefix)


def _chip_at(x, y, j):
    return (1 - x if j & 2 else x, 1 - y if j & 1 else y)


def _block_of(ref, dev, shape, axis):
    if axis is None:
        return ref.at[dev]
    if axis == 0:
        return ref.at[pl.ds(pl.multiple_of(dev * shape[0], shape[0]), shape[0]), :]
    return ref.at[:, pl.ds(pl.multiple_of(dev * shape[1], shape[1]), shape[1])]


def _gathered_shape(shape, axis):
    if axis is None:
        return (N_DEV,) + shape
    return (N_DEV * shape[0], shape[1]) if axis == 0 else (shape[0], N_DEV * shape[1])


def _all_gather(shards, name, axes=None, place=(), place_axes=(), after=()):
    n, m, na = len(shards), len(place), len(after)
    axes = [None] * n if axes is None else axes

    def body(*refs):
        x_refs, p_refs = refs[:n], refs[n:n + m]
        out_refs, land_refs = refs[n + m + na:2 * n + m + na], refs[2 * n + m + na:2 * n + 2 * m + na]
        send_sems, recv_sems, local_sems = refs[2 * n + 2 * m + na:]
        x, y, c = lax.axis_index("x"), lax.axis_index("y"), lax.axis_index("c")
        me, sibling = (x, y, c), (x, y, 1 - c)
        chips = [_chip_at(x, y, j) for j in (1, 2, 3)]

        def rows(t, px, py, pc):
            return _block_of(out_refs[t], 4 * px + 2 * py + pc, shards[t].shape, axes[t])

        def copy(t, k, block, to, src=None):
            return pltpu.make_async_remote_copy(src_ref=rows(t, *block) if src is None else src, dst_ref=rows(t, *block),
                                                send_sem=send_sems.at[t, k], recv_sem=recv_sems.at[t, k],
                                                device_id=to, device_id_type=MESH)

        mine = [pltpu.make_async_copy(x_refs[t], rows(t, *me), local_sems.at[t]) for t in range(n)]
        mine += [pltpu.make_async_copy(p_refs[u], _block_of(land_refs[u], 4 * x + 2 * y + c, place[u].shape, place_axes[u]),
                                       local_sems.at[n + u]) for u in range(m)]
        for cp in mine:
            cp.start()
        first = []
        for t in range(n):
            first.append(copy(t, 0, me, sibling, src=x_refs[t]))
            first += [copy(t, 1 + j, me, (*chip, c), src=x_refs[t]) for j, chip in enumerate(chips)]
        for cp in first:
            cp.start()
        passed = []
        for t in range(n):
            for j, chip in enumerate(chips):
                copy(t, 1 + j, (*chip, c), me).wait_recv()
                passed.append(copy(t, 4 + j, (*chip, c), sibling))
                passed[-1].start()
        for t in range(n):
            copy(t, 0, sibling, me).wait_recv()
            for j, chip in enumerate(chips):
                copy(t, 4 + j, (*chip, 1 - c), me).wait_recv()
        for cp in first + passed:
            cp.wait_send()
        for cp in mine:
            cp.wait()

    any_spec = pl.BlockSpec(memory_space=pl.ANY)
    out_shape = [jax.ShapeDtypeStruct(_gathered_shape(a.shape, ax), a.dtype) for a, ax in zip((*shards, *place), (*axes, *place_axes))]
    return pl.pallas_call(
        body, name=name, out_shape=out_shape, in_specs=[any_spec] * (n + m + na), out_specs=[any_spec] * (n + m),
        scratch_shapes=[pltpu.SemaphoreType.DMA((n, 7)), pltpu.SemaphoreType.DMA((n, 7)), pltpu.SemaphoreType.DMA((n + m,))])(
            *shards, *place, *after)


def _place_own(shards, axes, name):
    n = len(shards)

    def body(*refs):
        x_refs, out_refs, sems = refs[:n], refs[n:2 * n], refs[2 * n]
        dev = 4 * lax.axis_index("x") + 2 * lax.axis_index("y") + lax.axis_index("c")
        copies = [pltpu.make_async_copy(x_refs[t], _block_of(out_refs[t], dev, shards[t].shape, axes[t]), sems.at[t]) for t in range(n)]
        for cp in copies:
            cp.start()
        for cp in copies:
            cp.wait()

    any_spec = pl.BlockSpec(memory_space=pl.ANY)
    return pl.pallas_call(
        body, name=name, out_shape=[jax.ShapeDtypeStruct(_gathered_shape(a.shape, ax), a.dtype) for a, ax in zip(shards, axes)],
        in_specs=[any_spec] * n, out_specs=[any_spec] * n, scratch_shapes=[pltpu.SemaphoreType.DMA((n,))])(*shards)


def _exchange_sibling(tensors, name):
    n = len(tensors)

    def body(*refs):
        g_refs, recv_refs = refs[:n], refs[n:2 * n]
        send_sems, recv_sems = refs[2 * n:]
        x, y, c = lax.axis_index("x"), lax.axis_index("y"), lax.axis_index("c")
        copies = []
        for t in range(n):
            for j in range(4):
                cx, cy = _chip_at(x, y, j)
                copies.append(pltpu.make_async_remote_copy(
                    src_ref=g_refs[t].at[4 * cx + 2 * cy + (1 - c)], dst_ref=recv_refs[t].at[j], send_sem=send_sems.at[t, j],
                    recv_sem=recv_sems.at[t, j], device_id=(x, y, 1 - c), device_id_type=MESH))
        for cp in copies:
            cp.start()
        for cp in copies:
            cp.wait_recv()
        for cp in copies:
            cp.wait_send()

    any_spec = pl.BlockSpec(memory_space=pl.ANY)
    return pl.pallas_call(
        body, name=name, out_shape=[jax.ShapeDtypeStruct((4,) + a.shape[1:], a.dtype) for a in tensors],
        in_specs=[any_spec] * n, out_specs=[any_spec] * n,
        scratch_shapes=[pltpu.SemaphoreType.DMA((n, 4)), pltpu.SemaphoreType.DMA((n, 4))])(*tensors)


def _exchange_chips(parts, name):
    n = len(parts)

    def body(*refs):
        p_refs, recv_refs = refs[:n], refs[n:2 * n]
        send_sems, recv_sems = refs[2 * n:]
        x, y, c = lax.axis_index("x"), lax.axis_index("y"), lax.axis_index("c")
        copies = []
        for t in range(n):
            for j in (1, 2, 3):
                copies.append(pltpu.make_async_remote_copy(
                    src_ref=p_refs[t].at[j], dst_ref=recv_refs[t].at[j - 1], send_sem=send_sems.at[t, j - 1],
                    recv_sem=recv_sems.at[t, j - 1], device_id=(*_chip_at(x, y, j), c), device_id_type=MESH))
        for cp in copies:
            cp.start()
        for cp in copies:
            cp.wait_recv()
        for cp in copies:
            cp.wait_send()

    any_spec = pl.BlockSpec(memory_space=pl.ANY)
    return pl.pallas_call(
        body, name=name, out_shape=[jax.ShapeDtypeStruct((3,) + a.shape[1:], a.dtype) for a in parts],
        in_specs=[any_spec] * n, out_specs=[any_spec] * n,
        scratch_shapes=[pltpu.SemaphoreType.DMA((n, 3)), pltpu.SemaphoreType.DMA((n, 3))])(*parts)


_HBM = pl.BlockSpec(memory_space=pltpu.HBM)
_SEM = pl.BlockSpec(memory_space=pltpu.SEMAPHORE)
_DATAFLOW = pltpu.SideEffectType.DATAFLOW_SIDE_EFFECTING


def _peer(x, y, c, mask):
    return (1 - x if mask & 4 else x, 1 - y if mask & 2 else y, 1 - c if mask & 1 else c)


def _gather_copies(x_refs, land_refs, send_sems, recv_sems, waiting, shapes, axes):
    x, y, c = lax.axis_index("x"), lax.axis_index("y"), lax.axis_index("c")
    copies = []
    for t in range(len(x_refs)):
        for mask in range(1, N_DEV):
            px, py, pc = _peer(x, y, c, mask)
            block = 4 * px + 2 * py + pc if waiting else 4 * x + 2 * y + c
            copies.append(pltpu.make_async_remote_copy(
                src_ref=x_refs[t], dst_ref=_block_of(land_refs[t], block, shapes[t], axes[t]), send_sem=send_sems.at[7 * t + mask - 1],
                recv_sem=recv_sems.at[7 * t + mask - 1], device_id=(px, py, pc), device_id_type=MESH))
    return copies


def _chip_copies(p_refs, land_refs, send_sems, recv_sems, waiting):
    x, y, c = lax.axis_index("x"), lax.axis_index("y"), lax.axis_index("c")
    return [pltpu.make_async_remote_copy(src_ref=p_refs[t].at[j], dst_ref=land_refs[t].at[j - 1], send_sem=send_sems.at[3 * t + j - 1],
                                         recv_sem=recv_sems.at[3 * t + j - 1], device_id=(*_chip_at(x, y, j), c), device_id_type=MESH)
            for t in range(len(p_refs)) for j in (1, 2, 3)]


def _copies_start(srcs, land_shapes, per_src, copies, name, lands=None):
    n = len(srcs)

    def body(*refs):
        for cp in copies(refs[:n], refs[n:2 * n], refs[2 * n], refs[2 * n + 1], False):
            cp.start()
        refs[-1][...] = jnp.zeros(refs[-1].shape, F32)

    if lands is None:
        lands = [lax.empty(shape, a.dtype) for shape, a in zip(land_shapes, srcs)]
    args =[pltpu.with_memory_space_constraint(a, pltpu.HBM) for a in (*srcs, *lands)]
    sems = pltpu.SemaphoreType.DMA((n * per_src,))
    res = pl.pallas_call(
        body, name=name, out_shape=(sems, sems, *[pltpu.HBM(a.shape, a.dtype) for a in args], jax.ShapeDtypeStruct((8, LANES), F32)),
        in_specs=[_HBM] * (2 * n), out_specs=(_SEM, _SEM, *[_HBM] * (2 * n), pl.BlockSpec(memory_space=pltpu.VMEM)),
        input_output_aliases={i: 2 + i for i in range(2 * n)}, compiler_params=pltpu.CompilerParams(has_side_effects=_DATAFLOW))(*args)
    return res[0], res[1], list(res[2:2 + n]), list(res[2 + n:2 + 2 * n]), res[-1]


def _copies_wait(send_sems, recv_sems, srcs, lands, after, copies, name):
    n = len(srcs)

    def body(*refs):
        for cp in copies(refs[:n], refs[n:2 * n], refs[2 * n], refs[2 * n + 1], True):
            cp.wait_send()
            cp.wait_recv()

    res = pl.pallas_call(
        body, name=name, out_shape=tuple(pltpu.HBM(a.shape, a.dtype) for a in (*srcs, *lands)),
        in_specs=[_HBM] * (2 * n) + [_SEM, _SEM, pl.BlockSpec(memory_space=pl.ANY)], out_specs=tuple([_HBM] * (2 * n)),
        input_output_aliases={i: i for i in range(2 * n)}, compiler_params=pltpu.CompilerParams(has_side_effects=_DATAFLOW))(
            *srcs, *lands, send_sems, recv_sems, after)
    return list(res[:n]), list(res[n:])


def _add_sibling(g8, recv1, name):
    _, a, b = g8.shape
    ta = next(cand for cand in (512, 384, 256, 128) if a % cand == 0)

    def body(g_ref, r_ref, o_ref):
        o_ref[...] = (g_ref[...].astype(F32) + r_ref[...].astype(F32)).astype(o_ref.dtype)

    def own(j, i):
        x, y, c = lax.axis_index("x"), lax.axis_index("y"), lax.axis_index("c")
        return (4 * (x ^ (j >> 1)) + 2 * (y ^ (j & 1)) + c, i, 0)

    slot = pl.BlockSpec((None, ta, b), lambda j, i: (j, i, 0))
    return pl.pallas_call(body, name=name, grid=(4, a // ta), in_specs=[pl.BlockSpec((None, ta, b), own), slot],
                          out_specs=slot, out_shape=jax.ShapeDtypeStruct((4, a, b), g8.dtype),
                          compiler_params=_params(2))(g8, recv1)


def _ada_forward(c_all, ada_w, ada_b_cols):
    depth, d, cols = ada_w.shape

    def body(c_ref, w_ref, b_ref, o_ref):
        cv = c_ref[...]
        act = cv * _sigmoid(cv)
        o_ref[...] = jnp.dot(act, w_ref[...], preferred_element_type=F32, precision=lax.Precision.HIGHEST) + b_ref[...]

    return pl.pallas_call(
        body, name="ada_forward", grid=(depth,),
        in_specs=[pl.BlockSpec((N_DEV, d), lambda l: (0, 0)), pl.BlockSpec((None, d, cols), lambda l: (l, 0, 0)),
                  pl.BlockSpec((None, 1, cols), lambda l: (l, 0, 0))],
        out_specs=pl.BlockSpec((None, N_DEV, cols), lambda l: (l, 0, 0)),
        out_shape=jax.ShapeDtypeStruct((depth, N_DEV, cols), F32), compiler_params=_params(1))(
            c_all, ada_w, ada_b_cols.reshape(depth, 1, cols))


def _ada_backward(c_all, dmod_cols):
    depth, _, cols = dmod_cols.shape
    d = c_all.shape[1]

    def body(c_ref, g_ref, o_ref):
        cv = c_ref[...]
        act = cv * _sigmoid(cv)
        o_ref[...] = lax.dot_general(act, g_ref[...], _TN, preferred_element_type=F32, precision=lax.Precision.HIGHEST)

    return pl.pallas_call(
        body, name="ada_backward", grid=(depth,),
        in_specs=[pl.BlockSpec((N_DEV, d), lambda l: (0, 0)), pl.BlockSpec((None, N_DEV, cols), lambda l: (l, 0, 0))],
        out_specs=pl.BlockSpec((None, d, cols), lambda l: (l, 0, 0)),
        out_shape=jax.ShapeDtypeStruct((depth, d, cols), F32), compiler_params=_params(1))(c_all, dmod_cols)


def _device_sum(a):
    _, r, w = a.shape

    def body(a_ref, o_ref):
        acc = a_ref[0]
        for dev in range(1, N_DEV):
            acc = acc + a_ref[dev]
        o_ref[...] = acc

    return pl.pallas_call(body, name="device_sum", out_shape=jax.ShapeDtypeStruct((r, w), F32),
                          in_specs=[pl.BlockSpec(memory_space=pltpu.VMEM)], out_specs=pl.BlockSpec(memory_space=pltpu.VMEM))(a)


def _adamw(w, g_layers, m, v, name):
    shape = w.shape
    cols = shape[-1]
    rows = int(np.prod(shape[:-1]))
    layers = len(g_layers)
    per_layer = rows // layers
    tile = next((cand for cand in (512, 352, 256, 128) if per_layer % cand == 0), per_layer)
    n_l = per_layer // tile
    n_parts = len(g_layers[0])

    def fn(pid, tv, hv, bv):
        wv, mv, vv = tv[:3]
        gv = None
        for l in range(layers):
            gl = None
            for part in tv[3 + l * n_parts:3 + (l + 1) * n_parts]:
                gl = part.astype(F32) if gl is None else gl + part.astype(F32)
            gv = gl if gv is None else jnp.where(pid >= l * n_l, gl, gv)
        mn = ADAM_B1 * mv + (1.0 - ADAM_B1) * gv
        vn = ADAM_B2 * vv + (1.0 - ADAM_B2) * (gv * gv)
        m_hat = mn / (1.0 - ADAM_B1 ** ADAM_STEP)
        v_hat = vn / (1.0 - ADAM_B2 ** ADAM_STEP)
        delta = -ADAM_LR * (m_hat / (jnp.sqrt(v_hat) + ADAM_EPS) + ADAM_WD * wv)
        return [gv, delta, mn, vn], []

    tiled = [(a.reshape(rows, cols), 0, cols) for a in (w, m, v)]
    for l, parts in enumerate(g_layers):
        clamp = functools.partial(lambda i, l: jnp.clip(i - l * n_l, 0, n_l - 1), l=l)
        tiled += [(p.reshape(per_layer, cols), 0, cols, clamp) for p in parts]
    res = _rowwise(fn, name, rows=rows, tile=tile, tiled=tiled, outs=[(cols, F32)] * 4)
    return [r.reshape(shape) for r in res]


def _norm_modulate(x, y, gate, ln_g, scale, shift, name, tile):
    t, d = x.shape

    def fn(pid, tv, hv, bv):
        if y is None:
            xn = tv[0]
            g_ln, sc, sh = bv
        else:
            g_gate, g_ln, sc, sh = bv
            xn = tv[0] + g_gate * tv[1]
        r = lax.rsqrt(jnp.mean(xn * xn, axis=-1, keepdims=True) + EPS)
        h = (xn * r * g_ln) * (1.0 + sc) + sh
        return ([h] if y is None else [xn, h]), []

    if y is None:
        h, = _rowwise(fn, name, rows=t, tile=tile, tiled=[(x, 0, d)], bcast=[ln_g, scale, shift], outs=[(d, BF16)])
        return x, h
    xn, h = _rowwise(fn, name, rows=t, tile=tile, tiled=[(x, 0, d), (y, 0, d)], bcast=[gate, ln_g, scale, shift],
                     outs=[(d, F32), (d, BF16)])
    return xn, h


def _norm_backward(dh, x, dres, ln_g, scale, name, tile):
    t, d = x.shape

    def fn(pid, tv, hv, bv):
        dhv, xv, dr = tv
        g_ln, sc = bv
        r = lax.rsqrt(jnp.mean(xv * xv, axis=-1, keepdims=True) + EPS)
        xn = xv * r
        dxn = dhv * (1.0 + sc) * g_ln
        dx = dr + r * (dxn - xn * jnp.mean(dxn * xn, axis=-1, keepdims=True))
        return [dx], [_colsum(dhv), _colsum(dhv * (xn * g_ln)), _colsum(dhv * (1.0 + sc) * xn)]

    return _rowwise(fn, name, rows=t, tile=tile, tiled=[(dh, 0, d), (x, 0, d), (dres, 0, d)], bcast=[ln_g, scale],
                    outs=[(d, F32)], accs=[(1, d)] * 3)


def _shift_rows(u, halo, k, pid, first_tile_zero):
    rows = lax.broadcasted_iota(jnp.int32, u.shape, 0)
    halo = halo * jnp.where(pid == 0, 0.0, 1.0) if first_tile_zero else halo
    out = pltpu.roll(u, k, axis=0)
    for j in range(k):
        out = jnp.where(rows == j, halo[8 - k + j:8 - k + j + 1, :], out)
    return out


def _shift_rows_up(u, halo, k, pid, n_tiles):
    tile = u.shape[0]
    rows = lax.broadcasted_iota(jnp.int32, u.shape, 0)
    halo = halo * jnp.where(pid == n_tiles - 1, 0.0, 1.0)
    out = pltpu.roll(u, tile - k, axis=0)
    for j in range(k):
        out = jnp.where(rows == tile - k + j, halo[j:j + 1, :], out)
    return out


def _layer_forward(x_in, y_prev, gate_prev, mod, w, l, tile):
    sh1, sc1, g1, sh2, sc2, g2 = mod
    t, d = x_in.shape
    f = w["w_g"].shape[1]
    bd2 = _head_sum_matrix()
    s = {}
    s["x"], s["h"] = _norm_modulate(x_in, y_prev, gate_prev, w["ln1"], sc1, sh1, f"l{l}_norm1", tile)
    p = _matmul([(s["h"], 0, w["w_in"], 0)], "nn", m=t, n=8 * d, k=d, tm=1024, tn=512, out_dtype=F32, name=f"l{l}_in_proj")
    s["p"] = p

    def qk_norm(pid, tv, hv, bv):
        qr, kr, vr = tv
        qg, kg, bd = bv

        def nrm(xv, g):
            r = lax.rsqrt(_head_sums(xv * xv, bd) * (1.0 / HEAD_DIM) + EPS)
            return xv * r * g
        return [nrm(qr, qg) * 0.125, nrm(kr, kg), vr], []

    s["qn"], s["kn"], s["vb"] = _rowwise(qk_norm, f"l{l}_qk_norm", rows=t, tile=tile, tiled=[(p, 0, d), (p, 1, d), (p, 2, d)],
                                         bcast=[w["qg"], w["kg"], bd2], outs=[(d, BF16)] * 3)
    s["ya"] = _attention_fwd(s["qn"], s["kn"], s["vb"], f"l{l}_attention")

    def conv_fwd(pid, tv, hv, bv):
        cb, cc, cx = tv
        u = cc * cx
        hu = hv[0] * hv[1]
        cw = bv[0]
        conv = cw[0:1, :] * _shift_rows(u, hu, 2, pid, True) + cw[1:2, :] * _shift_rows(u, hu, 1, pid, True) + cw[2:3, :] * u
        return [cb * conv], []

    s["yb"], = _rowwise(conv_fwd, f"l{l}_conv", rows=t, tile=tile, tiled=[(p, 3, d), (p, 4, d), (p, 5, d)],
                        halos=[(p, 4, d, "prev"), (p, 5, d, "prev")], bcast=[w["conv"]], outs=[(d, BF16)])
    s["a"] = _matmul([(s["ya"], 0, w["w_a"], 0)], "nn", m=t, n=d, k=d, tm=1024, tn=512, out_dtype=F32, name=f"l{l}_branch_a")
    s["b"] = _matmul([(s["yb"], 0, w["w_b"], 0)], "nn", m=t, n=d, k=d, tm=1024, tn=512, out_dtype=F32, name=f"l{l}_branch_b")

    def merge(pid, tv, hv, bv):
        av, bvv, ga, gb = tv
        return [_sigmoid(ga) * av + _sigmoid(gb) * bvv], []

    s["merged"], = _rowwise(merge, f"l{l}_merge", rows=t, tile=tile, tiled=[(s["a"], 0, d), (s["b"], 0, d), (p, 6, d), (p, 7, d)],
                            outs=[(d, BF16)])
    s["mo"] = _matmul([(s["merged"], 0, w["w_o"], 0)], "nn", m=t, n=d, k=d, tm=1024, tn=512, out_dtype=F32, name=f"l{l}_out_proj")
    s["x2"], s["h2"] = _norm_modulate(s["x"], s["mo"], g1, w["ln2"], sc2, sh2, f"l{l}_norm2", tile)
    fn_tile = f // 2
    s["g"] = _matmul([(s["h2"], 0, w["w_g"], 0)], "nn", m=t, n=f, k=d, tm=512, tn=fn_tile, out_dtype=F32, name=f"l{l}_ffn_gate")
    s["u"] = _matmul([(s["h2"], 0, w["w_u"], 0)], "nn", m=t, n=f, k=d, tm=512, tn=fn_tile, out_dtype=F32, name=f"l{l}_ffn_up")

    def swiglu(pid, tv, hv, bv):
        gv, uv = tv
        return [gv * _sigmoid(gv) * uv], []

    s["s"], = _rowwise(swiglu, f"l{l}_swiglu", rows=t, tile=tile // 2, tiled=[(s["g"], 0, f), (s["u"], 0, f)], outs=[(f, BF16)])
    s["f"] = _matmul([(s["s"], 0, w["w_d"], 0)], "nn", m=t, n=d, k=f, tm=512, tn=512, out_dtype=F32, name=f"l{l}_ffn_down")
    return s


def _layer_backward(dx3, s, mod, w, l, tile):
    sh1, sc1, g1, sh2, sc2, g2 = mod
    t, d = dx3.shape
    f = w["w_g"].shape[1]
    p = s["p"]
    bd2 = _head_sum_matrix()
    n_tiles = t // tile
    grads = {}

    def gate_bwd(pid, tv, hv, bv):
        return [tv[0] * bv[0]], [_colsum(tv[0] * tv[1])]

    df, dg2 = _rowwise(gate_bwd, f"l{l}_bwd_gate2", rows=t, tile=tile, tiled=[(dx3, 0, d), (s["f"], 0, d)], bcast=[g2],
                       outs=[(d, BF16)], accs=[(1, d)])
    grads["w_d"] = _matmul([(s["s"], 0, df, 0)], "tn", m=f, n=d, k=t, tm=512, tn=512, out_dtype=BF16, name=f"l{l}_dw_down")
    ds = _matmul([(df, 0, w["w_d"], 0)], "nt", m=t, n=f, k=d, tm=512, tn=f // 2, out_dtype=F32, name=f"l{l}_d_swiglu")

    def swiglu_bwd(pid, tv, hv, bv):
        dsv, gv, uv = tv
        sig = _sigmoid(gv)
        return [dsv * uv * (sig * (1.0 + gv * (1.0 - sig))), dsv * (gv * sig)], []

    dgt, dup = _rowwise(swiglu_bwd, f"l{l}_bwd_swiglu", rows=t, tile=tile // 2, tiled=[(ds, 0, f), (s["g"], 0, f), (s["u"], 0, f)],
                        outs=[(f, BF16)] * 2)
    grads["w_g"] = _matmul([(s["h2"], 0, dgt, 0)], "tn", m=d, n=f, k=t, tm=512, tn=f // 2, out_dtype=BF16, name=f"l{l}_dw_gate")
    grads["w_u"] = _matmul([(s["h2"], 0, dup, 0)], "tn", m=d, n=f, k=t, tm=512, tn=f // 2, out_dtype=BF16, name=f"l{l}_dw_up")
    dh2 = _matmul([(dgt, 0, w["w_g"], 0), (dup, 0, w["w_u"], 0)], "nt", m=t, n=d, k=f, tm=512, tn=512, out_dtype=F32,
                  name=f"l{l}_dh2")
    dx2, dsh2, dsc2, grads["ln2"] = _norm_backward(dh2, s["x2"], dx3, w["ln2"], sc2, f"l{l}_bwd_norm2", tile)

    dmo, dg1 = _rowwise(gate_bwd, f"l{l}_bwd_gate1", rows=t, tile=tile, tiled=[(dx2, 0, d), (s["mo"], 0, d)], bcast=[g1],
                        outs=[(d, BF16)], accs=[(1, d)])
    grads["w_o"] = _matmul([(s["merged"], 0, dmo, 0)], "tn", m=d, n=d, k=t, tm=512, tn=512, out_dtype=BF16, name=f"l{l}_dw_out")
    dmerged = _matmul([(dmo, 0, w["w_o"], 0)], "nt", m=t, n=d, k=d, tm=1024, tn=512, out_dtype=F32, name=f"l{l}_d_merged")

    def merge_bwd(pid, tv, hv, bv):
        dm, av, bvv, ga, gb = tv
        sa, sb = _sigmoid(ga), _sigmoid(gb)
        return [dm * sa, dm * sb, dm * av * (sa * (1.0 - sa)), dm * bvv * (sb * (1.0 - sb))], []

    d_a, d_b, dga, dgb = _rowwise(merge_bwd, f"l{l}_bwd_merge", rows=t, tile=tile,
                                  tiled=[(dmerged, 0, d), (s["a"], 0, d), (s["b"], 0, d), (p, 6, d), (p, 7, d)], outs=[(d, BF16)] * 4)
    grads["w_a"] = _matmul([(s["ya"], 0, d_a, 0)], "tn", m=d, n=d, k=t, tm=512, tn=512, out_dtype=BF16, name=f"l{l}_dw_a")
    grads["w_b"] = _matmul([(s["yb"], 0, d_b, 0)], "tn", m=d, n=d, k=t, tm=512, tn=512, out_dtype=BF16, name=f"l{l}_dw_b")
    dya = _matmul([(d_a, 0, w["w_a"], 0)], "nt", m=t, n=d, k=d, tm=1024, tn=512, out_dtype=BF16, name=f"l{l}_d_ya")
    dyb = _matmul([(d_b, 0, w["w_b"], 0)], "nt", m=t, n=d, k=d, tm=1024, tn=512, out_dtype=F32, name=f"l{l}_d_yb")

    def conv_bwd(pid, tv, hv, bv):
        dy, cb, cc, cx = tv
        cw = bv[0]
        u, hu = cc * cx, hv[0] * hv[1]
        u1, u2 = _shift_rows(u, hu, 1, pid, True), _shift_rows(u, hu, 2, pid, True)
        conv = cw[0:1, :] * u2 + cw[1:2, :] * u1 + cw[2:3, :] * u
        dconv, hd = dy * cb, hv[2] * hv[3]
        du = (cw[2:3, :] * dconv + cw[1:2, :] * _shift_rows_up(dconv, hd, 1, pid, n_tiles)
              + cw[0:1, :] * _shift_rows_up(dconv, hd, 2, pid, n_tiles))
        return [dy * conv, du * cx, du * cc], [_colsum(dconv * u2), _colsum(dconv * u1), _colsum(dconv * u)]

    dcb, dcc, dcx, dcw0, dcw1, dcw2 = _rowwise(
        conv_bwd, f"l{l}_bwd_conv", rows=t, tile=tile, tiled=[(dyb, 0, d), (p, 3, d), (p, 4, d), (p, 5, d)],
        halos=[(p, 4, d, "prev"), (p, 5, d, "prev"), (dyb, 0, d, "next"), (p, 3, d, "next")], bcast=[w["conv"]],
        outs=[(d, BF16)] * 3, accs=[(1, d)] * 3)
    grads["conv"] = jnp.concatenate([dcw0, dcw1, dcw2], axis=0)

    dqs, dkn, dv = _attention_bwd(s["qn"], s["kn"], s["vb"], dya, f"l{l}_bwd_attention")

    def qk_norm_bwd(pid, tv, hv, bv):
        dq, dk, qr, kr, dvv = tv
        qg, kg, bd = bv

        def bwd(dy, xv, g):
            r = lax.rsqrt(_head_sums(xv * xv, bd) * (1.0 / HEAD_DIM) + EPS)
            yv = xv * r
            dyn = dy * g
            dx = r * (dyn - yv * (_head_sums(dyn * yv, bd) * (1.0 / HEAD_DIM)))
            return dx, _colsum(dy * yv)

        dxq, dgq = bwd(dq * 0.125, qr, qg)
        dxk, dgk = bwd(dk, kr, kg)
        return [dxq, dxk, dvv], [dgq, dgk]

    dqr, dkr, dvb, grads["qg"], grads["kg"] = _rowwise(
        qk_norm_bwd, f"l{l}_bwd_qk_norm", rows=t, tile=tile, tiled=[(dqs, 0, d), (dkn, 0, d), (p, 0, d), (p, 1, d), (dv, 0, d)],
        bcast=[w["qg"], w["kg"], bd2], outs=[(d, BF16)] * 3, accs=[(1, d)] * 2)

    dp = [dqr, dkr, dvb, dcb, dcc, dcx, dga, dgb]
    grads["w_in"] = [_matmul([(s["h"], 0, dpk, 0)], "tn", m=d, n=d, k=t, tm=512, tn=512, out_dtype=BF16, name=f"l{l}_dw_in{k}")
                     for k, dpk in enumerate(dp)]
    dh = _matmul([(dpk, 0, w["w_in"], k) for k, dpk in enumerate(dp)], "nt", m=t, n=d, k=d, tm=512, tn=512, out_dtype=F32,
                 name=f"l{l}_dh")
    dx, dsh1, dsc1, grads["ln1"] = _norm_backward(dh, s["x"], dx2, w["ln1"], sc1, f"l{l}_bwd_norm1", tile)
    return dx, grads, [dsh1, dsc1, dg1, dsh2, dsc2, dg2]


_BIG = ["w_in", "w_branch_a", "w_branch_b", "w_out", "w_ffn_gate", "w_ffn_up", "w_ffn_down"]
_SHORT = dict(w_in="w_in", w_branch_a="w_a", w_branch_b="w_b", w_out="w_o", w_ffn_gate="w_g", w_ffn_up="w_u", w_ffn_down="w_d")
_COL_SHARDED = {"w_in", "w_ffn_gate", "w_ffn_up"}


def kernel(x, c, ada_w, ada_b, ln1_g, w_in, q_norm_g, k_norm_g, conv_w, w_branch_a, w_branch_b, w_out, ln2_g, w_ffn_gate, w_ffn_up, w_ffn_down, loss_target, m_ada_w, m_ada_b, m_ln1_g, m_w_in, m_q_norm_g, m_k_norm_g, m_conv_w, m_w_branch_a, m_w_branch_b, m_w_out, m_ln2_g, m_w_ffn_gate, m_w_ffn_up, m_w_ffn_down, v_ada_w, v_ada_b, v_ln1_g, v_w_in, v_q_norm_g, v_k_norm_g, v_conv_w, v_w_branch_a, v_w_branch_b, v_w_out, v_ln2_g, v_w_ffn_gate, v_w_ffn_up, v_w_ffn_down):
    weights = dict(ada_w=ada_w, ada_b=ada_b, ln1_g=ln1_g, w_in=w_in, q_norm_g=q_norm_g, k_norm_g=k_norm_g, conv_w=conv_w,
                   w_branch_a=w_branch_a, w_branch_b=w_branch_b, w_out=w_out, ln2_g=ln2_g, w_ffn_gate=w_ffn_gate,
                   w_ffn_up=w_ffn_up, w_ffn_down=w_ffn_down)
    m_in = dict(ada_w=m_ada_w, ada_b=m_ada_b, ln1_g=m_ln1_g, w_in=m_w_in, q_norm_g=m_q_norm_g, k_norm_g=m_k_norm_g,
                conv_w=m_conv_w, w_branch_a=m_w_branch_a, w_branch_b=m_w_branch_b, w_out=m_w_out, ln2_g=m_ln2_g,
                w_ffn_gate=m_w_ffn_gate, w_ffn_up=m_w_ffn_up, w_ffn_down=m_w_ffn_down)
    v_in = dict(ada_w=v_ada_w, ada_b=v_ada_b, ln1_g=v_ln1_g, w_in=v_w_in, q_norm_g=v_q_norm_g, k_norm_g=v_k_norm_g,
                conv_w=v_conv_w, w_branch_a=v_w_branch_a, w_branch_b=v_w_branch_b, w_out=v_w_out, ln2_g=v_ln2_g,
                w_ffn_gate=v_w_ffn_gate, w_ffn_up=v_w_ffn_up, w_ffn_down=v_w_ffn_down)
    names = list(weights)

    mx, my, mc = lax.axis_index("x"), lax.axis_index("y"), lax.axis_index("c")
    me = 4 * mx + 2 * my + mc
    xs, target = x[0], loss_target[0]
    t, d = xs.shape
    depth = ada_w.shape[0]
    mod_cols = ada_w.shape[2]
    conv_cols = conv_w.shape[2]
    row_w = 1024
    tile = 512 if t % 512 == 0 else t

    small = jnp.concatenate([c.reshape(-1), conv_w.reshape(-1)])
    small_n = -(-small.shape[0] // row_w) * row_w
    small = jnp.pad(small, (0, small_n - small.shape[0])).reshape(-1, row_w)
    small_all = _all_gather([small], "gather_cond")[0].reshape(N_DEV, -1)
    c_all = small_all[:, :d]
    conv_full = jnp.transpose(small_all[:, d:d + depth * 3 * conv_cols].reshape(N_DEV, depth, 3, conv_cols), (1, 2, 0, 3)
                              ).reshape(depth, 3, N_DEV * conv_cols)

    ada_b_cols = lax.dynamic_slice_in_dim(ada_b, me * mod_cols, mod_cols, axis=1)
    mod_part = _ada_forward(c_all, ada_w, ada_b_cols)
    mod_all = _all_gather([mod_part.reshape(-1, row_w)], "gather_mod")[0].reshape(N_DEV, depth, N_DEV, mod_cols)
    mod_mine = lax.dynamic_index_in_dim(mod_all, me, axis=2, keepdims=False)
    mod = jnp.transpose(mod_mine, (1, 0, 2)).reshape(depth, 6, 1, d)

    f_shard = w_ffn_gate.shape[2]
    f_pad = -(-f_shard // LANES) * LANES - f_shard
    pads = dict(w_ffn_gate=((0, 0), (0, 0), (0, f_pad)), w_ffn_up=((0, 0), (0, 0), (0, f_pad)), w_ffn_down=((0, 0), (0, f_pad), (0, 0)))
    shards = [weights[n].astype(BF16) for n in _BIG]
    shards = [jnp.pad(a, pads[n]) if n in pads else a for n, a in zip(_BIG, shards)]
    axes = [1 if n in _COL_SHARDED else 0 for n in _BIG]
    rest = [a[l] for l in range(1, depth) for a in shards]
    rest_axes = axes * (depth - 1)
    first = _all_gather([a[0] for a in shards], "gather_weights_first", axes, place=rest, place_axes=rest_axes, after=[mod_all])
    first, rest_land = first[:len(_BIG)], first[len(_BIG):]
    rest_copies = functools.partial(_gather_copies, shapes=[a.shape for a in rest], axes=rest_axes)
    send_sems, recv_sems, rest, rest_land, started = _copies_start(
        rest, None, N_DEV - 1, rest_copies, "gather_weights_rest_start", lands=rest_land)

    def layer_weights(full, l):
        wl = {_SHORT[n]: a for n, a in zip(_BIG, full)}
        wl["ln1"], wl["ln2"] = ln1_g[l][None], ln2_g[l][None]
        wl["qg"] = jnp.tile(q_norm_g[l], d // HEAD_DIM)[None]
        wl["kg"] = jnp.tile(k_norm_g[l], d // HEAD_DIM)[None]
        wl["conv"] = conv_full[l]
        return wl

    layer_w = [layer_weights(first, 0)]
    layer_w[0]["ln1"] = layer_w[0]["ln1"] + started[:1, :1]

    saved = []
    x_cur, y_prev, gate_prev = xs, None, None
    for l in range(depth):
        if l == 1:
            _, landed = _copies_wait(send_sems, recv_sems, rest, rest_land, saved[0]["f"], rest_copies, "gather_weights_rest_wait")
            layer_w += [layer_weights(landed[(k - 1) * len(_BIG):k * len(_BIG)], k) for k in range(1, depth)]
        mods = [mod[l, k] for k in range(6)]
        s = _layer_forward(x_cur, y_prev, gate_prev, mods, layer_w[l], l, tile)
        saved.append(s)
        x_cur, y_prev, gate_prev = s["x2"], s["f"], mods[5]

    def loss_head(pid, tv, hv, bv):
        diff = tv[0] + bv[0] * tv[1] - tv[2]
        return [diff * (1.0 / d)], [_colsum(diff * diff) * (0.5 / d)]

    dx, loss_cols = _rowwise(loss_head, "loss_head", rows=t, tile=tile, tiled=[(x_cur, 0, d), (y_prev, 0, d), (target, 0, d)],
                             bcast=[gate_prev], outs=[(d, F32)], accs=[(1, d)])

    def by_owner(n, full):
        if n == "w_in":
            return jnp.stack(full)
        if n in _COL_SHARDED:
            return jnp.transpose(full.reshape(full.shape[0], N_DEV, -1), (1, 0, 2))
        return full.reshape(N_DEV, -1, full.shape[1])

    layer_g, dmods, parts, recv2, pending = [None] * depth, [None] * depth, [None] * depth, [None] * depth, {}
    started = None
    for l in reversed(range(depth)):
        mods = [mod[l, k] for k in range(6)]
        if started is not None:
            mods[5] = mods[5] + started[:1, :1]
        dx, layer_g[l], dmods[l] = _layer_backward(dx, saved[l], mods, layer_w[l], l, tile)
        tensors = [by_owner(n, layer_g[l][_SHORT[n]]) for n in _BIG]
        recv1 = _exchange_sibling(tensors, f"rs_exchange_sibling_l{l}")
        parts[l] = [_add_sibling(a, r, f"rs_add_sibling_l{l}_{i}") for i, (a, r) in enumerate(zip(tensors, recv1))]
        if l > 0:
            *pending[l], started = _copies_start(parts[l], [(3,) + a.shape[1:] for a in parts[l]], 3, _chip_copies,
                                                 f"rs_exchange_chips_l{l}_start")
        else:
            recv2[l] = _exchange_chips(parts[l], "rs_exchange_chips_l0")
    for l, (s_sems, r_sems, srcs, lands) in pending.items():
        parts[l], recv2[l] = _copies_wait(s_sems, r_sems, srcs, lands, dx, _chip_copies, f"rs_exchange_chips_l{l}_wait")
    grad_x = dx[None]

    pieces = [jnp.concatenate(dmods[l], axis=1) for l in range(depth)]
    for key in ("ln1", "ln2", "qg", "kg"):
        pieces += [layer_g[l][key] for l in range(depth)]
    pieces += [layer_g[l]["conv"].reshape(1, -1) for l in range(depth)]
    pieces.append(loss_cols)
    part_small = jnp.concatenate(pieces, axis=1).reshape(-1, row_w)
    part_all = _all_gather([part_small], "gather_small_grads")[0]
    summed = _device_sum(part_all).reshape(-1)
    n_mod = depth * 6 * d
    dmod_all = part_all.reshape(N_DEV, -1)[:, :n_mod].reshape(N_DEV, depth, 6 * d)
    dmod_cols = jnp.transpose(lax.dynamic_slice_in_dim(dmod_all, me * mod_cols, mod_cols, axis=2), (1, 0, 2))
    g = {"ada_w": _ada_backward(c_all, dmod_cols), "ada_b": summed[:n_mod].reshape(depth, 6 * d)}
    off = n_mod
    g["ln1_g"] = summed[off:off + depth * d].reshape(depth, d)
    g["ln2_g"] = summed[off + depth * d:off + 2 * depth * d].reshape(depth, d)
    g["q_norm_g"] = summed[off + 2 * depth * d:off + 3 * depth * d].reshape(depth, d // HEAD_DIM, HEAD_DIM).sum(axis=1)
    g["k_norm_g"] = summed[off + 3 * depth * d:off + 4 * depth * d].reshape(depth, d // HEAD_DIM, HEAD_DIM).sum(axis=1)
    off += 4 * depth * d
    conv_g = summed[off:off + depth * 3 * d].reshape(depth, 3, N_DEV, conv_cols)
    g["conv_w"] = lax.dynamic_index_in_dim(conv_g, me, axis=2, keepdims=False)
    off += depth * 3 * d
    loss = jnp.sum(summed[off:off + d])

    def shard_of(n, a):
        if n in ("w_ffn_gate", "w_ffn_up"):
            return a[:, :f_shard]
        return a[:f_shard] if n == "w_ffn_down" else a

    outs = {}
    for k, n in enumerate(_BIG):
        g_layers = [[shard_of(n, a) for a in (parts[l][k][0], *recv2[l][k])] for l in range(depth)]
        outs[n] = _adamw(weights[n], g_layers, m_in[n], v_in[n], f"adamw_{n}")
    for n in names:
        if n not in outs:
            outs[n] = _adamw(weights[n], [[g[n]]], m_in[n], v_in[n], f"adamw_{n}")
    return (loss, grad_x, *[outs[n][0] for n in names], *[outs[n][1] for n in names], *[outs[n][2] for n in names],
            *[outs[n][3] for n in names])
```

```python
import functools

import numpy as np
import jax
import jax.numpy as jnp
from jax import lax
from jax.experimental import pallas as pl
from jax.experimental.pallas import tpu as pltpu

F32, BF16 = jnp.float32, jnp.bfloat16
MESH = pl.DeviceIdType.MESH
N_DEV = 8
LANES = 128
HEAD_DIM = 64
HEAD_PAIR = 2 * HEAD_DIM
ATT_TILE = 256
ATT_UNROLL = 4
EPS = 1e-6
VMEM_LIMIT = 56 * 1024 * 1024

ADAM_LR, ADAM_B1, ADAM_B2, ADAM_EPS, ADAM_WD, ADAM_STEP = 0.001, 0.9, 0.999, 1e-08, 0.01, 10

_NT = (((1,), (1,)), ((), ()))
_TN = (((0,), (0,)), ((), ()))
_NN = (((1,), (0,)), ((), ()))


def _params(n_grid):
    return pltpu.CompilerParams(dimension_semantics=("arbitrary",) * n_grid, vmem_limit_bytes=VMEM_LIMIT)


def _sigmoid(x):
    return 1.0 / (1.0 + jnp.exp(-x))


def _rowwise(fn, name, *, rows, tile, tiled=(), halos=(), bcast=(), outs=(), accs=()):
    n = rows // tile
    assert n * tile == rows
    in_specs, args = [], []
    for t in tiled:
        arr, cb, w = t[:3]
        rowmap = t[3] if len(t) > 3 else (lambda i: i)
        in_specs.append(pl.BlockSpec((tile, w), functools.partial(lambda i, cb, rowmap: (rowmap(i), cb), cb=cb, rowmap=rowmap)))
        args.append(arr)
    per_tile8 = tile // 8
    for arr, cb, w, side in halos:
        last8 = arr.shape[0] // 8 - 1
        if side == "prev":
            imap = functools.partial(lambda i, cb: (jnp.maximum(i * per_tile8 - 1, 0), cb), cb=cb)
        else:
            imap = functools.partial(lambda i, cb, last8: (jnp.minimum((i + 1) * per_tile8, last8), cb), cb=cb, last8=last8)
        in_specs.append(pl.BlockSpec((8, w), imap))
        args.append(arr)
    for arr in bcast:
        in_specs.append(pl.BlockSpec(arr.shape, functools.partial(lambda i, nd: (0,) * nd, nd=arr.ndim)))
        args.append(arr)
    out_shape = [jax.ShapeDtypeStruct((rows, w), dt) for w, dt in outs] + [jax.ShapeDtypeStruct(s, F32) for s in accs]
    out_specs = [pl.BlockSpec((tile, w), lambda i: (i, 0)) for w, _ in outs] + [pl.BlockSpec(s, lambda i: (0, 0)) for s in accs]
    nt, nh, nb, no, na = len(tiled), len(halos), len(bcast), len(outs), len(accs)

    def body(*refs):
        pid = pl.program_id(0)
        tv = [r[...] for r in refs[:nt]]
        hv = [r[...] for r in refs[nt:nt + nh]]
        bv = [r[...] for r in refs[nt + nh:nt + nh + nb]]
        out_refs = refs[nt + nh + nb:nt + nh + nb + no]
        acc_refs = refs[nt + nh + nb + no:]
        ov, av = fn(pid, tv, hv, bv)
        for r, v in zip(out_refs, ov):
            r[...] = v.astype(r.dtype)
        if na:
            @pl.when(pid == 0)
            def _():
                for r in acc_refs:
                    r[...] = jnp.zeros(r.shape, F32)
            for r, v in zip(acc_refs, av):
                r[...] += v

    res = pl.pallas_call(body, name=name, grid=(n,), in_specs=in_specs, out_specs=out_specs, out_shape=out_shape,
                         compiler_params=_params(1))(*args)
    return res


def _colsum(v):
    return jnp.sum(v, axis=0, keepdims=True)


def _matmul(pairs, mode, *, m, n, k, tm, tn, out_dtype, name):
    tm, tn = min(tm, m), min(tn, n)
    assert m % tm == 0 and n % tn == 0
    in_specs, args = [], []
    for a, acb, b, bcb in pairs:
        if mode == "tn":
            in_specs.append(pl.BlockSpec((k, tm), functools.partial(lambda i, j, o: (0, o + i), o=acb)))
            in_specs.append(pl.BlockSpec((k, tn), functools.partial(lambda i, j, o: (0, o + j), o=bcb)))
        elif mode == "nn":
            in_specs.append(pl.BlockSpec((tm, k), functools.partial(lambda i, j, o: (i, o), o=acb)))
            in_specs.append(pl.BlockSpec((k, tn), functools.partial(lambda i, j, o: (0, o + j), o=bcb)))
        else:
            in_specs.append(pl.BlockSpec((tm, k), functools.partial(lambda i, j, o: (i, o), o=acb)))
            in_specs.append(pl.BlockSpec((tn, k), functools.partial(lambda i, j, o: (j, o), o=bcb)))
        args += [a, b]
    dims = {"nn": _NN, "nt": _NT, "tn": _TN}[mode]
    npairs = len(pairs)

    def body(*refs):
        o_ref = refs[2 * npairs]
        acc = None
        for p in range(npairs):
            d = lax.dot_general(refs[2 * p][...].astype(BF16), refs[2 * p + 1][...].astype(BF16), dims,
                                preferred_element_type=F32)
            acc = d if acc is None else acc + d
        o_ref[...] = acc.astype(o_ref.dtype)

    return pl.pallas_call(body, name=name, grid=(m // tm, n // tn), in_specs=in_specs,
                          out_specs=pl.BlockSpec((tm, tn), lambda i, j: (i, j)),
                          out_shape=jax.ShapeDtypeStruct((m, n), out_dtype), compiler_params=_params(2))(*args)


def _scan_matrix(kind, n):
    r = np.arange(n)
    tri = (r[:, None] > r[None, :]) if kind == "suffix" else (r[:, None] < r[None, :])
    half = np.concatenate([tri.astype(np.float32), np.ones((n, LANES), np.float32)], axis=1)
    return jnp.asarray(np.concatenate([half, half], axis=0), BF16)


def _head_sum_matrix():
    r = np.arange(LANES)
    bd = (r[:, None] // HEAD_DIM == r[None, :] // HEAD_DIM).astype(np.float32)
    return jnp.asarray(np.concatenate([bd, bd], axis=0), BF16)


def _split_cat(v):
    hi = v.astype(BF16)
    lo = (v - hi.astype(F32)).astype(BF16)
    return jnp.concatenate([hi, lo], axis=1)


def _head_sums(v, bd2):
    hi = v.astype(BF16)
    lo = (v - hi.astype(F32)).astype(BF16)
    parts = []
    for g in range(v.shape[1] // LANES):
        sl = slice(g * LANES, (g + 1) * LANES)
        parts.append(jnp.dot(jnp.concatenate([hi[:, sl], lo[:, sl]], axis=1), bd2, preferred_element_type=F32))
    return jnp.concatenate(parts, axis=1)


def _scan_parts(v, scan):
    return [jnp.dot(_split_cat(v[:, b * LANES:(b + 1) * LANES]), scan, preferred_element_type=F32)
            for b in range(v.shape[1] // LANES)]


def _chain_sums(parts, carry, reverse):
    nb = len(parts)
    outs = [None] * nb
    for b in (reversed(range(nb)) if reverse else range(nb)):
        outs[b] = carry + parts[b][:, :LANES]
        carry = carry + parts[b][:, LANES:]
    return jnp.concatenate(outs, axis=1), carry


def _log_weights(z, mask):
    sp = jnp.log(1.0 + jnp.exp(-jnp.abs(z)))
    log_not = -(jnp.maximum(z, 0.0) + sp)
    log_beta = z + log_not
    return (log_not if mask is None else jnp.where(mask, log_not, 0.0)), log_beta


def _pipeline(n_chains, stages):
    for step in range(n_chains + len(stages) - 1):
        for s, stage in enumerate(stages):
            if 0 <= step - s < n_chains:
                stage(step - s)


def _head_masks(tq):
    lane = lax.broadcasted_iota(jnp.int32, (tq, HEAD_PAIR), 1)
    return [(lane // HEAD_DIM) == hh for hh in range(2)]


def _diag_mask(tq):
    return lax.broadcasted_iota(jnp.int32, (tq, tq), 1) < lax.broadcasted_iota(jnp.int32, (tq, tq), 0)


def _attention_fwd(qn, kn, vb, name):
    t, d = qn.shape
    tq = min(ATT_TILE, t)
    nq, hp = t // tq, d // HEAD_PAIR
    scan_suffix = _scan_matrix("suffix", LANES)

    def body(q_ref, k_ref, v_ref, sc_ref, o_ref):
        qi = pl.program_id(1)
        q2 = q_ref[...].astype(F32)
        scan = sc_ref[...]
        heads, diag = _head_masks(tq), _diag_mask(tq)
        qms = [jnp.where(h, q2, 0.0).astype(BF16) for h in heads]

        def rows_of(j):
            return pl.ds(pl.multiple_of(j * tq, tq), tq)

        def step(js, state, mask):
            carry, acc = [state[0], state[2]], [state[1], state[3]]
            kts, vts = [k_ref[rows_of(j), :] for j in js], [v_ref[rows_of(j), :] for j in js]
            z, lw, parts, a = {}, {}, {}, {}

            def s_scores(c):
                z[c] = lax.dot_general(qms[c % 2], kts[c // 2], _NT, preferred_element_type=F32)

            def s_logs(c):
                lw[c] = _log_weights(z.pop(c), mask)

            def s_scan(c):
                parts[c] = _scan_parts(lw[c][0], scan)

            def s_weights(c):
                tail, carry[c % 2] = _chain_sums(parts.pop(c), carry[c % 2], True)
                av = jnp.exp(lw.pop(c)[1] + tail)
                a[c] = (av if mask is None else jnp.where(mask, av, 0.0)).astype(BF16)

            def s_values(c):
                acc[c % 2] = acc[c % 2] + jnp.dot(a.pop(c), vts[c // 2], preferred_element_type=F32)

            _pipeline(2 * len(js), [s_scores, s_logs, s_scan, s_weights, s_values])
            return (carry[0], acc[0], carry[1], acc[1])

        zero, zq = jnp.zeros((tq, LANES), F32), jnp.zeros((tq, HEAD_PAIR), F32)
        state = step([qi], (zero, zq, zero, zq), diag)
        rem = qi % ATT_UNROLL
        odd = rem % 2
        state = lax.fori_loop(0, odd, lambda p, c: step([qi - 1], c, None), state)
        state = lax.fori_loop(0, rem // 2, lambda p, c: step([qi - 1 - odd, qi - 2 - odd], c, None), state)
        state = lax.fori_loop(0, qi // ATT_UNROLL,
                              lambda p, c: step([qi - 1 - rem - ATT_UNROLL * p - u for u in range(ATT_UNROLL)], c, None), state)
        o_ref[...] = jnp.where(heads[0], state[1], state[3]).astype(o_ref.dtype)

    return pl.pallas_call(
        body, name=name, grid=(hp, nq),
        in_specs=[pl.BlockSpec((tq, HEAD_PAIR), lambda h, i: (i, h)),
                  pl.BlockSpec((t, HEAD_PAIR), lambda h, i: (0, h)),
                  pl.BlockSpec((t, HEAD_PAIR), lambda h, i: (0, h)),
                  pl.BlockSpec(scan_suffix.shape, lambda h, i: (0, 0))],
        out_specs=pl.BlockSpec((tq, HEAD_PAIR), lambda h, i: (i, h)),
        out_shape=jax.ShapeDtypeStruct((t, d), BF16), compiler_params=_params(2))(qn, kn, vb, scan_suffix)


def _attention_bwd(qn, kn, vb, dob, name):
    t, d = qn.shape
    tq = min(ATT_TILE, t)
    nq, hp = t // tq, d // HEAD_PAIR
    scan_suffix, scan_prefix = _scan_matrix("suffix", LANES), _scan_matrix("prefix", LANES)

    def body(q_ref, k_ref, v_ref, do_ref, ss_ref, sp_ref, dq_ref, dk_ref, dv_ref, g_s, b_s):
        qi = pl.program_id(1)

        @pl.when(qi == 0)
        def _():
            dk_ref[...] = jnp.zeros(dk_ref.shape, F32)
            dv_ref[...] = jnp.zeros(dv_ref.shape, F32)

        q2, do2 = q_ref[...].astype(F32), do_ref[...].astype(F32)
        ssuf, spre = ss_ref[...], sp_ref[...]
        heads, diag = _head_masks(tq), _diag_mask(tq)
        qms = [jnp.where(h, q2, 0.0).astype(BF16) for h in heads]
        doms = [jnp.where(h, do2, 0.0).astype(BF16) for h in heads]

        def rows_of(j):
            return pl.ds(pl.multiple_of(j * tq, tq), tq)

        def pass_one(js, carries, mask):
            carry = list(carries)
            kts, vts = [k_ref[rows_of(j), :] for j in js], [v_ref[rows_of(j), :] for j in js]
            z, d_a, lw, parts, a, dv = {}, {}, {}, {}, {}, {}

            def s_scores(c):
                z[c] = lax.dot_general(qms[c % 2], kts[c // 2], _NT, preferred_element_type=F32)
                d_a[c] = lax.dot_general(doms[c % 2], vts[c // 2], _NT, preferred_element_type=F32)

            def s_logs(c):
                lw[c] = _log_weights(z.pop(c), mask)

            def s_scan(c):
                parts[c] = _scan_parts(lw[c][0], ssuf)

            def s_weights(c):
                tail, carry[c % 2] = _chain_sums(parts.pop(c), carry[c % 2], True)
                log_beta = lw.pop(c)[1]
                av, beta = jnp.exp(log_beta + tail), jnp.exp(log_beta)
                if mask is not None:
                    av, beta = jnp.where(mask, av, 0.0), jnp.where(mask, beta, 0.0)
                g_s[c % 2, js[c // 2]] = av * d_a.pop(c)
                b_s[c % 2, js[c // 2]] = beta
                a[c] = av.astype(BF16)

            def s_values(c):
                dv[c] = lax.dot_general(a.pop(c), doms[c % 2], _TN, preferred_element_type=F32)
                if c % 2 == 1:
                    dv_ref[rows_of(js[c // 2]), :] += dv.pop(c - 1) + dv.pop(c)

            _pipeline(2 * len(js), [s_scores, s_logs, s_scan, s_weights, s_values])
            return tuple(carry)

        zero = jnp.zeros((tq, LANES), F32)
        rem = qi % ATT_UNROLL
        carries = pass_one([qi], (zero, zero), diag)
        odd = rem % 2
        carries = lax.fori_loop(0, odd, lambda p, c: pass_one([qi - 1], c, None), carries)
        carries = lax.fori_loop(0, rem // 2, lambda p, c: pass_one([qi - 1 - odd, qi - 2 - odd], c, None), carries)
        lax.fori_loop(0, qi // ATT_UNROLL,
                      lambda p, c: pass_one([qi - 1 - rem - ATT_UNROLL * p - u for u in range(ATT_UNROLL)], c, None), carries)

        def pass_two(js, state):
            prefix, dq = [state[0], state[2]], [state[1], state[3]]
            kts = [k_ref[rows_of(j), :] for j in js]
            gb, parts, dz, dk = {}, {}, {}, {}

            def s_scan(c):
                gb[c] = (g_s[c % 2, js[c // 2]], b_s[c % 2, js[c // 2]])
                parts[c] = _scan_parts(gb[c][0], spre)

            def s_dz(c):
                before, prefix[c % 2] = _chain_sums(parts.pop(c), prefix[c % 2], False)
                g, beta = gb.pop(c)
                dz[c] = (g - beta * (g + before)).astype(BF16)

            def s_grads(c):
                dzc = dz.pop(c)
                dq[c % 2] = dq[c % 2] + jnp.dot(dzc, kts[c // 2], preferred_element_type=F32)
                dk[c] = lax.dot_general(dzc, qms[c % 2], _TN, preferred_element_type=F32)
                if c % 2 == 1:
                    dk_ref[rows_of(js[c // 2]), :] += dk.pop(c - 1) + dk.pop(c)

            _pipeline(2 * len(js), [s_scan, s_dz, s_grads])
            return (prefix[0], dq[0], prefix[1], dq[1])

        zq = jnp.zeros((tq, HEAD_PAIR), F32)
        first = (qi + 1) % ATT_UNROLL
        lone = first % 2
        out = lax.fori_loop(0, lone, lambda p, c: pass_two([0], c), (zero, zq, zero, zq))
        out = lax.fori_loop(0, first // 2, lambda p, c: pass_two([lone, lone + 1], c), out)
        out = lax.fori_loop(0, (qi + 1) // ATT_UNROLL,
                            lambda p, c: pass_two([first + ATT_UNROLL * p + u for u in range(ATT_UNROLL)], c), out)
        dq_ref[...] = jnp.where(heads[0], out[1], out[3])

    blk = pl.BlockSpec((tq, HEAD_PAIR), lambda h, i: (i, h))
    col_spec = pl.BlockSpec((t, HEAD_PAIR), lambda h, i: (0, h))
    const = pl.BlockSpec(scan_suffix.shape, lambda h, i: (0, 0))
    full = jax.ShapeDtypeStruct((t, d), F32)
    return pl.pallas_call(
        body, name=name, grid=(hp, nq), in_specs=[blk, col_spec, col_spec, blk, const, const],
        out_specs=[blk, col_spec, col_spec], out_shape=[full, full, full],
        scratch_shapes=[pltpu.VMEM((2, nq, tq, tq), F32), pltpu.VMEM((2, nq, tq, tq), F32)],
        compiler_params=_params(2))(qn, kn, vb, dob, scan_suffix, scan_prefix)


def _chip_at(x, y, j):
    return (1 - x if j & 2 else x, 1 - y if j & 1 else y)


def _block_of(ref, dev, shape, axis):
    if axis is None:
        return ref.at[dev]
    if axis == 0:
        return ref.at[pl.ds(pl.multiple_of(dev * shape[0], shape[0]), shape[0]), :]
    return ref.at[:, pl.ds(pl.multiple_of(dev * shape[1], shape[1]), shape[1])]


def _gathered_shape(shape, axis):
    if axis is None:
        return (N_DEV,) + shape
    return (N_DEV * shape[0], shape[1]) if axis == 0 else (shape[0], N_DEV * shape[1])


def _all_gather(shards, name, axes=None, place=(), place_axes=(), after=()):
    n, m, na = len(shards), len(place), len(after)
    axes = [None] * n if axes is None else axes

    def body(*refs):
        x_refs, p_refs = refs[:n], refs[n:n + m]
        out_refs, land_refs = refs[n + m + na:2 * n + m + na], refs[2 * n + m + na:2 * n + 2 * m + na]
        send_sems, recv_sems, local_sems = refs[2 * n + 2 * m + na:]
        x, y, c = lax.axis_index("x"), lax.axis_index("y"), lax.axis_index("c")
        me, sibling = (x, y, c), (x, y, 1 - c)
        chips = [_chip_at(x, y, j) for j in (1, 2, 3)]

        def rows(t, px, py, pc):
            return _block_of(out_refs[t], 4 * px + 2 * py + pc, shards[t].shape, axes[t])

        def copy(t, k, block, to, src=None):
            return pltpu.make_async_remote_copy(src_ref=rows(t, *block) if src is None else src, dst_ref=rows(t, *block),
                                                send_sem=send_sems.at[t, k], recv_sem=recv_sems.at[t, k],
                                                device_id=to, device_id_type=MESH)

        mine = [pltpu.make_async_copy(x_refs[t], rows(t, *me), local_sems.at[t]) for t in range(n)]
        mine += [pltpu.make_async_copy(p_refs[u], _block_of(land_refs[u], 4 * x + 2 * y + c, place[u].shape, place_axes[u]),
                                       local_sems.at[n + u]) for u in range(m)]
        for cp in mine:
            cp.start()
        first = []
        for t in range(n):
            first.append(copy(t, 0, me, sibling, src=x_refs[t]))
            first += [copy(t, 1 + j, me, (*chip, c), src=x_refs[t]) for j, chip in enumerate(chips)]
        for cp in first:
            cp.start()
        passed = []
        for t in range(n):
            for j, chip in enumerate(chips):
                copy(t, 1 + j, (*chip, c), me).wait_recv()
                passed.append(copy(t, 4 + j, (*chip, c), sibling))
                passed[-1].start()
        for t in range(n):
            copy(t, 0, sibling, me).wait_recv()
            for j, chip in enumerate(chips):
                copy(t, 4 + j, (*chip, 1 - c), me).wait_recv()
        for cp in first + passed:
            cp.wait_send()
        for cp in mine:
            cp.wait()

    any_spec = pl.BlockSpec(memory_space=pl.ANY)
    out_shape = [jax.ShapeDtypeStruct(_gathered_shape(a.shape, ax), a.dtype) for a, ax in zip((*shards, *place), (*axes, *place_axes))]
    return pl.pallas_call(
        body, name=name, out_shape=out_shape, in_specs=[any_spec] * (n + m + na), out_specs=[any_spec] * (n + m),
        scratch_shapes=[pltpu.SemaphoreType.DMA((n, 7)), pltpu.SemaphoreType.DMA((n, 7)), pltpu.SemaphoreType.DMA((n + m,))])(
            *shards, *place, *after)


def _place_own(shards, axes, name):
    n = len(shards)

    def body(*refs):
        x_refs, out_refs, sems = refs[:n], refs[n:2 * n], refs[2 * n]
        dev = 4 * lax.axis_index("x") + 2 * lax.axis_index("y") + lax.axis_index("c")
        copies = [pltpu.make_async_copy(x_refs[t], _block_of(out_refs[t], dev, shards[t].shape, axes[t]), sems.at[t]) for t in range(n)]
        for cp in copies:
            cp.start()
        for cp in copies:
            cp.wait()

    any_spec = pl.BlockSpec(memory_space=pl.ANY)
    return pl.pallas_call(
        body, name=name, out_shape=[jax.ShapeDtypeStruct(_gathered_shape(a.shape, ax), a.dtype) for a, ax in zip(shards, axes)],
        in_specs=[any_spec] * n, out_specs=[any_spec] * n, scratch_shapes=[pltpu.SemaphoreType.DMA((n,))])(*shards)


def _exchange_sibling(tensors, name):
    n = len(tensors)

    def body(*refs):
        g_refs, recv_refs = refs[:n], refs[n:2 * n]
        send_sems, recv_sems = refs[2 * n:]
        x, y, c = lax.axis_index("x"), lax.axis_index("y"), lax.axis_index("c")
        copies = []
        for t in range(n):
            for j in range(4):
                cx, cy = _chip_at(x, y, j)
                copies.append(pltpu.make_async_remote_copy(
                    src_ref=g_refs[t].at[4 * cx + 2 * cy + (1 - c)], dst_ref=recv_refs[t].at[j], send_sem=send_sems.at[t, j],
                    recv_sem=recv_sems.at[t, j], device_id=(x, y, 1 - c), device_id_type=MESH))
        for cp in copies:
            cp.start()
        for cp in copies:
            cp.wait_recv()
        for cp in copies:
            cp.wait_send()

    any_spec = pl.BlockSpec(memory_space=pl.ANY)
    return pl.pallas_call(
        body, name=name, out_shape=[jax.ShapeDtypeStruct((4,) + a.shape[1:], a.dtype) for a in tensors],
        in_specs=[any_spec] * n, out_specs=[any_spec] * n,
        scratch_shapes=[pltpu.SemaphoreType.DMA((n, 4)), pltpu.SemaphoreType.DMA((n, 4))])(*tensors)


def _exchange_chips(parts, name):
    n = len(parts)

    def body(*refs):
        p_refs, recv_refs = refs[:n], refs[n:2 * n]
        send_sems, recv_sems = refs[2 * n:]
        x, y, c = lax.axis_index("x"), lax.axis_index("y"), lax.axis_index("c")
        copies = []
        for t in range(n):
            for j in (1, 2, 3):
                copies.append(pltpu.make_async_remote_copy(
                    src_ref=p_refs[t].at[j], dst_ref=recv_refs[t].at[j - 1], send_sem=send_sems.at[t, j - 1],
                    recv_sem=recv_sems.at[t, j - 1], device_id=(*_chip_at(x, y, j), c), device_id_type=MESH))
        for cp in copies:
            cp.start()
        for cp in copies:
            cp.wait_recv()
        for cp in copies:
            cp.wait_send()

    any_spec = pl.BlockSpec(memory_space=pl.ANY)
    return pl.pallas_call(
        body, name=name, out_shape=[jax.ShapeDtypeStruct((3,) + a.shape[1:], a.dtype) for a in parts],
        in_specs=[any_spec] * n, out_specs=[any_spec] * n,
        scratch_shapes=[pltpu.SemaphoreType.DMA((n, 3)), pltpu.SemaphoreType.DMA((n, 3))])(*parts)


_HBM = pl.BlockSpec(memory_space=pltpu.HBM)
_SEM = pl.BlockSpec(memory_space=pltpu.SEMAPHORE)
_DATAFLOW = pltpu.SideEffectType.DATAFLOW_SIDE_EFFECTING


def _peer(x, y, c, mask):
    return (1 - x if mask & 4 else x, 1 - y if mask & 2 else y, 1 - c if mask & 1 else c)


def _gather_copies(x_refs, land_refs, send_sems, recv_sems, waiting, shapes, axes):
    x, y, c = lax.axis_index("x"), lax.axis_index("y"), lax.axis_index("c")
    copies = []
    for t in range(len(x_refs)):
        for mask in range(1, N_DEV):
            px, py, pc = _peer(x, y, c, mask)
            block = 4 * px + 2 * py + pc if waiting else 4 * x + 2 * y + c
            copies.append(pltpu.make_async_remote_copy(
                src_ref=x_refs[t], dst_ref=_block_of(land_refs[t], block, shapes[t], axes[t]), send_sem=send_sems.at[7 * t + mask - 1],
                recv_sem=recv_sems.at[7 * t + mask - 1], device_id=(px, py, pc), device_id_type=MESH))
    return copies


def _chip_copies(p_refs, land_refs, send_sems, recv_sems, waiting):
    x, y, c = lax.axis_index("x"), lax.axis_index("y"), lax.axis_index("c")
    return [pltpu.make_async_remote_copy(src_ref=p_refs[t].at[j], dst_ref=land_refs[t].at[j - 1], send_sem=send_sems.at[3 * t + j - 1],
                                         recv_sem=recv_sems.at[3 * t + j - 1], device_id=(*_chip_at(x, y, j), c), device_id_type=MESH)
            for t in range(len(p_refs)) for j in (1, 2, 3)]


def _copies_start(srcs, land_shapes, per_src, copies, name, lands=None):
    n = len(srcs)

    def body(*refs):
        for cp in copies(refs[:n], refs[n:2 * n], refs[2 * n], refs[2 * n + 1], False):
            cp.start()
        refs[-1][...] = jnp.zeros(refs[-1].shape, F32)

    if lands is None:
        lands = [lax.empty(shape, a.dtype) for shape, a in zip(land_shapes, srcs)]
    args =[pltpu.with_memory_space_constraint(a, pltpu.HBM) for a in (*srcs, *lands)]
    sems = pltpu.SemaphoreType.DMA((n * per_src,))
    res = pl.pallas_call(
        body, name=name, out_shape=(sems, sems, *[pltpu.HBM(a.shape, a.dtype) for a in args], jax.ShapeDtypeStruct((8, LANES), F32)),
        in_specs=[_HBM] * (2 * n), out_specs=(_SEM, _SEM, *[_HBM] * (2 * n), pl.BlockSpec(memory_space=pltpu.VMEM)),
        input_output_aliases={i: 2 + i for i in range(2 * n)}, compiler_params=pltpu.CompilerParams(has_side_effects=_DATAFLOW))(*args)
    return res[0], res[1], list(res[2:2 + n]), list(res[2 + n:2 + 2 * n]), res[-1]


def _copies_wait(send_sems, recv_sems, srcs, lands, after, copies, name):
    n = len(srcs)

    def body(*refs):
        for cp in copies(refs[:n], refs[n:2 * n], refs[2 * n], refs[2 * n + 1], True):
            cp.wait_send()
            cp.wait_recv()

    res = pl.pallas_call(
        body, name=name, out_shape=tuple(pltpu.HBM(a.shape, a.dtype) for a in (*srcs, *lands)),
        in_specs=[_HBM] * (2 * n) + [_SEM, _SEM, pl.BlockSpec(memory_space=pl.ANY)], out_specs=tuple([_HBM] * (2 * n)),
        input_output_aliases={i: i for i in range(2 * n)}, compiler_params=pltpu.CompilerParams(has_side_effects=_DATAFLOW))(
            *srcs, *lands, send_sems, recv_sems, after)
    return list(res[:n]), list(res[n:])


def _add_sibling(g8, recv1, name):
    _, a, b = g8.shape
    ta = next(cand for cand in (512, 384, 256, 128) if a % cand == 0)

    def body(g_ref, r_ref, o_ref):
        o_ref[...] = (g_ref[...].astype(F32) + r_ref[...].astype(F32)).astype(o_ref.dtype)

    def own(j, i):
        x, y, c = lax.axis_index("x"), lax.axis_index("y"), lax.axis_index("c")
        return (4 * (x ^ (j >> 1)) + 2 * (y ^ (j & 1)) + c, i, 0)

    slot = pl.BlockSpec((None, ta, b), lambda j, i: (j, i, 0))
    return pl.pallas_call(body, name=name, grid=(4, a // ta), in_specs=[pl.BlockSpec((None, ta, b), own), slot],
                          out_specs=slot, out_shape=jax.ShapeDtypeStruct((4, a, b), g8.dtype),
                          compiler_params=_params(2))(g8, recv1)


def _ada_forward(c_all, ada_w, ada_b_cols):
    depth, d, cols = ada_w.shape

    def body(c_ref, w_ref, b_ref, o_ref):
        cv = c_ref[...]
        act = cv * _sigmoid(cv)
        o_ref[...] = jnp.dot(act, w_ref[...], preferred_element_type=F32, precision=lax.Precision.HIGHEST) + b_ref[...]

    return pl.pallas_call(
        body, name="ada_forward", grid=(depth,),
        in_specs=[pl.BlockSpec((N_DEV, d), lambda l: (0, 0)), pl.BlockSpec((None, d, cols), lambda l: (l, 0, 0)),
                  pl.BlockSpec((None, 1, cols), lambda l: (l, 0, 0))],
        out_specs=pl.BlockSpec((None, N_DEV, cols), lambda l: (l, 0, 0)),
        out_shape=jax.ShapeDtypeStruct((depth, N_DEV, cols), F32), compiler_params=_params(1))(
            c_all, ada_w, ada_b_cols.reshape(depth, 1, cols))


def _ada_backward(c_all, dmod_cols):
    depth, _, cols = dmod_cols.shape
    d = c_all.shape[1]

    def body(c_ref, g_ref, o_ref):
        cv = c_ref[...]
        act = cv * _sigmoid(cv)
        o_ref[...] = lax.dot_general(act, g_ref[...], _TN, preferred_element_type=F32, precision=lax.Precision.HIGHEST)

    return pl.pallas_call(
        body, name="ada_backward", grid=(depth,),
        in_specs=[pl.BlockSpec((N_DEV, d), lambda l: (0, 0)), pl.BlockSpec((None, N_DEV, cols), lambda l: (l, 0, 0))],
        out_specs=pl.BlockSpec((None, d, cols), lambda l: (l, 0, 0)),
        out_shape=jax.ShapeDtypeStruct((depth, d, cols), F32), compiler_params=_params(1))(c_all, dmod_cols)


def _device_sum(a):
    _, r, w = a.shape

    def body(a_ref, o_ref):
        acc = a_ref[0]
        for dev in range(1, N_DEV):
            acc = acc + a_ref[dev]
        o_ref[...] = acc

    return pl.pallas_call(body, name="device_sum", out_shape=jax.ShapeDtypeStruct((r, w), F32),
                          in_specs=[pl.BlockSpec(memory_space=pltpu.VMEM)], out_specs=pl.BlockSpec(memory_space=pltpu.VMEM))(a)


def _adamw(w, g_layers, m, v, name):
    shape = w.shape
    cols = shape[-1]
    rows = int(np.prod(shape[:-1]))
    layers = len(g_layers)
    per_layer = rows // layers
    tile = next((cand for cand in (512, 352, 256, 128) if per_layer % cand == 0), per_layer)
    n_l = per_layer // tile
    n_parts = len(g_layers[0])

    def fn(pid, tv, hv, bv):
        wv, mv, vv = tv[:3]
        gv = None
        for l in range(layers):
            gl = None
            for part in tv[3 + l * n_parts:3 + (l + 1) * n_parts]:
                gl = part.astype(F32) if gl is None else gl + part.astype(F32)
            gv = gl if gv is None else jnp.where(pid >= l * n_l, gl, gv)
        mn = ADAM_B1 * mv + (1.0 - ADAM_B1) * gv
        vn = ADAM_B2 * vv + (1.0 - ADAM_B2) * (gv * gv)
        m_hat = mn / (1.0 - ADAM_B1 ** ADAM_STEP)
        v_hat = vn / (1.0 - ADAM_B2 ** ADAM_STEP)
        delta = -ADAM_LR * (m_hat / (jnp.sqrt(v_hat) + ADAM_EPS) + ADAM_WD * wv)
        return [gv, delta, mn, vn], []

    tiled = [(a.reshape(rows, cols), 0, cols) for a in (w, m, v)]
    for l, parts in enumerate(g_layers):
        clamp = functools.partial(lambda i, l: jnp.clip(i - l * n_l, 0, n_l - 1), l=l)
        tiled += [(p.reshape(per_layer, cols), 0, cols, clamp) for p in parts]
    res = _rowwise(fn, name, rows=rows, tile=tile, tiled=tiled, outs=[(cols, F32)] * 4)
    return [r.reshape(shape) for r in res]


def _norm_modulate(x, y, gate, ln_g, scale, shift, name, tile):
    t, d = x.shape

    def fn(pid, tv, hv, bv):
        if y is None:
            xn = tv[0]
            g_ln, sc, sh = bv
        else:
            g_gate, g_ln, sc, sh = bv
            xn = tv[0] + g_gate * tv[1]
        r = lax.rsqrt(jnp.mean(xn * xn, axis=-1, keepdims=True) + EPS)
        h = (xn * r * g_ln) * (1.0 + sc) + sh
        return ([h] if y is None else [xn, h]), []

    if y is None:
        h, = _rowwise(fn, name, rows=t, tile=tile, tiled=[(x, 0, d)], bcast=[ln_g, scale, shift], outs=[(d, BF16)])
        return x, h
    xn, h = _rowwise(fn, name, rows=t, tile=tile, tiled=[(x, 0, d), (y, 0, d)], bcast=[gate, ln_g, scale, shift],
                     outs=[(d, F32), (d, BF16)])
    return xn, h


def _norm_backward(dh, x, dres, ln_g, scale, name, tile):
    t, d = x.shape

    def fn(pid, tv, hv, bv):
        dhv, xv, dr = tv
        g_ln, sc = bv
        r = lax.rsqrt(jnp.mean(xv * xv, axis=-1, keepdims=True) + EPS)
        xn = xv * r
        dxn = dhv * (1.0 + sc) * g_ln
        dx = dr + r * (dxn - xn * jnp.mean(dxn * xn, axis=-1, keepdims=True))
        return [dx], [_colsum(dhv), _colsum(dhv * (xn * g_ln)), _colsum(dhv * (1.0 + sc) * xn)]

    return _rowwise(fn, name, rows=t, tile=tile, tiled=[(dh, 0, d), (x, 0, d), (dres, 0, d)], bcast=[ln_g, scale],
                    outs=[(d, F32)], accs=[(1, d)] * 3)


def _shift_rows(u, halo, k, pid, first_tile_zero):
    rows = lax.broadcasted_iota(jnp.int32, u.shape, 0)
    halo = halo * jnp.where(pid == 0, 0.0, 1.0) if first_tile_zero else halo
    out = pltpu.roll(u, k, axis=0)
    for j in range(k):
        out = jnp.where(rows == j, halo[8 - k + j:8 - k + j + 1, :], out)
    return out


def _shift_rows_up(u, halo, k, pid, n_tiles):
    tile = u.shape[0]
    rows = lax.broadcasted_iota(jnp.int32, u.shape, 0)
    halo = halo * jnp.where(pid == n_tiles - 1, 0.0, 1.0)
    out = pltpu.roll(u, tile - k, axis=0)
    for j in range(k):
        out = jnp.where(rows == tile - k + j, halo[j:j + 1, :], out)
    return out


def _layer_forward(x_in, y_prev, gate_prev, mod, w, l, tile):
    sh1, sc1, g1, sh2, sc2, g2 = mod
    t, d = x_in.shape
    f = w["w_g"].shape[1]
    bd2 = _head_sum_matrix()
    s = {}
    s["x"], s["h"] = _norm_modulate(x_in, y_prev, gate_prev, w["ln1"], sc1, sh1, f"l{l}_norm1", tile)
    p = _matmul([(s["h"], 0, w["w_in"], 0)], "nn", m=t, n=8 * d, k=d, tm=1024, tn=512, out_dtype=F32, name=f"l{l}_in_proj")
    s["p"] = p

    def qk_norm(pid, tv, hv, bv):
        qr, kr, vr = tv
        qg, kg, bd = bv

        def nrm(xv, g):
            r = lax.rsqrt(_head_sums(xv * xv, bd) * (1.0 / HEAD_DIM) + EPS)
            return xv * r * g
        return [nrm(qr, qg) * 0.125, nrm(kr, kg), vr], []

    s["qn"], s["kn"], s["vb"] = _rowwise(qk_norm, f"l{l}_qk_norm", rows=t, tile=tile, tiled=[(p, 0, d), (p, 1, d), (p, 2, d)],
                                         bcast=[w["qg"], w["kg"], bd2], outs=[(d, BF16)] * 3)
    s["ya"] = _attention_fwd(s["qn"], s["kn"], s["vb"], f"l{l}_attention")

    def conv_fwd(pid, tv, hv, bv):
        cb, cc, cx = tv
        u = cc * cx
        hu = hv[0] * hv[1]
        cw = bv[0]
        conv = cw[0:1, :] * _shift_rows(u, hu, 2, pid, True) + cw[1:2, :] * _shift_rows(u, hu, 1, pid, True) + cw[2:3, :] * u
        return [cb * conv], []

    s["yb"], = _rowwise(conv_fwd, f"l{l}_conv", rows=t, tile=tile, tiled=[(p, 3, d), (p, 4, d), (p, 5, d)],
                        halos=[(p, 4, d, "prev"), (p, 5, d, "prev")], bcast=[w["conv"]], outs=[(d, BF16)])
    s["a"] = _matmul([(s["ya"], 0, w["w_a"], 0)], "nn", m=t, n=d, k=d, tm=1024, tn=512, out_dtype=F32, name=f"l{l}_branch_a")
    s["b"] = _matmul([(s["yb"], 0, w["w_b"], 0)], "nn", m=t, n=d, k=d, tm=1024, tn=512, out_dtype=F32, name=f"l{l}_branch_b")

    def merge(pid, tv, hv, bv):
        av, bvv, ga, gb = tv
        return [_sigmoid(ga) * av + _sigmoid(gb) * bvv], []

    s["merged"], = _rowwise(merge, f"l{l}_merge", rows=t, tile=tile, tiled=[(s["a"], 0, d), (s["b"], 0, d), (p, 6, d), (p, 7, d)],
                            outs=[(d, BF16)])
    s["mo"] = _matmul([(s["merged"], 0, w["w_o"], 0)], "nn", m=t, n=d, k=d, tm=1024, tn=512, out_dtype=F32, name=f"l{l}_out_proj")
    s["x2"], s["h2"] = _norm_modulate(s["x"], s["mo"], g1, w["ln2"], sc2, sh2, f"l{l}_norm2", tile)
    fn_tile = f // 2
    s["g"] = _matmul([(s["h2"], 0, w["w_g"], 0)], "nn", m=t, n=f, k=d, tm=512, tn=fn_tile, out_dtype=F32, name=f"l{l}_ffn_gate")
    s["u"] = _matmul([(s["h2"], 0, w["w_u"], 0)], "nn", m=t, n=f, k=d, tm=512, tn=fn_tile, out_dtype=F32, name=f"l{l}_ffn_up")

    def swiglu(pid, tv, hv, bv):
        gv, uv = tv
        return [gv * _sigmoid(gv) * uv], []

    s["s"], = _rowwise(swiglu, f"l{l}_swiglu", rows=t, tile=tile // 2, tiled=[(s["g"], 0, f), (s["u"], 0, f)], outs=[(f, BF16)])
    s["f"] = _matmul([(s["s"], 0, w["w_d"], 0)], "nn", m=t, n=d, k=f, tm=512, tn=512, out_dtype=F32, name=f"l{l}_ffn_down")
    return s


def _layer_backward(dx3, s, mod, w, l, tile):
    sh1, sc1, g1, sh2, sc2, g2 = mod
    t, d = dx3.shape
    f = w["w_g"].shape[1]
    p = s["p"]
    bd2 = _head_sum_matrix()
    n_tiles = t // tile
    grads = {}

    def gate_bwd(pid, tv, hv, bv):
        return [tv[0] * bv[0]], [_colsum(tv[0] * tv[1])]

    df, dg2 = _rowwise(gate_bwd, f"l{l}_bwd_gate2", rows=t, tile=tile, tiled=[(dx3, 0, d), (s["f"], 0, d)], bcast=[g2],
                       outs=[(d, BF16)], accs=[(1, d)])
    grads["w_d"] = _matmul([(s["s"], 0, df, 0)], "tn", m=f, n=d, k=t, tm=512, tn=512, out_dtype=BF16, name=f"l{l}_dw_down")
    ds = _matmul([(df, 0, w["w_d"], 0)], "nt", m=t, n=f, k=d, tm=512, tn=f // 2, out_dtype=F32, name=f"l{l}_d_swiglu")

    def swiglu_bwd(pid, tv, hv, bv):
        dsv, gv, uv = tv
        sig = _sigmoid(gv)
        return [dsv * uv * (sig * (1.0 + gv * (1.0 - sig))), dsv * (gv * sig)], []

    dgt, dup = _rowwise(swiglu_bwd, f"l{l}_bwd_swiglu", rows=t, tile=tile // 2, tiled=[(ds, 0, f), (s["g"], 0, f), (s["u"], 0, f)],
                        outs=[(f, BF16)] * 2)
    grads["w_g"] = _matmul([(s["h2"], 0, dgt, 0)], "tn", m=d, n=f, k=t, tm=512, tn=f // 2, out_dtype=BF16, name=f"l{l}_dw_gate")
    grads["w_u"] = _matmul([(s["h2"], 0, dup, 0)], "tn", m=d, n=f, k=t, tm=512, tn=f // 2, out_dtype=BF16, name=f"l{l}_dw_up")
    dh2 = _matmul([(dgt, 0, w["w_g"], 0), (dup, 0, w["w_u"], 0)], "nt", m=t, n=d, k=f, tm=512, tn=512, out_dtype=F32,
                  name=f"l{l}_dh2")
    dx2, dsh2, dsc2, grads["ln2"] = _norm_backward(dh2, s["x2"], dx3, w["ln2"], sc2, f"l{l}_bwd_norm2", tile)

    dmo, dg1 = _rowwise(gate_bwd, f"l{l}_bwd_gate1", rows=t, tile=tile, tiled=[(dx2, 0, d), (s["mo"], 0, d)], bcast=[g1],
                        outs=[(d, BF16)], accs=[(1, d)])
    grads["w_o"] = _matmul([(s["merged"], 0, dmo, 0)], "tn", m=d, n=d, k=t, tm=512, tn=512, out_dtype=BF16, name=f"l{l}_dw_out")
    dmerged = _matmul([(dmo, 0, w["w_o"], 0)], "nt", m=t, n=d, k=d, tm=1024, tn=512, out_dtype=F32, name=f"l{l}_d_merged")

    def merge_bwd(pid, tv, hv, bv):
        dm, av, bvv, ga, gb = tv
        sa, sb = _sigmoid(ga), _sigmoid(gb)
        return [dm * sa, dm * sb, dm * av * (sa * (1.0 - sa)), dm * bvv * (sb * (1.0 - sb))], []

    d_a, d_b, dga, dgb = _rowwise(merge_bwd, f"l{l}_bwd_merge", rows=t, tile=tile,
                                  tiled=[(dmerged, 0, d), (s["a"], 0, d), (s["b"], 0, d), (p, 6, d), (p, 7, d)], outs=[(d, BF16)] * 4)
    grads["w_a"] = _matmul([(s["ya"], 0, d_a, 0)], "tn", m=d, n=d, k=t, tm=512, tn=512, out_dtype=BF16, name=f"l{l}_dw_a")
    grads["w_b"] = _matmul([(s["yb"], 0, d_b, 0)], "tn", m=d, n=d, k=t, tm=512, tn=512, out_dtype=BF16, name=f"l{l}_dw_b")
    dya = _matmul([(d_a, 0, w["w_a"], 0)], "nt", m=t, n=d, k=d, tm=1024, tn=512, out_dtype=BF16, name=f"l{l}_d_ya")
    dyb = _matmul([(d_b, 0, w["w_b"], 0)], "nt", m=t, n=d, k=d, tm=1024, tn=512, out_dtype=F32, name=f"l{l}_d_yb")

    def conv_bwd(pid, tv, hv, bv):
        dy, cb, cc, cx = tv
        cw = bv[0]
        u, hu = cc * cx, hv[0] * hv[1]
        u1, u2 = _shift_rows(u, hu, 1, pid, True), _shift_rows(u, hu, 2, pid, True)
        conv = cw[0:1, :] * u2 + cw[1:2, :] * u1 + cw[2:3, :] * u
        dconv, hd = dy * cb, hv[2] * hv[3]
        du = (cw[2:3, :] * dconv + cw[1:2, :] * _shift_rows_up(dconv, hd, 1, pid, n_tiles)
              + cw[0:1, :] * _shift_rows_up(dconv, hd, 2, pid, n_tiles))
        return [dy * conv, du * cx, du * cc], [_colsum(dconv * u2), _colsum(dconv * u1), _colsum(dconv * u)]

    dcb, dcc, dcx, dcw0, dcw1, dcw2 = _rowwise(
        conv_bwd, f"l{l}_bwd_conv", rows=t, tile=tile, tiled=[(dyb, 0, d), (p, 3, d), (p, 4, d), (p, 5, d)],
        halos=[(p, 4, d, "prev"), (p, 5, d, "prev"), (dyb, 0, d, "next"), (p, 3, d, "next")], bcast=[w["conv"]],
        outs=[(d, BF16)] * 3, accs=[(1, d)] * 3)
    grads["conv"] = jnp.concatenate([dcw0, dcw1, dcw2], axis=0)

    dqs, dkn, dv = _attention_bwd(s["qn"], s["kn"], s["vb"], dya, f"l{l}_bwd_attention")

    def qk_norm_bwd(pid, tv, hv, bv):
        dq, dk, qr, kr, dvv = tv
        qg, kg, bd = bv

        def bwd(dy, xv, g):
            r = lax.rsqrt(_head_sums(xv * xv, bd) * (1.0 / HEAD_DIM) + EPS)
            yv = xv * r
            dyn = dy * g
            dx = r * (dyn - yv * (_head_sums(dyn * yv, bd) * (1.0 / HEAD_DIM)))
            return dx, _colsum(dy * yv)

        dxq, dgq = bwd(dq * 0.125, qr, qg)
        dxk, dgk = bwd(dk, kr, kg)
        return [dxq, dxk, dvv], [dgq, dgk]

    dqr, dkr, dvb, grads["qg"], grads["kg"] = _rowwise(
        qk_norm_bwd, f"l{l}_bwd_qk_norm", rows=t, tile=tile, tiled=[(dqs, 0, d), (dkn, 0, d), (p, 0, d), (p, 1, d), (dv, 0, d)],
        bcast=[w["qg"], w["kg"], bd2], outs=[(d, BF16)] * 3, accs=[(1, d)] * 2)

    dp = [dqr, dkr, dvb, dcb, dcc, dcx, dga, dgb]
    grads["w_in"] = [_matmul([(s["h"], 0, dpk, 0)], "tn", m=d, n=d, k=t, tm=512, tn=512, out_dtype=BF16, name=f"l{l}_dw_in{k}")
                     for k, dpk in enumerate(dp)]
    dh = _matmul([(dpk, 0, w["w_in"], k) for k, dpk in enumerate(dp)], "nt", m=t, n=d, k=d, tm=512, tn=512, out_dtype=F32,
                 name=f"l{l}_dh")
    dx, dsh1, dsc1, grads["ln1"] = _norm_backward(dh, s["x"], dx2, w["ln1"], sc1, f"l{l}_bwd_norm1", tile)
    return dx, grads, [dsh1, dsc1, dg1, dsh2, dsc2, dg2]


_BIG = ["w_in", "w_branch_a", "w_branch_b", "w_out", "w_ffn_gate", "w_ffn_up", "w_ffn_down"]
_SHORT = dict(w_in="w_in", w_branch_a="w_a", w_branch_b="w_b", w_out="w_o", w_ffn_gate="w_g", w_ffn_up="w_u", w_ffn_down="w_d")
_COL_SHARDED = {"w_in", "w_ffn_gate", "w_ffn_up"}


def kernel(x, c, ada_w, ada_b, ln1_g, w_in, q_norm_g, k_norm_g, conv_w, w_branch_a, w_branch_b, w_out, ln2_g, w_ffn_gate, w_ffn_up, w_ffn_down, loss_target, m_ada_w, m_ada_b, m_ln1_g, m_w_in, m_q_norm_g, m_k_norm_g, m_conv_w, m_w_branch_a, m_w_branch_b, m_w_out, m_ln2_g, m_w_ffn_gate, m_w_ffn_up, m_w_ffn_down, v_ada_w, v_ada_b, v_ln1_g, v_w_in, v_q_norm_g, v_k_norm_g, v_conv_w, v_w_branch_a, v_w_branch_b, v_w_out, v_ln2_g, v_w_ffn_gate, v_w_ffn_up, v_w_ffn_down):
    weights = dict(ada_w=ada_w, ada_b=ada_b, ln1_g=ln1_g, w_in=w_in, q_norm_g=q_norm_g, k_norm_g=k_norm_g, conv_w=conv_w,
                   w_branch_a=w_branch_a, w_branch_b=w_branch_b, w_out=w_out, ln2_g=ln2_g, w_ffn_gate=w_ffn_gate,
                   w_ffn_up=w_ffn_up, w_ffn_down=w_ffn_down)
    m_in = dict(ada_w=m_ada_w, ada_b=m_ada_b, ln1_g=m_ln1_g, w_in=m_w_in, q_norm_g=m_q_norm_g, k_norm_g=m_k_norm_g,
                conv_w=m_conv_w, w_branch_a=m_w_branch_a, w_branch_b=m_w_branch_b, w_out=m_w_out, ln2_g=m_ln2_g,
                w_ffn_gate=m_w_ffn_gate, w_ffn_up=m_w_ffn_up, w_ffn_down=m_w_ffn_down)
    v_in = dict(ada_w=v_ada_w, ada_b=v_ada_b, ln1_g=v_ln1_g, w_in=v_w_in, q_norm_g=v_q_norm_g, k_norm_g=v_k_norm_g,
                conv_w=v_conv_w, w_branch_a=v_w_branch_a, w_branch_b=v_w_branch_b, w_out=v_w_out, ln2_g=v_ln2_g,
                w_ffn_gate=v_w_ffn_gate, w_ffn_up=v_w_ffn_up, w_ffn_down=v_w_ffn_down)
    names = list(weights)

    mx, my, mc = lax.axis_index("x"), lax.axis_index("y"), lax.axis_index("c")
    me = 4 * mx + 2 * my + mc
    xs, target = x[0], loss_target[0]
    t, d = xs.shape
    depth = ada_w.shape[0]
    mod_cols = ada_w.shape[2]
    conv_cols = conv_w.shape[2]
    row_w = 1024
    tile = 512 if t % 512 == 0 else t

    small = jnp.concatenate([c.reshape(-1), conv_w.reshape(-1)])
    small_n = -(-small.shape[0] // row_w) * row_w
    small = jnp.pad(small, (0, small_n - small.shape[0])).reshape(-1, row_w)
    small_all = _all_gather([small], "gather_cond")[0].reshape(N_DEV, -1)
    c_all = small_all[:, :d]
    conv_full = jnp.transpose(small_all[:, d:d + depth * 3 * conv_cols].reshape(N_DEV, depth, 3, conv_cols), (1, 2, 0, 3)
                              ).reshape(depth, 3, N_DEV * conv_cols)

    ada_b_cols = lax.dynamic_slice_in_dim(ada_b, me * mod_cols, mod_cols, axis=1)
    mod_part = _ada_forward(c_all, ada_w, ada_b_cols)
    mod_all = _all_gather([mod_part.reshape(-1, row_w)], "gather_mod")[0].reshape(N_DEV, depth, N_DEV, mod_cols)
    mod_mine = lax.dynamic_index_in_dim(mod_all, me, axis=2, keepdims=False)
    mod = jnp.transpose(mod_mine, (1, 0, 2)).reshape(depth, 6, 1, d)

    f_shard = w_ffn_gate.shape[2]
    f_pad = -(-f_shard // LANES) * LANES - f_shard
    pads = dict(w_ffn_gate=((0, 0), (0, 0), (0, f_pad)), w_ffn_up=((0, 0), (0, 0), (0, f_pad)), w_ffn_down=((0, 0), (0, f_pad), (0, 0)))
    shards = [weights[n].astype(BF16) for n in _BIG]
    shards = [jnp.pad(a, pads[n]) if n in pads else a for n, a in zip(_BIG, shards)]
    axes = [1 if n in _COL_SHARDED else 0 for n in _BIG]
    rest = [a[l] for l in range(1, depth) for a in shards]
    rest_axes = axes * (depth - 1)
    first = _all_gather([a[0] for a in shards], "gather_weights_first", axes, place=rest, place_axes=rest_axes, after=[mod_all])
    first, rest_land = first[:len(_BIG)], first[len(_BIG):]
    rest_copies = functools.partial(_gather_copies, shapes=[a.shape for a in rest], axes=rest_axes)
    send_sems, recv_sems, rest, rest_land, started = _copies_start(
        rest, None, N_DEV - 1, rest_copies, "gather_weights_rest_start", lands=rest_land)

    def layer_weights(full, l):
        wl = {_SHORT[n]: a for n, a in zip(_BIG, full)}
        wl["ln1"], wl["ln2"] = ln1_g[l][None], ln2_g[l][None]
        wl["qg"] = jnp.tile(q_norm_g[l], d // HEAD_DIM)[None]
        wl["kg"] = jnp.tile(k_norm_g[l], d // HEAD_DIM)[None]
        wl["conv"] = conv_full[l]
        return wl

    layer_w = [layer_weights(first, 0)]
    layer_w[0]["ln1"] = layer_w[0]["ln1"] + started[:1, :1]

    saved = []
    x_cur, y_prev, gate_prev = xs, None, None
    for l in range(depth):
        if l == 1:
            _, landed = _copies_wait(send_sems, recv_sems, rest, rest_land, saved[0]["f"], rest_copies, "gather_weights_rest_wait")
            layer_w += [layer_weights(landed[(k - 1) * len(_BIG):k * len(_BIG)], k) for k in range(1, depth)]
        mods = [mod[l, k] for k in range(6)]
        s = _layer_forward(x_cur, y_prev, gate_prev, mods, layer_w[l], l, tile)
        saved.append(s)
        x_cur, y_prev, gate_prev = s["x2"], s["f"], mods[5]

    def loss_head(pid, tv, hv, bv):
        diff = tv[0] + bv[0] * tv[1] - tv[2]
        return [diff * (1.0 / d)], [_colsum(diff * diff) * (0.5 / d)]

    dx, loss_cols = _rowwise(loss_head, "loss_head", rows=t, tile=tile, tiled=[(x_cur, 0, d), (y_prev, 0, d), (target, 0, d)],
                             bcast=[gate_prev], outs=[(d, F32)], accs=[(1, d)])

    def by_owner(n, full):
        if n == "w_in":
            return jnp.stack(full)
        if n in _COL_SHARDED:
            return jnp.transpose(full.reshape(full.shape[0], N_DEV, -1), (1, 0, 2))
        return full.reshape(N_DEV, -1, full.shape[1])

    layer_g, dmods, parts, recv2, pending = [None] * depth, [None] * depth, [None] * depth, [None] * depth, {}
    started = None
    for l in reversed(range(depth)):
        mods = [mod[l, k] for k in range(6)]
        if started is not None:
            mods[5] = mods[5] + started[:1, :1]
        dx, layer_g[l], dmods[l] = _layer_backward(dx, saved[l], mods, layer_w[l], l, tile)
        tensors = [by_owner(n, layer_g[l][_SHORT[n]]) for n in _BIG]
        recv1 = _exchange_sibling(tensors, f"rs_exchange_sibling_l{l}")
        parts[l] = [_add_sibling(a, r, f"rs_add_sibling_l{l}_{i}") for i, (a, r) in enumerate(zip(tensors, recv1))]
        if l > 0:
            *pending[l], started = _copies_start(parts[l], [(3,) + a.shape[1:] for a in parts[l]], 3, _chip_copies,
                                                 f"rs_exchange_chips_l{l}_start")
        else:
            recv2[l] = _exchange_chips(parts[l], "rs_exchange_chips_l0")
    for l, (s_sems, r_sems, srcs, lands) in pending.items():
        parts[l], recv2[l] = _copies_wait(s_sems, r_sems, srcs, lands, dx, _chip_copies, f"rs_exchange_chips_l{l}_wait")
    grad_x = dx[None]

    pieces = [jnp.concatenate(dmods[l], axis=1) for l in range(depth)]
    for key in ("ln1", "ln2", "qg", "kg"):
        pieces += [layer_g[l][key] for l in range(depth)]
    pieces += [layer_g[l]["conv"].reshape(1, -1) for l in range(depth)]
    pieces.append(loss_cols)
    part_small = jnp.concatenate(pieces, axis=1).reshape(-1, row_w)
    part_all = _all_gather([part_small], "gather_small_grads")[0]
    summed = _device_sum(part_all).reshape(-1)
    n_mod = depth * 6 * d
    dmod_all = part_all.reshape(N_DEV, -1)[:, :n_mod].reshape(N_DEV, depth, 6 * d)
    dmod_cols = jnp.transpose(lax.dynamic_slice_in_dim(dmod_all, me * mod_cols, mod_cols, axis=2), (1, 0, 2))
    g = {"ada_w": _ada_backward(c_all, dmod_cols), "ada_b": summed[:n_mod].reshape(depth, 6 * d)}
    off = n_mod
    g["ln1_g"] = summed[off:off + depth * d].reshape(depth, d)
    g["ln2_g"] = summed[off + depth * d:off + 2 * depth * d].reshape(depth, d)
    g["q_norm_g"] = summed[off + 2 * depth * d:off + 3 * depth * d].reshape(depth, d // HEAD_DIM, HEAD_DIM).sum(axis=1)
    g["k_norm_g"] = summed[off + 3 * depth * d:off + 4 * depth * d].reshape(depth, d // HEAD_DIM, HEAD_DIM).sum(axis=1)
    off += 4 * depth * d
    conv_g = summed[off:off + depth * 3 * d].reshape(depth, 3, N_DEV, conv_cols)
    g["conv_w"] = lax.dynamic_index_in_dim(conv_g, me, axis=2, keepdims=False)
    off += depth * 3 * d
    loss = jnp.sum(summed[off:off + d])

    def shard_of(n, a):
        if n in ("w_ffn_gate", "w_ffn_up"):
            return a[:, :f_shard]
        return a[:f_shard] if n == "w_ffn_down" else a

    outs = {}
    for k, n in enumerate(_BIG):
        g_layers = [[shard_of(n, a) for a in (parts[l][k][0], *recv2[l][k])] for l in range(depth)]
        outs[n] = _adamw(weights[n], g_layers, m_in[n], v_in[n], f"adamw_{n}")
    for n in names:
        if n not in outs:
            outs[n] = _adamw(weights[n], [[g[n]]], m_in[n], v_in[n], f"adamw_{n}")
    return (loss, grad_x, *[outs[n][0] for n in names], *[outs[n][1] for n in names], *[outs[n][2] for n in names],
            *[outs[n][3] for n in names])
```
